```python
import jax, jax.numpy as jnp
from jax import lax
import numpy as np

D_MODEL = 2048
BATCH = 4
SEQ = 2048
DEPTH = 2

GRID_W = 64
CTX_LEN = 256
EPS = 1e-6
D_FF = (11 * D_MODEL) // 4
N_MOD = 9
BRANCH_W = D_MODEL // 2
N_BRANCH = 3
M_HEADS = 4
M_HEAD_DIM = BRANCH_W // M_HEADS
M_CHUNK = 64
ROPE_BASE = 10000.0
NA_HEADS = 8
NA_HEAD_DIM = BRANCH_W // NA_HEADS
NA_KH = 8
NA_KW = 16
LRU_BLOCKS = 8
LRU_BLOCK_DIM = BRANCH_W // LRU_BLOCKS
LRU_CONV = 4
LRU_C = 8.0
IN_SPLITS = (BRANCH_W, BRANCH_W, BRANCH_W, BRANCH_W, 4 * M_HEADS, BRANCH_W, BRANCH_W, BRANCH_W, BRANCH_W, BRANCH_W, N_BRANCH * D_MODEL)
P_IN = sum(IN_SPLITS)

kernel_name = "hybrid_mlstm_natten_rglru_dit_block"


def rmsnorm(x, g):
    xf = x.astype(jnp.float32)
    y = xf * lax.rsqrt(jnp.mean(xf * xf, axis=-1, keepdims=True) + EPS)
    return (y * g.astype(jnp.float32)).astype(x.dtype)


def adaln(x, g, shift, scale):
    return rmsnorm(x, g) * (1 + scale) + shift


def swiglu(h, w_in, w_out):
    gte, up = jnp.split(h @ w_in, 2, axis=-1)
    return (jax.nn.silu(gte) * up) @ w_out


def split_cols(p):
    idx, acc = [], 0
    for s in IN_SPLITS[:-1]:
        acc += s
        idx.append(acc)
    return jnp.split(p, idx, axis=-1)


def rope_2d(t, prow, pcol):
    half = t.shape[-1] // 2

    def rot(u, pos):
        nf = u.shape[-1] // 2
        inv = ROPE_BASE ** (-jnp.arange(nf, dtype=jnp.float32) / nf)
        ang = pos.astype(jnp.float32)[:, None] * inv[None, :]
        cos = jnp.cos(ang)[None, :, None, :]
        sin = jnp.sin(ang)[None, :, None, :]
        u1, u2 = u[..., :nf], u[..., nf:]
        return jnp.concatenate([u1 * cos - u2 * sin, u1 * sin + u2 * cos], axis=-1)

    return jnp.concatenate([rot(t[..., :half], prow), rot(t[..., half:], pcol)], axis=-1)


def _to_chunks(t):
    b, l, h = t.shape[:3]
    t = t.reshape((b, l // M_CHUNK, M_CHUNK, h) + t.shape[3:])
    return jnp.swapaxes(jnp.moveaxis(t, 1, 0), 2, 3)


def mlstm_scan(q, k, v, i_pre, f_pre, state, need_out):
    b, l, h, dh = q.shape
    tril = jnp.tril(jnp.ones((M_CHUNK, M_CHUNK), dtype=bool))

    def step(carry, inp):
        c_mem, n_mem, m_prev = carry
        qc, kc, vc, ic, lf = inp
        cum = jnp.cumsum(lf, axis=-1)
        logw = jnp.where(tril, cum[..., :, None] - cum[..., None, :] + ic[..., None, :], -jnp.inf)
        m_row = jnp.maximum(cum + m_prev[..., None], jnp.max(logw, axis=-1))
        m_new = m_row[..., -1]
        w_state = jnp.exp(cum[..., -1:] - cum + ic - m_new[..., None])
        decay = jnp.exp(cum[..., -1] + m_prev - m_new)
        c_new = decay[..., None, None] * c_mem + jnp.einsum('bhs,bhsv,bhsk->bhvk', w_state, vc, kc)
        n_new = decay[..., None] * n_mem + jnp.einsum('bhs,bhsk->bhk', w_state, kc)
        if not need_out:
            return (c_new, n_new, m_new), None
        inter = jnp.exp(cum + m_prev[..., None] - m_row)
        sc = jnp.einsum('bhtk,bhsk->bhts', qc, kc) * jnp.exp(logw - m_row[..., None])
        num = inter[..., None] * jnp.einsum('bhvk,bhtk->bhtv', c_mem, qc) + jnp.einsum('bhts,bhsv->bhtv', sc, vc)
        den = inter * jnp.einsum('bhk,bhtk->bht', n_mem, qc) + jnp.sum(sc, axis=-1)
        hid = num / jnp.maximum(jnp.abs(den), jnp.exp(-m_row))[..., None]
        return (c_new, n_new, m_new), hid

    xs = (_to_chunks(q), _to_chunks(k), _to_chunks(v), _to_chunks(i_pre), _to_chunks(jax.nn.log_sigmoid(f_pre)))
    state, hs = lax.scan(step, state, xs)
    if need_out:
        hs = jnp.transpose(hs, (1, 0, 3, 2, 4)).reshape(b, l, h, dh)
    return hs, state


def _flip(t):
    return jnp.flip(t, axis=1)


def mlstm_branch(q, k, v, o, g, qc, kc, vc, oc, gc, b_i, b_f, gn, prow, pcol, need_ctx_out):
    f32 = jnp.float32
    b, l = q.shape[:2]
    lc = qc.shape[1]
    heads = lambda t: t.reshape(t.shape[0], t.shape[1], M_HEADS, M_HEAD_DIM).astype(f32)
    kscale = M_HEAD_DIM ** -0.5
    q = rope_2d(heads(q), prow, pcol)
    k = rope_2d(heads(k), prow, pcol) * kscale
    v = heads(v)
    qc, kc, vc = heads(qc), heads(kc) * kscale, heads(vc)
    g = g.astype(f32).reshape(b, l, 2, 2, M_HEADS)
    gc = gc.astype(f32).reshape(b, lc, 2, 2, M_HEADS)
    zero = (jnp.zeros((b, M_HEADS, M_HEAD_DIM, M_HEAD_DIM), f32), jnp.zeros((b, M_HEADS, M_HEAD_DIM), f32),
            jnp.full((b, M_HEADS), -jnp.inf, f32))
    outs, outs_c = [], []
    for d in range(2):
        fl = _flip if d == 1 else (lambda t: t)
        hc_d, st = mlstm_scan(fl(qc), fl(kc), fl(vc), fl(gc[:, :, d, 0] + b_i[d]), fl(gc[:, :, d, 1] + b_f[d]), zero, need_ctx_out)
        hx_d, _ = mlstm_scan(fl(q), fl(k), fl(v), fl(g[:, :, d, 0] + b_i[d]), fl(g[:, :, d, 1] + b_f[d]), st, True)
        outs.append(fl(hx_d))
        if need_ctx_out:
            outs_c.append(fl(hc_d))

    def finish(hsum, og):
        hn = hsum * lax.rsqrt(jnp.mean(hsum * hsum, axis=-1, keepdims=True) + EPS)
        hn = hn.reshape(hn.shape[0], hn.shape[1], BRANCH_W) * gn.astype(f32)
        return (jax.nn.sigmoid(og.astype(f32)) * hn).astype(og.dtype)

    y = finish(outs[0] + outs[1], o)
    yc = finish(outs_c[0] + outs_c[1], oc) if need_ctx_out else None
    return y, yc


def na_branch(q, k, v, qc, kc, vc, rpb, need_ctx_out):
    f32 = jnp.float32
    b, l = q.shape[:2]
    rows = l // GRID_W
    kh = min(NA_KH, rows)
    scale = NA_HEAD_DIM ** -0.5
    grid = lambda t: t.reshape(b, rows, GRID_W, NA_HEADS, NA_HEAD_DIM)
    heads = lambda t: t.reshape(t.shape[0], t.shape[1], NA_HEADS, NA_HEAD_DIM)
    qg, kg, vg = grid(q), grid(k), grid(v)
    qc, kc, vc = heads(qc), heads(kc), heads(vc)
    r = jnp.arange(rows)
    row_idx = jnp.clip(r - kh // 2, 0, rows - kh)[:, None] + jnp.arange(kh)[None, :]
    k_win = kg[:, row_idx]
    v_win = vg[:, row_idx]
    col = jnp.arange(GRID_W)
    col_start = jnp.clip(col - NA_KW // 2, 0, GRID_W - NA_KW)
    col_ok = (col[None, :] >= col_start[:, None]) & (col[None, :] < col_start[:, None] + NA_KW)
    dr = row_idx - r[:, None] + (NA_KH - 1)
    dc = jnp.clip(col[None, :] - col[:, None], -(NA_KW - 1), NA_KW - 1) + (NA_KW - 1)
    bias = rpb[:, dr[:, None, :, None], dc[None, :, None, :]].astype(f32)
    bias = jnp.where(col_ok[None, None, :, None, :], bias, -jnp.inf)
    s_win = jnp.einsum('brqhd,brjkhd->bhrqjk', qg, k_win).astype(f32) * scale + bias
    s_ctx = jnp.einsum('brqhd,bchd->bhrqc', qg, kc).astype(f32) * scale
    nwin = kh * GRID_W
    s = jnp.concatenate([s_win.reshape(b, NA_HEADS, rows, GRID_W, nwin), s_ctx], axis=-1)
    p = jax.nn.softmax(s, axis=-1).astype(v.dtype)
    p_win = p[..., :nwin].reshape(b, NA_HEADS, rows, GRID_W, kh, GRID_W)
    out = jnp.einsum('bhrqjk,brjkhd->brqhd', p_win, v_win) + jnp.einsum('bhrqc,bchd->brqhd', p[..., nwin:], vc)
    out = out.reshape(b, l, BRANCH_W)
    if not need_ctx_out:
        return out, None
    sc = jnp.einsum('bqhd,bkhd->bhqk', qc, kc).astype(f32) * scale
    pc = jax.nn.softmax(sc, axis=-1).astype(vc.dtype)
    outc = jnp.einsum('bhqk,bkhd->bqhd', pc, vc).reshape(b, qc.shape[1], BRANCH_W)
    return out, outc


def dwconv(x, w, bias):
    l = x.shape[1]
    xp = jnp.pad(x, ((0, 0), ((LRU_CONV - 1) // 2, LRU_CONV // 2), (0, 0)))
    y = xp[:, 0:l] * w[0]
    for j in range(1, LRU_CONV):
        y = y + xp[:, j:j + l] * w[j]
    return y + bias


def rglru_coeffs(xb, w_a, b_a, w_x, b_x, lam):
    f32 = jnp.float32
    b, l = xb.shape[:2]
    xf = xb.astype(f32)
    xg = xf.reshape(b, l, LRU_BLOCKS, LRU_BLOCK_DIM)
    r = jax.nn.sigmoid(jnp.einsum('blgi,ngij->blngj', xg, w_a.astype(f32)).reshape(b, l, 2, BRANCH_W) + b_a.astype(f32))
    i = jax.nn.sigmoid(jnp.einsum('blgi,ngij->blngj', xg, w_x.astype(f32)).reshape(b, l, 2, BRANCH_W) + b_x.astype(f32))
    log_a = -LRU_C * r * jax.nn.softplus(-lam.astype(f32))
    a = jnp.exp(log_a)
    u = jnp.sqrt(-jnp.expm1(2.0 * log_a)) * i * xf[:, :, None, :]
    return a, u


def linear_scan(a, u, h0):
    acum, ucum = lax.associative_scan(lambda e1, e2: (e1[0] * e2[0], e2[0] * e1[1] + e2[1]), (a, u), axis=1)
    return ucum + acum * h0[:, None, :]


def lru_branch(xb, gate, xbc, gatec, conv_w, conv_b, w_a, b_a, w_x, b_x, lam, need_ctx_out):
    f32 = jnp.float32
    b = xb.shape[0]
    a, u = rglru_coeffs(dwconv(xb, conv_w, conv_b), w_a, b_a, w_x, b_x, lam)
    ac, uc = rglru_coeffs(dwconv(xbc, conv_w, conv_b), w_a, b_a, w_x, b_x, lam)
    zero = jnp.zeros((b, BRANCH_W), f32)
    hc_f = linear_scan(ac[:, :, 0], uc[:, :, 0], zero)
    hc_b = _flip(linear_scan(_flip(ac[:, :, 1]), _flip(uc[:, :, 1]), zero))
    h_f = linear_scan(a[:, :, 0], u[:, :, 0], hc_f[:, -1])
    h_b = _flip(linear_scan(_flip(a[:, :, 1]), _flip(u[:, :, 1]), hc_b[:, 0]))
    y = ((h_f + h_b) * jax.nn.gelu(gate.astype(f32))).astype(gate.dtype)
    yc = ((hc_f + hc_b) * jax.nn.gelu(gatec.astype(f32))).astype(gatec.dtype) if need_ctx_out else None
    return y, yc


def merge_branches(ys, gcols, w_branch, w_out):
    b, l = gcols.shape[:2]
    stacked = jnp.stack(ys, axis=2)
    proj = jnp.einsum('blnw,nwd->blnd', stacked, w_branch)
    gates = jax.nn.sigmoid(gcols.reshape(b, l, N_BRANCH, D_MODEL))
    return jnp.sum(gates * proj, axis=2) @ w_out


def mixer(hx, hc, w_in, b_i, b_f, gn, rpb, conv_w, conv_b, w_a, b_a, w_x, b_x, lam, w_branch, w_out, prow, pcol, need_ctx_out):
    mq, mk, mv, mo, mg, nq, nk, nv, lx, lg, gx = split_cols(hx @ w_in)
    mqc, mkc, mvc, moc, mgc, nqc, nkc, nvc, lxc, lgc, gxc = split_cols(hc @ w_in)
    y_m, yc_m = mlstm_branch(mq, mk, mv, mo, mg, mqc, mkc, mvc, moc, mgc, b_i, b_f, gn, prow, pcol, need_ctx_out)
    y_n, yc_n = na_branch(nq, nk, nv, nqc, nkc, nvc, rpb, need_ctx_out)
    y_l, yc_l = lru_branch(lx, lg, lxc, lgc, conv_w, conv_b, w_a, b_a, w_x, b_x, lam, need_ctx_out)
    y = merge_branches([y_m, y_n, y_l], gx, w_branch, w_out)
    yc = merge_branches([yc_m, yc_n, yc_l], gxc, w_branch, w_out) if need_ctx_out else None
    return y, yc


def setup_inputs(seed: int = 0) -> dict:
    key = jax.random.key(seed)
    ks = jax.random.split(key, 28)
    f32 = jnp.float32

    def nrm(k, shape, s):
        return jax.random.normal(k, shape, f32) * s

    def gain(k, shape):
        return 1.0 + nrm(k, shape, 0.1)

    a0 = jax.random.uniform(ks[26], (DEPTH, 2, BRANCH_W), f32, 0.9, 0.999)
    return {
        "x": nrm(ks[0], (BATCH, SEQ, D_MODEL), 1.0),
        "c": nrm(ks[1], (BATCH, D_MODEL), 1.0),
        "ctx": nrm(ks[2], (BATCH, CTX_LEN, D_MODEL), 1.0),
        "c_ctx": nrm(ks[3], (D_MODEL,), 1.0),
        "w_mod": nrm(ks[4], (DEPTH, D_MODEL, N_MOD * D_MODEL), 0.5 * D_MODEL ** -0.5),
        "b_mod": nrm(ks[5], (DEPTH, N_MOD * D_MODEL), 0.02),
        "norm_ffn1": gain(ks[6], (DEPTH, D_MODEL)),
        "norm_mix": gain(ks[7], (DEPTH, D_MODEL)),
        "norm_ffn2": gain(ks[8], (DEPTH, D_MODEL)),
        "ffn1_w_in": nrm(ks[9], (DEPTH, D_MODEL, 2 * D_FF), D_MODEL ** -0.5),
        "ffn1_w_out": nrm(ks[10], (DEPTH, D_FF, D_MODEL), D_FF ** -0.5),
        "ffn2_w_in": nrm(ks[11], (DEPTH, D_MODEL, 2 * D_FF), D_MODEL ** -0.5),
        "ffn2_w_out": nrm(ks[12], (DEPTH, D_FF, D_MODEL), D_FF ** -0.5),
        "w_in": nrm(ks[13], (DEPTH, D_MODEL, P_IN), D_MODEL ** -0.5),
        "mlstm_b_i": nrm(ks[14], (DEPTH, 2, M_HEADS), 0.1),
        "mlstm_b_f": jnp.linspace(3.0, 6.0, M_HEADS, dtype=f32) + nrm(ks[15], (DEPTH, 2, M_HEADS), 0.1),
        "mlstm_gn": gain(ks[16], (DEPTH, BRANCH_W)),
        "na_rpb": nrm(ks[17], (DEPTH, NA_HEADS, 2 * NA_KH - 1, 2 * NA_KW - 1), 0.5),
        "lru_conv_w": nrm(ks[18], (DEPTH, LRU_CONV, BRANCH_W), LRU_CONV ** -0.5),
        "lru_conv_b": nrm(ks[19], (DEPTH, BRANCH_W), 0.02),
        "lru_w_a": nrm(ks[20], (DEPTH, 2, LRU_BLOCKS, LRU_BLOCK_DIM, LRU_BLOCK_DIM), LRU_BLOCK_DIM ** -0.5),
        "lru_b_a": nrm(ks[21], (DEPTH, 2, BRANCH_W), 0.02),
        "lru_w_x": nrm(ks[22], (DEPTH, 2, LRU_BLOCKS, LRU_BLOCK_DIM, LRU_BLOCK_DIM), LRU_BLOCK_DIM ** -0.5),
        "lru_b_x": nrm(ks[23], (DEPTH, 2, BRANCH_W), 0.02),
        "lru_lambda": jnp.log(a0) - jnp.log1p(-a0),
        "w_branch": nrm(ks[24], (DEPTH, N_BRANCH, BRANCH_W, D_MODEL), BRANCH_W ** -0.5),
        "w_out": nrm(ks[25], (DEPTH, D_MODEL, D_MODEL), D_MODEL ** -0.5),
        "norm_final": gain(ks[27], (D_MODEL,)),
    }


def reference(x, c, ctx, c_ctx, w_mod, b_mod, norm_ffn1, norm_mix, norm_ffn2, ffn1_w_in, ffn1_w_out, ffn2_w_in, ffn2_w_out,
              w_in, mlstm_b_i, mlstm_b_f, mlstm_gn, na_rpb, lru_conv_w, lru_conv_b, lru_w_a, lru_b_a, lru_w_x, lru_b_x,
              lru_lambda, w_branch, w_out, norm_final):
    l = x.shape[1]
    pos = jnp.arange(l)
    prow, pcol = pos // GRID_W, pos % GRID_W
    c_act = jax.nn.silu(c)
    cc_act = jax.nn.silu(c_ctx)
    xc = ctx
    for li in range(DEPTH):
        last = li == DEPTH - 1
        mod = (c_act @ w_mod[li] + b_mod[li])[:, None, :]
        modc = (cc_act @ w_mod[li] + b_mod[li])[None, None, :]
        sh1, sc1, g1, sh2, sc2, g2, sh3, sc3, g3 = jnp.split(mod, N_MOD, axis=-1)
        csh1, csc1, cg1, csh2, csc2, cg2, csh3, csc3, cg3 = jnp.split(modc, N_MOD, axis=-1)
        x = x + 0.5 * g1 * swiglu(adaln(x, norm_ffn1[li], sh1, sc1), ffn1_w_in[li], ffn1_w_out[li])
        xc = xc + 0.5 * cg1 * swiglu(adaln(xc, norm_ffn1[li], csh1, csc1), ffn1_w_in[li], ffn1_w_out[li])
        y, yc = mixer(adaln(x, norm_mix[li], sh2, sc2), adaln(xc, norm_mix[li], csh2, csc2), w_in[li],
                      mlstm_b_i[li], mlstm_b_f[li], mlstm_gn[li], na_rpb[li], lru_conv_w[li], lru_conv_b[li],
                      lru_w_a[li], lru_b_a[li], lru_w_x[li], lru_b_x[li], lru_lambda[li], w_branch[li], w_out[li],
                      prow, pcol, not last)
        x = x + g2 * y
        x = x + 0.5 * g3 * swiglu(adaln(x, norm_ffn2[li], sh3, sc3), ffn2_w_in[li], ffn2_w_out[li])
        if not last:
            xc = xc + cg2 * yc
            xc = xc + 0.5 * cg3 * swiglu(adaln(xc, norm_ffn2[li], csh3, csc3), ffn2_w_in[li], ffn2_w_out[li])
    return rmsnorm(x, norm_final)
```

```python
import functools

import jax
import jax.numpy as jnp
from jax import lax
from jax.experimental import pallas as pl
from jax.experimental.pallas import tpu as pltpu

F32 = jnp.float32
BF16 = jnp.bfloat16

D_MODEL = 2048
BATCH = 4
SEQ = 2048
DEPTH = 2
GRID_W = 64
CTX_LEN = 256
EPS = 1e-6
D_FF = (11 * D_MODEL) // 4
N_MOD = 9
BRANCH_W = D_MODEL // 2
N_BRANCH = 3
M_HEADS = 4
M_HEAD_DIM = BRANCH_W // M_HEADS
ROPE_BASE = 10000.0
NA_HEADS = 8
NA_HEAD_DIM = BRANCH_W // NA_HEADS
NA_KH = 8
NA_KW = 16
LRU_BLOCKS = 8
LRU_BLOCK_DIM = BRANCH_W // LRU_BLOCKS
LRU_CONV = 4
LRU_C = 8.0

SUBLANES = 8
LANES = 128

MOD_ROWS = 8
CTX_MOD_ROW = BATCH
TM = 512
TF = 512
TN_IN = 512
TN_MERGE = 512
TN_OUT = 512
TN_MOD = 1024
M_CHUNK = 256
N_CHUNKS = (SEQ + CTX_LEN) // M_CHUNK
LRU_CHUNK = 256
LRU_STEPS = (SEQ + CTX_LEN) // LRU_CHUNK
GATE_W = LANES
GATE_R = SUBLANES

COL_MQ, COL_MK, COL_MV, COL_MO = 0, BRANCH_W, 2 * BRANCH_W, 3 * BRANCH_W
COL_NQ, COL_NK, COL_NV = 4 * BRANCH_W, 5 * BRANCH_W, 6 * BRANCH_W
COL_LX, COL_LG = 7 * BRANCH_W, 8 * BRANCH_W
COL_GX = 9 * BRANCH_W
P_COLS = COL_GX + N_BRANCH * D_MODEL

VMEM_LIMIT = 56 * 1024 * 1024


def _cparams(sem):
    return pltpu.CompilerParams(dimension_semantics=sem, vmem_limit_bytes=VMEM_LIMIT)


def _sigmoid(x):
    return jax.nn.sigmoid(x)


def _log_sigmoid(x):
    return jnp.minimum(x, 0.0) - jnp.log1p(jnp.exp(-jnp.abs(x)))


def _softplus(x):
    return jnp.maximum(x, 0.0) + jnp.log1p(jnp.exp(-jnp.abs(x)))


def _dot(a, b):
    return jnp.dot(a, b, preferred_element_type=F32)


def _dot_nt(a, b):
    return lax.dot_general(a, b, (((1,), (1,)), ((), ())), preferred_element_type=F32)


def _dot_tn(a, b):
    return lax.dot_general(a, b, (((0,), (0,)), ((), ())), preferred_element_type=F32)


def _dot_exact(a, b):
    return jnp.dot(a, b, preferred_element_type=F32, precision=lax.Precision.HIGHEST)


def _adaln(x, gain, shift, scale):
    y = x * lax.rsqrt(jnp.mean(x * x, axis=-1, keepdims=True) + EPS)
    return (y * gain) * (1.0 + scale) + shift


def _mod_kernel(c_ref, w_ref, b_ref, o_ref):
    c = c_ref[...]
    act = (c * _sigmoid(c)).astype(BF16)
    o_ref[...] = _dot(act, w_ref[...].astype(BF16)) + b_ref[...]


def _modulation(c_rows, w_mod, b_mod):
    n_cols = N_MOD * D_MODEL
    return pl.pallas_call(
        _mod_kernel,
        grid=(DEPTH, n_cols // TN_MOD),
        in_specs=[
            pl.BlockSpec((MOD_ROWS, D_MODEL), lambda l, j: (0, 0)),
            pl.BlockSpec((None, D_MODEL, TN_MOD), lambda l, j: (l, 0, j)),
            pl.BlockSpec((None, 1, TN_MOD), lambda l, j: (l, 0, j)),
        ],
        out_specs=pl.BlockSpec((None, MOD_ROWS, TN_MOD), lambda l, j: (l, 0, j)),
        out_shape=jax.ShapeDtypeStruct((DEPTH, MOD_ROWS, n_cols), F32),
        compiler_params=_cparams(("arbitrary", "arbitrary")),
        name="modulation",
    )(c_rows, w_mod, b_mod.reshape(DEPTH, 1, n_cols))


def _mod_spec(k, row_of_tile):
    return pl.BlockSpec((None, None, 1, D_MODEL), lambda i, j: (row_of_tile(i), k, 0, 0))


def _lat_row(i):
    return i // (SEQ // TM)


def _ctx_row(i):
    return CTX_MOD_ROW


def _ffn_kernel(*refs, final):
    if final:
        x_ref, sh_ref, sc_ref, g_ref, ng_ref, wg_ref, wu_ref, wo_ref, fg_ref, o_ref, h_scr = refs
    else:
        x_ref, sh_ref, sc_ref, g_ref, ng_ref, wg_ref, wu_ref, wo_ref, o_ref, h_scr = refs
    f = pl.program_id(1)

    @pl.when(f == 0)
    def _():
        h = _adaln(x_ref[...], ng_ref[...], sh_ref[...], sc_ref[...])
        h_scr[...] = h.astype(BF16)
        o_ref[...] = jnp.zeros_like(o_ref)

    h = h_scr[...]
    gte = _dot(h, wg_ref[...])
    up = _dot(h, wu_ref[...])
    act = (gte * _sigmoid(gte) * up).astype(BF16)
    o_ref[...] += _dot(act, wo_ref[...])

    @pl.when(f == pl.num_programs(1) - 1)
    def _():
        out = x_ref[...] + 0.5 * g_ref[...] * o_ref[...]
        if final:
            out = out * lax.rsqrt(jnp.mean(out * out, axis=-1, keepdims=True) + EPS) * fg_ref[...]
        o_ref[...] = out


def _ffn(x, mod_l, k0, norm_g, w_in, w_out, row_of_tile, final_gain=None):
    rows = x.shape[0]
    nf = D_FF // TF
    final = final_gain is not None
    in_specs = [
        pl.BlockSpec((TM, D_MODEL), lambda i, f: (i, 0)),
        _mod_spec(k0, row_of_tile), _mod_spec(k0 + 1, row_of_tile), _mod_spec(k0 + 2, row_of_tile),
        pl.BlockSpec((1, D_MODEL), lambda i, f: (0, 0)),
        pl.BlockSpec((D_MODEL, TF), lambda i, f: (0, f)),
        pl.BlockSpec((D_MODEL, TF), lambda i, f: (0, f + nf)),
        pl.BlockSpec((TF, D_MODEL), lambda i, f: (f, 0)),
    ]
    args = [x, mod_l, mod_l, mod_l, norm_g.reshape(1, D_MODEL), w_in, w_in, w_out]
    if final:
        in_specs.append(pl.BlockSpec((1, D_MODEL), lambda i, f: (0, 0)))
        args.append(final_gain.reshape(1, D_MODEL))
    return pl.pallas_call(
        functools.partial(_ffn_kernel, final=final),
        grid=(rows // TM, nf),
        in_specs=in_specs,
        out_specs=pl.BlockSpec((TM, D_MODEL), lambda i, f: (i, 0)),
        out_shape=jax.ShapeDtypeStruct((rows, D_MODEL), F32),
        scratch_shapes=[pltpu.VMEM((TM, D_MODEL), BF16)],
        compiler_params=_cparams(("arbitrary", "arbitrary")),
        name="ffn_final" if final else "ffn",
    )(*args)


def _inproj_kernel(x_ref, sh_ref, sc_ref, ng_ref, w_ref, wg_ref, bg_ref, wgt_ref, bgt_ref,
                   p_ref, g_ref, gt_ref, h_scr):
    j = pl.program_id(1)

    @pl.when(j == 0)
    def _():
        h = _adaln(x_ref[...], ng_ref[...], sh_ref[...], sc_ref[...]).astype(BF16)
        h_scr[...] = h
        g_ref[...] = _dot(h, wg_ref[...]) + bg_ref[...]
        for s in range(TM // M_CHUNK):
            gt_ref[s] = _dot_nt(wgt_ref[...], h[s * M_CHUNK:(s + 1) * M_CHUNK, :]) + bgt_ref[...]

    p_ref[...] = _dot(h_scr[...], w_ref[...])


def _inproj(x, mod_l, norm_g, w, wg, bg, wgt, bgt, row_of_tile):
    rows = x.shape[0]
    return pl.pallas_call(
        _inproj_kernel,
        grid=(rows // TM, P_COLS // TN_IN),
        in_specs=[
            pl.BlockSpec((TM, D_MODEL), lambda i, j: (i, 0)),
            _mod_spec(3, row_of_tile), _mod_spec(4, row_of_tile),
            pl.BlockSpec((1, D_MODEL), lambda i, j: (0, 0)),
            pl.BlockSpec((D_MODEL, TN_IN), lambda i, j: (0, j)),
            pl.BlockSpec((D_MODEL, M_HEADS * GATE_W), lambda i, j: (0, 0)),
            pl.BlockSpec((1, M_HEADS * GATE_W), lambda i, j: (0, 0)),
            pl.BlockSpec((M_HEADS * GATE_R, D_MODEL), lambda i, j: (0, 0)),
            pl.BlockSpec((M_HEADS * GATE_R, 1), lambda i, j: (0, 0)),
        ],
        out_specs=[
            pl.BlockSpec((TM, TN_IN), lambda i, j: (i, j)),
            pl.BlockSpec((TM, M_HEADS * GATE_W), lambda i, j: (i, 0)),
            pl.BlockSpec((TM // M_CHUNK, M_HEADS * GATE_R, M_CHUNK), lambda i, j: (i, 0, 0)),
        ],
        out_shape=[
            jax.ShapeDtypeStruct((rows, P_COLS), F32),
            jax.ShapeDtypeStruct((rows, M_HEADS * GATE_W), F32),
            jax.ShapeDtypeStruct((rows // M_CHUNK, M_HEADS * GATE_R, M_CHUNK), F32),
        ],
        scratch_shapes=[pltpu.VMEM((TM, D_MODEL), BF16)],
        compiler_params=_cparams(("arbitrary", "arbitrary")),
        name="inproj",
    )(x, mod_l, mod_l, norm_g.reshape(1, D_MODEL), w, wg, bg, wgt, bgt)


def _rope(u, cs, sn):
    halves = []
    for j in range(M_HEAD_DIM // LANES):
        sl = slice(j * LANES, (j + 1) * LANES)
        uj = u[:, sl]
        halves.append(uj * cs[:, sl] + pltpu.roll(uj, LANES // 2, 1) * sn[:, sl])
    return jnp.concatenate(halves, axis=1)


def _mlstm_chunk(q, k, v, i_col, cum_col, i_row, cum_row, mask, last, c_mem, n_mem, m_prev):
    logw = jnp.where(mask, cum_col - cum_row + i_row, -jnp.inf)
    m_row = jnp.maximum(cum_col + m_prev, jnp.max(logw, axis=-1, keepdims=True))
    m_new = m_row[last:last + 1, :]
    cum_last = cum_col[last:last + 1, :]
    w_state = jnp.exp(cum_last - cum_col + i_col - m_new)
    decay = jnp.exp(cum_last + m_prev - m_new)
    inter = jnp.exp(cum_col + m_prev - m_row)
    kf = k.astype(F32)
    sc = _dot_nt(q, k) * jnp.exp(logw - m_row)
    num = inter * _dot_nt(q, c_mem.astype(BF16)) + _dot(sc.astype(BF16), v)
    den = inter * jnp.sum(q.astype(F32) * n_mem, axis=-1, keepdims=True) + jnp.sum(sc, axis=-1, keepdims=True)
    hid = num / jnp.maximum(jnp.abs(den), jnp.exp(-m_row))
    c_new = decay * c_mem + _dot_tn((w_state * v.astype(F32)).astype(BF16), k)
    n_new = decay * n_mem + jnp.sum(w_state * kf, axis=0, keepdims=True)
    return hid, c_new, n_new, m_new


def _mlstm_kernel(ql, kl, vl, ol, qc, kc, vc, oc, gl, gtl, gc, gtc, cos, sin, gn, yl, yc,
                  q_scr, k_scr, hf_scr, hb_scr):
    ch = M_CHUNK
    kscale = M_HEAD_DIM ** -0.5

    def rows(c):
        return slice(c * ch, (c + 1) * ch)

    q_scr[rows(0), :] = qc[...].astype(BF16)
    k_scr[rows(0), :] = (kc[...] * kscale).astype(BF16)
    for c in range(1, N_CHUNKS):
        lat = rows(c - 1)
        q_scr[rows(c), :] = _rope(ql[lat, :], cos[lat, :], sin[lat, :]).astype(BF16)
        k_scr[rows(c), :] = (_rope(kl[lat, :], cos[lat, :], sin[lat, :]) * kscale).astype(BF16)

    t_idx = lax.broadcasted_iota(jnp.int32, (ch, ch), 0)
    s_idx = lax.broadcasted_iota(jnp.int32, (ch, ch), 1)
    lower = s_idx <= t_idx
    upper = s_idx >= t_idx
    lower_f = lower.astype(F32)
    upper_f = upper.astype(F32)

    def gates(c):
        g = gc[...] if c == 0 else gl[rows(c - 1), :]
        gt = gtc[0] if c == 0 else gtl[c - 1]
        return g, gt

    def values(c):
        return vc[...].astype(BF16) if c == 0 else vl[rows(c - 1), :].astype(BF16)

    def init():
        return (jnp.zeros((M_HEAD_DIM, M_HEAD_DIM), F32), jnp.zeros((1, M_HEAD_DIM), F32),
                jnp.full((1, 1), -jnp.inf, F32))

    st_f, st_b = init(), init()
    for step in range(N_CHUNKS):
        cf = step
        cb = 0 if step == 0 else N_CHUNKS - step
        g, gt = gates(cf)
        cum_col = _dot_exact(lower_f, _log_sigmoid(g))[:, 1:2]
        cum_row = _dot_exact(_log_sigmoid(gt), upper_f)[1:2, :]
        hid, *st_f = _mlstm_chunk(q_scr[rows(cf), :], k_scr[rows(cf), :], values(cf), g[:, 0:1], cum_col,
                                  gt[0:1, :], cum_row, lower, ch - 1, *st_f)
        hf_scr[rows(cf), :] = hid
        g, gt = gates(cb)
        cum_col = _dot_exact(upper_f, _log_sigmoid(g))[:, 3:4]
        cum_row = _dot_exact(_log_sigmoid(gt), lower_f)[3:4, :]
        hid, *st_b = _mlstm_chunk(q_scr[rows(cb), :], k_scr[rows(cb), :], values(cb), g[:, 2:3], cum_col,
                                  gt[2:3, :], cum_row, upper, 0, *st_b)
        hb_scr[rows(cb), :] = hid

    gain = gn[...]
    for c in range(N_CHUNKS):
        hsum = hf_scr[rows(c), :] + hb_scr[rows(c), :]
        hn = hsum * lax.rsqrt(jnp.mean(hsum * hsum, axis=-1, keepdims=True) + EPS) * gain
        if c == 0:
            yc[...] = _sigmoid(oc[...]) * hn
        else:
            yl[rows(c - 1), :] = _sigmoid(ol[rows(c - 1), :]) * hn


def _mlstm(p_lat, p_ctx, g_lat, gt_lat, g_ctx, gt_ctx, cos, sin, gn):
    hd = M_HEAD_DIM
    cb = lambda col: col // hd

    def lat(col):
        return pl.BlockSpec((SEQ, hd), lambda b, h: (b, cb(col) + h))

    def ctx(col):
        return pl.BlockSpec((CTX_LEN, hd), lambda b, h: (b, cb(col) + h))

    seq_chunks = SEQ // M_CHUNK
    ctx_chunks = CTX_LEN // M_CHUNK
    return pl.pallas_call(
        _mlstm_kernel,
        grid=(BATCH, M_HEADS),
        in_specs=[
            lat(COL_MQ), lat(COL_MK), lat(COL_MV), lat(COL_MO),
            ctx(COL_MQ), ctx(COL_MK), ctx(COL_MV), ctx(COL_MO),
            pl.BlockSpec((SEQ, GATE_W), lambda b, h: (b, h)),
            pl.BlockSpec((seq_chunks, GATE_R, M_CHUNK), lambda b, h: (b, h, 0)),
            pl.BlockSpec((CTX_LEN, GATE_W), lambda b, h: (b, h)),
            pl.BlockSpec((ctx_chunks, GATE_R, M_CHUNK), lambda b, h: (b, h, 0)),
            pl.BlockSpec((SEQ, hd), lambda b, h: (0, 0)),
            pl.BlockSpec((SEQ, hd), lambda b, h: (0, 0)),
            pl.BlockSpec((1, hd), lambda b, h: (0, h)),
        ],
        out_specs=[
            pl.BlockSpec((SEQ, hd), lambda b, h: (b, h)),
            pl.BlockSpec((CTX_LEN, hd), lambda b, h: (b, h)),
        ],
        out_shape=[
            jax.ShapeDtypeStruct((BATCH * SEQ, BRANCH_W), F32),
            jax.ShapeDtypeStruct((BATCH * CTX_LEN, BRANCH_W), F32),
        ],
        scratch_shapes=[
            pltpu.VMEM((SEQ + CTX_LEN, hd), BF16), pltpu.VMEM((SEQ + CTX_LEN, hd), BF16),
            pltpu.VMEM((SEQ + CTX_LEN, hd), F32), pltpu.VMEM((SEQ + CTX_LEN, hd), F32),
        ],
        compiler_params=_cparams(("arbitrary", "arbitrary")),
        name="mlstm",
    )(p_lat, p_lat, p_lat, p_lat, p_ctx, p_ctx, p_ctx, p_ctx, g_lat, gt_lat, g_ctx, gt_ctx, cos, sin,
      gn.reshape(1, BRANCH_W))


def _rope_tables():
    pos = jnp.arange(SEQ)
    nf = M_HEAD_DIM // 4
    inv = ROPE_BASE ** (-jnp.arange(nf, dtype=F32) / nf)
    ang_r = (pos // GRID_W).astype(F32)[:, None] * inv[None, :]
    ang_c = (pos % GRID_W).astype(F32)[:, None] * inv[None, :]
    cr, sr, cc, sc = jnp.cos(ang_r), jnp.sin(ang_r), jnp.cos(ang_c), jnp.sin(ang_c)
    return (jnp.concatenate([cr, cr, cc, cc], axis=-1), jnp.concatenate([-sr, sr, -sc, sc], axis=-1))


NA_GROUP = 2
NA_WIN = NA_KH * GRID_W
NA_ROWS = SEQ // GRID_W


def _na_kernel(ql, kl, vl, qc, kc, vc, bias, yl, yc):
    scale = NA_HEAD_DIM ** -0.5
    kcs = [kc[:, hh * NA_HEAD_DIM:(hh + 1) * NA_HEAD_DIM].astype(BF16) for hh in range(NA_GROUP)]
    vcs = [vc[:, hh * NA_HEAD_DIM:(hh + 1) * NA_HEAD_DIM].astype(BF16) for hh in range(NA_GROUP)]

    def row_body(r, carry):
        start = jnp.clip(r - NA_KH // 2, 0, NA_ROWS - NA_KH)
        off = start - r + (NA_KH - 1)
        q_rows = pl.ds(pl.multiple_of(r * GRID_W, GRID_W), GRID_W)
        k_rows = pl.ds(pl.multiple_of(start * GRID_W, GRID_W), NA_WIN)
        outs = []
        for hh in range(NA_GROUP):
            sl = slice(hh * NA_HEAD_DIM, (hh + 1) * NA_HEAD_DIM)
            q = ql[q_rows, sl].astype(BF16)
            kw = kl[k_rows, sl].astype(BF16)
            vw = vl[k_rows, sl].astype(BF16)
            s_win = _dot_nt(q, kw) * scale + bias[hh, off]
            s_ctx = _dot_nt(q, kcs[hh]) * scale
            m = jnp.maximum(jnp.max(s_win, axis=-1, keepdims=True), jnp.max(s_ctx, axis=-1, keepdims=True))
            p_win = jnp.exp(s_win - m)
            p_ctx = jnp.exp(s_ctx - m)
            denom = jnp.sum(p_win, axis=-1, keepdims=True) + jnp.sum(p_ctx, axis=-1, keepdims=True)
            o = _dot(p_win.astype(BF16), vw) + _dot(p_ctx.astype(BF16), vcs[hh])
            outs.append(o / denom)
        yl[q_rows, :] = jnp.concatenate(outs, axis=1)
        return carry

    lax.fori_loop(0, NA_ROWS, row_body, 0)

    outs = []
    for hh in range(NA_GROUP):
        sl = slice(hh * NA_HEAD_DIM, (hh + 1) * NA_HEAD_DIM)
        s = _dot_nt(qc[:, sl].astype(BF16), kcs[hh]) * scale
        p = jnp.exp(s - jnp.max(s, axis=-1, keepdims=True))
        outs.append(_dot(p.astype(BF16), vcs[hh]) / jnp.sum(p, axis=-1, keepdims=True))
    yc[...] = jnp.concatenate(outs, axis=1)


def _na_bias(rpb):
    col = jnp.arange(GRID_W)
    col_start = jnp.clip(col - NA_KW // 2, 0, GRID_W - NA_KW)
    col_ok = (col[None, :] >= col_start[:, None]) & (col[None, :] < col_start[:, None] + NA_KW)
    dc = jnp.clip(col[None, :] - col[:, None], -(NA_KW - 1), NA_KW - 1) + (NA_KW - 1)
    dr = jnp.arange(NA_KH)[:, None] + jnp.arange(NA_KH)[None, :]
    b = rpb[:, dr[:, :, None, None], dc[None, None, :, :]].astype(F32)
    b = jnp.where(col_ok[None, None, None], b, -jnp.inf)
    return jnp.transpose(b, (0, 1, 3, 2, 4)).reshape(NA_HEADS, NA_KH, GRID_W, NA_WIN)


def _na(p_lat, p_ctx, bias):
    gw = NA_GROUP * NA_HEAD_DIM
    cb = lambda col: col // gw

    def lat(col):
        return pl.BlockSpec((SEQ, gw), lambda b, g: (b, cb(col) + g))

    def ctx(col):
        return pl.BlockSpec((CTX_LEN, gw), lambda b, g: (b, cb(col) + g))

    return pl.pallas_call(
        _na_kernel,
        grid=(BATCH, NA_HEADS // NA_GROUP),
        in_specs=[
            lat(COL_NQ), lat(COL_NK), lat(COL_NV), ctx(COL_NQ), ctx(COL_NK), ctx(COL_NV),
            pl.BlockSpec((NA_GROUP, NA_KH, GRID_W, NA_WIN), lambda b, g: (g, 0, 0, 0)),
        ],
        out_specs=[
            pl.BlockSpec((SEQ, gw), lambda b, g: (b, g)),
            pl.BlockSpec((CTX_LEN, gw), lambda b, g: (b, g)),
        ],
        out_shape=[
            jax.ShapeDtypeStruct((BATCH * SEQ, BRANCH_W), F32),
            jax.ShapeDtypeStruct((BATCH * CTX_LEN, BRANCH_W), F32),
        ],
        compiler_params=_cparams(("arbitrary", "arbitrary")),
        name="natten",
    )(p_lat, p_lat, p_lat, p_ctx, p_ctx, p_ctx, bias)


LRU_HALO = SUBLANES


def _lru_kernel(xl, xc, cw, cb, wa, ba, wx, bx, lam, hfl, hbl, hfc, hbc,
                ext_f, ext_b, a_f, u_f, a_b, u_b, carry):
    s = pl.program_id(1)
    ch = LRU_CHUNK
    n_lat = SEQ // ch
    halo = LRU_HALO
    zeros_halo = jnp.zeros((halo, BRANCH_W), F32)

    def fill(ext, chunk):
        @pl.when(s == 0)
        def _():
            ext[0:halo, :] = zeros_halo
            ext[halo:halo + ch, :] = xc[...]
            ext[halo + ch:, :] = zeros_halo

        @pl.when(s > 0)
        def _():
            base = pl.multiple_of(chunk * ch, ch)
            ext[halo:halo + ch, :] = xl[pl.ds(base, ch), :]
            prev = xl[pl.ds(pl.multiple_of(jnp.maximum(base - halo, 0), halo), halo), :]
            nxt = xl[pl.ds(pl.multiple_of(jnp.minimum(base + ch, SEQ - halo), halo), halo), :]
            ext[0:halo, :] = jnp.where(chunk > 0, prev, 0.0)
            ext[halo + ch:, :] = jnp.where(chunk < n_lat - 1, nxt, 0.0)

    chunk_f = jnp.maximum(s - 1, 0)
    chunk_b = jnp.maximum(n_lat - s, 0)
    fill(ext_f, chunk_f)
    fill(ext_b, chunk_b)

    def coeffs(ext, d, a_out, u_out):
        xconv = cb[...] + ext[halo - 1:halo - 1 + ch, :] * cw[0:1, :]
        for j in range(1, LRU_CONV):
            xconv = xconv + ext[halo - 1 + j:halo - 1 + j + ch, :] * cw[j:j + 1, :]
        sp = _softplus(-lam[d:d + 1, :])
        for g in range(LRU_BLOCKS):
            sl = slice(g * LRU_BLOCK_DIM, (g + 1) * LRU_BLOCK_DIM)
            xg = xconv[:, sl]
            xb = xg.astype(BF16)
            r = _sigmoid(_dot(xb, wa[d, g]) + ba[d:d + 1, sl])
            i = _sigmoid(_dot(xb, wx[d, g]) + bx[d:d + 1, sl])
            log_a = -LRU_C * r * sp[:, sl]
            a = jnp.exp(log_a)
            a_out[:, sl] = a
            u_out[:, sl] = jnp.sqrt(jnp.tanh(-log_a) * (a * a + 1.0)) * i * xg

    coeffs(ext_f, 0, a_f, u_f)
    coeffs(ext_b, 1, a_b, u_b)

    @pl.when(s == 0)
    def _():
        carry[...] = jnp.zeros_like(carry)

    row = lax.broadcasted_iota(jnp.int32, (SUBLANES, BRANCH_W), 0)
    n_tiles = ch // SUBLANES

    def scan_tiles(hf_out, hb_out):
        def body(t, hs):
            h_f, h_b = hs
            r0 = pl.multiple_of(t * SUBLANES, SUBLANES)
            a = a_f[pl.ds(r0, SUBLANES), :]
            u = u_f[pl.ds(r0, SUBLANES), :]
            for d in (1, 2, 4):
                ok = row >= d
                u = jnp.where(ok, a * pltpu.roll(u, d, 0) + u, u)
                a = jnp.where(ok, a * pltpu.roll(a, d, 0), a)
            h = u + a * h_f
            hf_out[pl.ds(r0, SUBLANES), :] = h
            h_f = h[SUBLANES - 1:SUBLANES, :]
            r1 = pl.multiple_of((n_tiles - 1 - t) * SUBLANES, SUBLANES)
            a = a_b[pl.ds(r1, SUBLANES), :]
            u = u_b[pl.ds(r1, SUBLANES), :]
            for d in (1, 2, 4):
                ok = row < SUBLANES - d
                u = jnp.where(ok, a * pltpu.roll(u, SUBLANES - d, 0) + u, u)
                a = jnp.where(ok, a * pltpu.roll(a, SUBLANES - d, 0), a)
            h = u + a * h_b
            hb_out[pl.ds(r1, SUBLANES), :] = h
            h_b = h[0:1, :]
            return h_f, h_b

        h_f, h_b = lax.fori_loop(0, n_tiles, body, (carry[0:1, :], carry[1:2, :]))
        carry[0:1, :] = h_f
        carry[1:2, :] = h_b

    @pl.when(s == 0)
    def _():
        scan_tiles(hfc, hbc)

    @pl.when(s > 0)
    def _():
        scan_tiles(hfl, hbl)


def _lru(p_lat, p_ctx, conv_w, conv_b, w_a, b_a, w_x, b_x, lam):
    ch = LRU_CHUNK
    n_lat = SEQ // ch
    full = lambda shape: pl.BlockSpec(shape, lambda b, s: (0,) * len(shape))
    wblk = (2, LRU_BLOCKS, LRU_BLOCK_DIM, LRU_BLOCK_DIM)
    return pl.pallas_call(
        _lru_kernel,
        grid=(BATCH, LRU_STEPS),
        in_specs=[
            pl.BlockSpec((SEQ, BRANCH_W), lambda b, s: (b, COL_LX // BRANCH_W)),
            pl.BlockSpec((CTX_LEN, BRANCH_W), lambda b, s: (b, COL_LX // BRANCH_W)),
            full((LRU_CONV, BRANCH_W)), full((1, BRANCH_W)),
            full(wblk), full((2, BRANCH_W)), full(wblk), full((2, BRANCH_W)), full((2, BRANCH_W)),
        ],
        out_specs=[
            pl.BlockSpec((ch, BRANCH_W), lambda b, s: (b * n_lat + jnp.maximum(s - 1, 0), 0)),
            pl.BlockSpec((ch, BRANCH_W), lambda b, s: (b * n_lat + jnp.minimum(n_lat - s, n_lat - 1), 0)),
            pl.BlockSpec((CTX_LEN, BRANCH_W), lambda b, s: (b, 0)),
            pl.BlockSpec((CTX_LEN, BRANCH_W), lambda b, s: (b, 0)),
        ],
        out_shape=[
            jax.ShapeDtypeStruct((BATCH * SEQ, BRANCH_W), F32),
            jax.ShapeDtypeStruct((BATCH * SEQ, BRANCH_W), F32),
            jax.ShapeDtypeStruct((BATCH * CTX_LEN, BRANCH_W), F32),
            jax.ShapeDtypeStruct((BATCH * CTX_LEN, BRANCH_W), F32),
        ],
        scratch_shapes=[
            pltpu.VMEM((ch + 2 * LRU_HALO, BRANCH_W), F32), pltpu.VMEM((ch + 2 * LRU_HALO, BRANCH_W), F32),
            pltpu.VMEM((ch, BRANCH_W), F32), pltpu.VMEM((ch, BRANCH_W), F32),
            pltpu.VMEM((ch, BRANCH_W), F32), pltpu.VMEM((ch, BRANCH_W), F32),
            pltpu.VMEM((SUBLANES, BRANCH_W), F32),
        ],
        compiler_params=_cparams(("arbitrary", "arbitrary")),
        name="rglru",
    )(p_lat, p_ctx, conv_w, conv_b.reshape(1, BRANCH_W), w_a, b_a, w_x, b_x, lam)


def _merge_kernel(ym, yn, hf, hb, lg, g0, g1, g2, wb, o_ref, ym_s, yn_s, yl_s):
    j = pl.program_id(1)

    @pl.when(j == 0)
    def _():
        ym_s[...] = ym[...].astype(BF16)
        yn_s[...] = yn[...].astype(BF16)
        yl_s[...] = ((hf[...] + hb[...]) * jax.nn.gelu(lg[...])).astype(BF16)

    acc = _sigmoid(g0[...]) * _dot(ym_s[...], wb[0])
    acc = acc + _sigmoid(g1[...]) * _dot(yn_s[...], wb[1])
    acc = acc + _sigmoid(g2[...]) * _dot(yl_s[...], wb[2])
    o_ref[...] = acc.astype(BF16)


def _merge(ym, yn, hf, hb, p, w_branch):
    rows = ym.shape[0]
    tn = TN_MERGE
    br = lambda: pl.BlockSpec((TM, BRANCH_W), lambda i, j: (i, 0))

    def gate(n):
        return pl.BlockSpec((TM, tn), lambda i, j: (i, (COL_GX + n * D_MODEL) // tn + j))

    return pl.pallas_call(
        _merge_kernel,
        grid=(rows // TM, D_MODEL // tn),
        in_specs=[
            br(), br(), br(), br(),
            pl.BlockSpec((TM, BRANCH_W), lambda i, j: (i, COL_LG // BRANCH_W)),
            gate(0), gate(1), gate(2),
            pl.BlockSpec((N_BRANCH, BRANCH_W, tn), lambda i, j: (0, 0, j)),
        ],
        out_specs=pl.BlockSpec((TM, tn), lambda i, j: (i, j)),
        out_shape=jax.ShapeDtypeStruct((rows, D_MODEL), BF16),
        scratch_shapes=[pltpu.VMEM((TM, BRANCH_W), BF16)] * 3,
        compiler_params=_cparams(("arbitrary", "arbitrary")),
        name="merge",
    )(ym, yn, hf, hb, p, p, p, p, w_branch)


def _outproj_kernel(s_ref, w_ref, x_ref, g_ref, o_ref):
    o_ref[...] = x_ref[...] + g_ref[...] * _dot(s_ref[...], w_ref[...])


def _outproj(s, w_out, x, mod_l, row_of_tile):
    rows = x.shape[0]
    tn = TN_OUT
    return pl.pallas_call(
        _outproj_kernel,
        grid=(rows // TM, D_MODEL // tn),
        in_specs=[
            pl.BlockSpec((TM, D_MODEL), lambda i, j: (i, 0)),
            pl.BlockSpec((D_MODEL, tn), lambda i, j: (0, j)),
            pl.BlockSpec((TM, tn), lambda i, j: (i, j)),
            pl.BlockSpec((None, None, 1, tn), lambda i, j: (row_of_tile(i), 5, 0, j)),
        ],
        out_specs=pl.BlockSpec((TM, tn), lambda i, j: (i, j)),
        out_shape=jax.ShapeDtypeStruct((rows, D_MODEL), F32),
        compiler_params=_cparams(("arbitrary", "arbitrary")),
        name="outproj",
    )(s, w_out, x, mod_l)


def _split_w_in(w):
    bw = BRANCH_W
    n_gate = 4 * M_HEADS
    o_mg = 4 * bw
    o_n = o_mg + n_gate
    main = jnp.concatenate([w[:, :o_mg], w[:, o_n:]], axis=1).astype(BF16)
    mg = w[:, o_mg:o_n].reshape(D_MODEL, 2, 2, M_HEADS)
    mg = jnp.transpose(mg, (0, 3, 1, 2)).reshape(D_MODEL, M_HEADS, 4)
    wg = jnp.pad(mg, ((0, 0), (0, 0), (0, GATE_W - 4))).reshape(D_MODEL, M_HEADS * GATE_W).astype(BF16)
    wgt = jnp.pad(jnp.transpose(mg, (1, 2, 0)), ((0, 0), (0, GATE_R - 4), (0, 0)))
    wgt = wgt.reshape(M_HEADS * GATE_R, D_MODEL).astype(BF16)
    return main, wg, wgt


def _gate_bias(b_i, b_f):
    b = jnp.stack([b_i, b_f], axis=1)
    b = jnp.transpose(b, (2, 0, 1)).reshape(M_HEADS, 4).astype(F32)
    bg = jnp.pad(b, ((0, 0), (0, GATE_W - 4))).reshape(1, M_HEADS * GATE_W)
    bgt = jnp.pad(b, ((0, 0), (0, GATE_R - 4))).reshape(M_HEADS * GATE_R, 1)
    return bg, bgt


def kernel(x, c, ctx, c_ctx, w_mod, b_mod, norm_ffn1, norm_mix, norm_ffn2, ffn1_w_in, ffn1_w_out, ffn2_w_in,
           ffn2_w_out, w_in, mlstm_b_i, mlstm_b_f, mlstm_gn, na_rpb, lru_conv_w, lru_conv_b, lru_w_a, lru_b_a,
           lru_w_x, lru_b_x, lru_lambda, w_branch, w_out, norm_final):
    assert x.shape == (BATCH, SEQ, D_MODEL) and ctx.shape == (BATCH, CTX_LEN, D_MODEL)
    xl = x.reshape(BATCH * SEQ, D_MODEL)
    xc = ctx.reshape(BATCH * CTX_LEN, D_MODEL)
    c_rows = jnp.concatenate([c, c_ctx[None, :], jnp.zeros((MOD_ROWS - BATCH - 1, D_MODEL), F32)], axis=0)
    mod = _modulation(c_rows, w_mod, b_mod).reshape(DEPTH, MOD_ROWS, N_MOD, 1, D_MODEL)
    cos, sin = _rope_tables()

    for li in range(DEPTH):
        last = li == DEPTH - 1
        mod_l = mod[li]
        w1_in, w1_out = ffn1_w_in[li].astype(BF16), ffn1_w_out[li].astype(BF16)
        w2_in, w2_out = ffn2_w_in[li].astype(BF16), ffn2_w_out[li].astype(BF16)
        w_main, wg, wgt = _split_w_in(w_in[li])
        bg, bgt = _gate_bias(mlstm_b_i[li], mlstm_b_f[li])
        wb = w_branch[li].astype(BF16)
        wo = w_out[li].astype(BF16)
        bias = _na_bias(na_rpb[li])
        wa, wx = lru_w_a[li].astype(BF16), lru_w_x[li].astype(BF16)

        xl = _ffn(xl, mod_l, 0, norm_ffn1[li], w1_in, w1_out, _lat_row)
        xc = _ffn(xc, mod_l, 0, norm_ffn1[li], w1_in, w1_out, _ctx_row)

        p_lat, g_lat, gt_lat = _inproj(xl, mod_l, norm_mix[li], w_main, wg, bg, wgt, bgt, _lat_row)
        p_ctx, g_ctx, gt_ctx = _inproj(xc, mod_l, norm_mix[li], w_main, wg, bg, wgt, bgt, _ctx_row)

        ym_l, ym_c = _mlstm(p_lat, p_ctx, g_lat, gt_lat, g_ctx, gt_ctx, cos, sin, mlstm_gn[li])
        yn_l, yn_c = _na(p_lat, p_ctx, bias)
        hf_l, hb_l, hf_c, hb_c = _lru(p_lat, p_ctx, lru_conv_w[li], lru_conv_b[li], wa, lru_b_a[li], wx,
                                      lru_b_x[li], lru_lambda[li])

        s_l = _merge(ym_l, yn_l, hf_l, hb_l, p_lat, wb)
        xl = _outproj(s_l, wo, xl, mod_l, _lat_row)
        xl = _ffn(xl, mod_l, 6, norm_ffn2[li], w2_in, w2_out, _lat_row, norm_final if last else None)
        if not last:
            s_c = _merge(ym_c, yn_c, hf_c, hb_c, p_ctx, wb)
            xc = _outproj(s_c, wo, xc, mod_l, _ctx_row)
            xc = _ffn(xc, mod_l, 6, norm_ffn2[li], w2_in, w2_out, _ctx_row)
    return xl.reshape(BATCH, SEQ, D_MODEL)
```

```python
import functools

import jax
import jax.numpy as jnp
from jax import lax
from jax.experimental import pallas as pl
from jax.experimental.pallas import tpu as pltpu

F32 = jnp.float32
BF16 = jnp.bfloat16

D_MODEL = 2048
BATCH = 4
SEQ = 2048
DEPTH = 2
GRID_W = 64
CTX_LEN = 256
EPS = 1e-6
D_FF = (11 * D_MODEL) // 4
N_MOD = 9
BRANCH_W = D_MODEL // 2
N_BRANCH = 3
M_HEADS = 4
M_HEAD_DIM = BRANCH_W // M_HEADS
ROPE_BASE = 10000.0
NA_HEADS = 8
NA_HEAD_DIM = BRANCH_W // NA_HEADS
NA_KH = 8
NA_KW = 16
LRU_BLOCKS = 8
LRU_BLOCK_DIM = BRANCH_W // LRU_BLOCKS
LRU_CONV = 4
LRU_C = 8.0

SUBLANES = 8
LANES = 128

MOD_ROWS = 8
CTX_MOD_ROW = BATCH
TM_FFN = 512
TM_IN = 1024
TM_MERGE = 512
TM_OUT = 512
TF = 512
TN_IN = 1024
TN_MERGE = 512
TN_OUT = 512
TN_MOD = 1024
M_CHUNK = 256
N_CHUNKS = (SEQ + CTX_LEN) // M_CHUNK
LRU_CHUNK = 256
LRU_STEPS = (SEQ + CTX_LEN) // LRU_CHUNK
GATE_W = LANES
GATE_R = SUBLANES

COL_MQ, COL_MK, COL_MV, COL_MO = 0, BRANCH_W, 2 * BRANCH_W, 3 * BRANCH_W
COL_NQ, COL_NK, COL_NV = 4 * BRANCH_W, 5 * BRANCH_W, 6 * BRANCH_W
COL_LX, COL_LG = 7 * BRANCH_W, 8 * BRANCH_W
COL_GX = 9 * BRANCH_W
P_COLS = COL_GX + N_BRANCH * D_MODEL

VMEM_BYTES = 64 * 1024 * 1024
VMEM_LIMIT = VMEM_BYTES - 8 * 1024 * 1024


def _cparams(sem):
    return pltpu.CompilerParams(dimension_semantics=sem, vmem_limit_bytes=VMEM_LIMIT)


def _col_tiles(w, tn):
    k, n = w.shape
    return jnp.transpose(w.reshape(k, n // tn, tn), (1, 0, 2))


def _sigmoid(x):
    return jax.nn.sigmoid(x)


def _log_sigmoid(x):
    return jnp.minimum(x, 0.0) - jnp.log1p(jnp.exp(-jnp.abs(x)))


def _softplus(x):
    return jnp.maximum(x, 0.0) + jnp.log1p(jnp.exp(-jnp.abs(x)))


def _dot(a, b):
    return jnp.dot(a, b, preferred_element_type=F32)


def _dot_nt(a, b):
    return lax.dot_general(a, b, (((1,), (1,)), ((), ())), preferred_element_type=F32)


def _dot_tn(a, b):
    return lax.dot_general(a, b, (((0,), (0,)), ((), ())), preferred_element_type=F32)


def _dot_exact(a, b):
    return jnp.dot(a, b, preferred_element_type=F32, precision=lax.Precision.HIGHEST)


def _adaln(x, gain, shift, scale):
    y = x * lax.rsqrt(jnp.mean(x * x, axis=-1, keepdims=True) + EPS)
    return (y * gain) * (1.0 + scale) + shift


def _mod_kernel(c_ref, w_ref, b_ref, o_ref):
    c = c_ref[...]
    act = (c * _sigmoid(c)).astype(BF16)
    o_ref[...] = _dot(act, w_ref[...].astype(BF16)) + b_ref[...]


def _modulation(c_rows, w_mod, b_mod):
    n_cols = N_MOD * D_MODEL
    return pl.pallas_call(
        _mod_kernel,
        grid=(DEPTH, n_cols // TN_MOD),
        in_specs=[
            pl.BlockSpec((MOD_ROWS, D_MODEL), lambda l, j: (0, 0)),
            pl.BlockSpec((None, D_MODEL, TN_MOD), lambda l, j: (l, 0, j)),
            pl.BlockSpec((None, 1, TN_MOD), lambda l, j: (l, 0, j)),
        ],
        out_specs=pl.BlockSpec((None, MOD_ROWS, TN_MOD), lambda l, j: (l, 0, j)),
        out_shape=jax.ShapeDtypeStruct((DEPTH, MOD_ROWS, n_cols), F32),
        compiler_params=_cparams(("arbitrary", "arbitrary")),
        name="modulation",
    )(c_rows, w_mod, b_mod.reshape(DEPTH, 1, n_cols))


def _mod_spec(k, row_of_tile):
    return pl.BlockSpec((None, None, 1, D_MODEL), lambda i, j: (row_of_tile(i), k, 0, 0))


def _lat_row(tm):
    return lambda i: i // (SEQ // tm)


def _ctx_row(tm):
    return lambda i: CTX_MOD_ROW


def _ffn_kernel(*refs, final):
    if final:
        x_ref, sh_ref, sc_ref, g_ref, ng_ref, wg_ref, wu_ref, wo_ref, fg_ref, o_ref, h_scr = refs
    else:
        x_ref, sh_ref, sc_ref, g_ref, ng_ref, wg_ref, wu_ref, wo_ref, o_ref, h_scr = refs
    f = pl.program_id(1)

    @pl.when(f == 0)
    def _():
        h = _adaln(x_ref[...], ng_ref[...], sh_ref[...], sc_ref[...])
        h_scr[...] = h.astype(BF16)
        o_ref[...] = jnp.zeros_like(o_ref)

    h = h_scr[...]
    gte = _dot(h, wg_ref[...])
    up = _dot(h, wu_ref[...])
    act = (gte * _sigmoid(gte) * up).astype(BF16)
    o_ref[...] += _dot(act, wo_ref[...])

    @pl.when(f == pl.num_programs(1) - 1)
    def _():
        out = x_ref[...] + 0.5 * g_ref[...] * o_ref[...]
        if final:
            out = out * lax.rsqrt(jnp.mean(out * out, axis=-1, keepdims=True) + EPS) * fg_ref[...]
        o_ref[...] = out


def _ffn(x, mod_l, k0, norm_g, w_in, w_out, row_fn, final_gain=None):
    rows = x.shape[0]
    nf = D_FF // TF
    tm = TM_FFN
    row_of_tile = row_fn(tm)
    final = final_gain is not None
    in_specs = [
        pl.BlockSpec((tm, D_MODEL), lambda i, f: (i, 0)),
        _mod_spec(k0, row_of_tile), _mod_spec(k0 + 1, row_of_tile), _mod_spec(k0 + 2, row_of_tile),
        pl.BlockSpec((1, D_MODEL), lambda i, f: (0, 0)),
        pl.BlockSpec((None, D_MODEL, TF), lambda i, f: (f, 0, 0)),
        pl.BlockSpec((None, D_MODEL, TF), lambda i, f: (f + nf, 0, 0)),
        pl.BlockSpec((TF, D_MODEL), lambda i, f: (f, 0)),
    ]
    args = [x, mod_l, mod_l, mod_l, norm_g.reshape(1, D_MODEL), w_in, w_in, w_out]
    if final:
        in_specs.append(pl.BlockSpec((1, D_MODEL), lambda i, f: (0, 0)))
        args.append(final_gain.reshape(1, D_MODEL))
    return pl.pallas_call(
        functools.partial(_ffn_kernel, final=final),
        grid=(rows // tm, nf),
        in_specs=in_specs,
        out_specs=pl.BlockSpec((tm, D_MODEL), lambda i, f: (i, 0)),
        out_shape=jax.ShapeDtypeStruct((rows, D_MODEL), F32),
        scratch_shapes=[pltpu.VMEM((tm, D_MODEL), BF16)],
        compiler_params=_cparams(("arbitrary", "arbitrary")),
        name="ffn_final" if final else "ffn",
    )(*args)


def _inproj_kernel(x_ref, sh_ref, sc_ref, ng_ref, w_ref, wg_ref, bg_ref, wgt_ref, bgt_ref,
                   p_ref, g_ref, gt_ref, h_scr):
    j = pl.program_id(1)

    @pl.when(j == 0)
    def _():
        h = _adaln(x_ref[...], ng_ref[...], sh_ref[...], sc_ref[...]).astype(BF16)
        h_scr[...] = h
        g_ref[...] = _dot(h, wg_ref[...]) + bg_ref[...]
        for s in range(TM_IN // M_CHUNK):
            gt_ref[s] = _dot_nt(wgt_ref[...], h[s * M_CHUNK:(s + 1) * M_CHUNK, :]) + bgt_ref[...]

    p_ref[...] = _dot(h_scr[...], w_ref[...]).astype(p_ref.dtype)


def _inproj(x, mod_l, norm_g, w, wg, bg, wgt, bgt, row_fn):
    rows = x.shape[0]
    tm = TM_IN
    row_of_tile = row_fn(tm)
    return pl.pallas_call(
        _inproj_kernel,
        grid=(rows // tm, P_COLS // TN_IN),
        in_specs=[
            pl.BlockSpec((tm, D_MODEL), lambda i, j: (i, 0)),
            _mod_spec(3, row_of_tile), _mod_spec(4, row_of_tile),
            pl.BlockSpec((1, D_MODEL), lambda i, j: (0, 0)),
            pl.BlockSpec((None, D_MODEL, TN_IN), lambda i, j: (j, 0, 0)),
            pl.BlockSpec((D_MODEL, M_HEADS * GATE_W), lambda i, j: (0, 0)),
            pl.BlockSpec((1, M_HEADS * GATE_W), lambda i, j: (0, 0)),
            pl.BlockSpec((M_HEADS * GATE_R, D_MODEL), lambda i, j: (0, 0)),
            pl.BlockSpec((M_HEADS * GATE_R, 1), lambda i, j: (0, 0)),
        ],
        out_specs=[
            pl.BlockSpec((tm, TN_IN), lambda i, j: (i, j)),
            pl.BlockSpec((tm, M_HEADS * GATE_W), lambda i, j: (i, 0)),
            pl.BlockSpec((tm // M_CHUNK, M_HEADS * GATE_R, M_CHUNK), lambda i, j: (i, 0, 0)),
        ],
        out_shape=[
            jax.ShapeDtypeStruct((rows, P_COLS), BF16),
            jax.ShapeDtypeStruct((rows, M_HEADS * GATE_W), F32),
            jax.ShapeDtypeStruct((rows // M_CHUNK, M_HEADS * GATE_R, M_CHUNK), F32),
        ],
        scratch_shapes=[pltpu.VMEM((tm, D_MODEL), BF16)],
        compiler_params=_cparams(("arbitrary", "arbitrary")),
        name="inproj",
    )(x, mod_l, mod_l, norm_g.reshape(1, D_MODEL), w, wg, bg, wgt, bgt)


def _rope(u, cs, sn):
    halves = []
    for j in range(M_HEAD_DIM // LANES):
        sl = slice(j * LANES, (j + 1) * LANES)
        uj = u[:, sl]
        halves.append(uj * cs[:, sl] + pltpu.roll(uj, LANES // 2, 1) * sn[:, sl])
    return jnp.concatenate(halves, axis=1)


def _mlstm_chunk(q, k, v, i_col, cum_col, i_row, cum_row, mask, last, c_mem, n_mem, m_prev):
    logw = jnp.where(mask, cum_col - cum_row + i_row, -jnp.inf)
    m_row = jnp.maximum(cum_col + m_prev, jnp.max(logw, axis=-1, keepdims=True))
    m_new = m_row[last:last + 1, :]
    cum_last = cum_col[last:last + 1, :]
    w_state = jnp.exp(cum_last - cum_col + i_col - m_new)
    decay = jnp.exp(cum_last + m_prev - m_new)
    inter = jnp.exp(cum_col + m_prev - m_row)
    kf = k.astype(F32)
    sc = _dot_nt(q, k) * jnp.exp(logw - m_row)
    num = inter * _dot_nt(q, c_mem.astype(BF16)) + _dot(sc.astype(BF16), v)
    den = inter * jnp.sum(q.astype(F32) * n_mem, axis=-1, keepdims=True) + jnp.sum(sc, axis=-1, keepdims=True)
    hid = num / jnp.maximum(jnp.abs(den), jnp.exp(-m_row))
    c_new = decay * c_mem + _dot_tn((w_state * v.astype(F32)).astype(BF16), k)
    n_new = decay * n_mem + jnp.sum(w_state * kf, axis=0, keepdims=True)
    return hid, c_new, n_new, m_new


def _mlstm_kernel(ql, kl, vl, ol, qc, kc, vc, oc, gl, gtl, gc, gtc, cos, sin, gn, yl, yc,
                  q_scr, k_scr, hf_scr, hb_scr):
    ch = M_CHUNK
    kscale = M_HEAD_DIM ** -0.5

    def rows(c):
        return slice(c * ch, (c + 1) * ch)

    q_scr[rows(0), :] = qc[...].astype(BF16)
    k_scr[rows(0), :] = (kc[...].astype(F32) * kscale).astype(BF16)
    for c in range(1, N_CHUNKS):
        lat = rows(c - 1)
        q_scr[rows(c), :] = _rope(ql[lat, :].astype(F32), cos[lat, :], sin[lat, :]).astype(BF16)
        k_scr[rows(c), :] = (_rope(kl[lat, :].astype(F32), cos[lat, :], sin[lat, :]) * kscale).astype(BF16)

    t_idx = lax.broadcasted_iota(jnp.int32, (ch, ch), 0)
    s_idx = lax.broadcasted_iota(jnp.int32, (ch, ch), 1)
    lower = s_idx <= t_idx
    upper = s_idx >= t_idx
    lower_f = lower.astype(F32)
    upper_f = upper.astype(F32)

    def gates(c):
        g = gc[...] if c == 0 else gl[rows(c - 1), :]
        gt = gtc[0] if c == 0 else gtl[c - 1]
        return g, gt

    def values(c):
        return vc[...].astype(BF16) if c == 0 else vl[rows(c - 1), :].astype(BF16)

    def init():
        return (jnp.zeros((M_HEAD_DIM, M_HEAD_DIM), F32), jnp.zeros((1, M_HEAD_DIM), F32),
                jnp.full((1, 1), -jnp.inf, F32))

    st_f, st_b = init(), init()
    for step in range(N_CHUNKS):
        cf = step
        cb = 0 if step == 0 else N_CHUNKS - step
        g, gt = gates(cf)
        cum_col = _dot_exact(lower_f, _log_sigmoid(g))[:, 1:2]
        cum_row = _dot_exact(_log_sigmoid(gt), upper_f)[1:2, :]
        hid, *st_f = _mlstm_chunk(q_scr[rows(cf), :], k_scr[rows(cf), :], values(cf), g[:, 0:1], cum_col,
                                  gt[0:1, :], cum_row, lower, ch - 1, *st_f)
        hf_scr[rows(cf), :] = hid
        g, gt = gates(cb)
        cum_col = _dot_exact(upper_f, _log_sigmoid(g))[:, 3:4]
        cum_row = _dot_exact(_log_sigmoid(gt), lower_f)[3:4, :]
        hid, *st_b = _mlstm_chunk(q_scr[rows(cb), :], k_scr[rows(cb), :], values(cb), g[:, 2:3], cum_col,
                                  gt[2:3, :], cum_row, upper, 0, *st_b)
        hb_scr[rows(cb), :] = hid

    gain = gn[...]
    for c in range(N_CHUNKS):
        hsum = hf_scr[rows(c), :] + hb_scr[rows(c), :]
        hn = hsum * lax.rsqrt(jnp.mean(hsum * hsum, axis=-1, keepdims=True) + EPS) * gain
        if c == 0:
            yc[...] = (_sigmoid(oc[...].astype(F32)) * hn).astype(yc.dtype)
        else:
            yl[rows(c - 1), :] = (_sigmoid(ol[rows(c - 1), :].astype(F32)) * hn).astype(yl.dtype)


def _mlstm(p_lat, p_ctx, g_lat, gt_lat, g_ctx, gt_ctx, cos, sin, gn):
    hd = M_HEAD_DIM
    cb = lambda col: col // hd

    def lat(col):
        return pl.BlockSpec((SEQ, hd), lambda b, h: (b, cb(col) + h))

    def ctx(col):
        return pl.BlockSpec((CTX_LEN, hd), lambda b, h: (b, cb(col) + h))

    seq_chunks = SEQ // M_CHUNK
    ctx_chunks = CTX_LEN // M_CHUNK
    return pl.pallas_call(
        _mlstm_kernel,
        grid=(BATCH, M_HEADS),
        in_specs=[
            lat(COL_MQ), lat(COL_MK), lat(COL_MV), lat(COL_MO),
            ctx(COL_MQ), ctx(COL_MK), ctx(COL_MV), ctx(COL_MO),
            pl.BlockSpec((SEQ, GATE_W), lambda b, h: (b, h)),
            pl.BlockSpec((seq_chunks, GATE_R, M_CHUNK), lambda b, h: (b, h, 0)),
            pl.BlockSpec((CTX_LEN, GATE_W), lambda b, h: (b, h)),
            pl.BlockSpec((ctx_chunks, GATE_R, M_CHUNK), lambda b, h: (b, h, 0)),
            pl.BlockSpec((SEQ, hd), lambda b, h: (0, 0)),
            pl.BlockSpec((SEQ, hd), lambda b, h: (0, 0)),
            pl.BlockSpec((1, hd), lambda b, h: (0, h)),
        ],
        out_specs=[
            pl.BlockSpec((SEQ, hd), lambda b, h: (b, h)),
            pl.BlockSpec((CTX_LEN, hd), lambda b, h: (b, h)),
        ],
        out_shape=[
            jax.ShapeDtypeStruct((BATCH * SEQ, BRANCH_W), BF16),
            jax.ShapeDtypeStruct((BATCH * CTX_LEN, BRANCH_W), BF16),
        ],
        scratch_shapes=[
            pltpu.VMEM((SEQ + CTX_LEN, hd), BF16), pltpu.VMEM((SEQ + CTX_LEN, hd), BF16),
            pltpu.VMEM((SEQ + CTX_LEN, hd), F32), pltpu.VMEM((SEQ + CTX_LEN, hd), F32),
        ],
        compiler_params=_cparams(("arbitrary", "arbitrary")),
        name="mlstm",
    )(p_lat, p_lat, p_lat, p_lat, p_ctx, p_ctx, p_ctx, p_ctx, g_lat, gt_lat, g_ctx, gt_ctx, cos, sin,
      gn.reshape(1, BRANCH_W))


def _rope_tables():
    pos = jnp.arange(SEQ)
    nf = M_HEAD_DIM // 4
    inv = ROPE_BASE ** (-jnp.arange(nf, dtype=F32) / nf)
    ang_r = (pos // GRID_W).astype(F32)[:, None] * inv[None, :]
    ang_c = (pos % GRID_W).astype(F32)[:, None] * inv[None, :]
    cr, sr, cc, sc = jnp.cos(ang_r), jnp.sin(ang_r), jnp.cos(ang_c), jnp.sin(ang_c)
    return (jnp.concatenate([cr, cr, cc, cc], axis=-1), jnp.concatenate([-sr, sr, -sc, sc], axis=-1))


NA_GROUP = 2
NA_WIN = NA_KH * GRID_W
NA_ROWS = SEQ // GRID_W


def _na_kernel(ql, kl, vl, qc, kc, vc, bias, yl, yc):
    scale = NA_HEAD_DIM ** -0.5
    kcs = [kc[:, hh * NA_HEAD_DIM:(hh + 1) * NA_HEAD_DIM].astype(BF16) for hh in range(NA_GROUP)]
    vcs = [vc[:, hh * NA_HEAD_DIM:(hh + 1) * NA_HEAD_DIM].astype(BF16) for hh in range(NA_GROUP)]

    def row_body(r, carry):
        start = jnp.clip(r - NA_KH // 2, 0, NA_ROWS - NA_KH)
        off = start - r + (NA_KH - 1)
        q_rows = pl.ds(pl.multiple_of(r * GRID_W, GRID_W), GRID_W)
        k_rows = pl.ds(pl.multiple_of(start * GRID_W, GRID_W), NA_WIN)
        outs = []
        for hh in range(NA_GROUP):
            sl = slice(hh * NA_HEAD_DIM, (hh + 1) * NA_HEAD_DIM)
            q = ql[q_rows, sl].astype(BF16)
            kw = kl[k_rows, sl].astype(BF16)
            vw = vl[k_rows, sl].astype(BF16)
            s_win = _dot_nt(q, kw) * scale + bias[hh, off]
            s_ctx = _dot_nt(q, kcs[hh]) * scale
            m = jnp.maximum(jnp.max(s_win, axis=-1, keepdims=True), jnp.max(s_ctx, axis=-1, keepdims=True))
            p_win = jnp.exp(s_win - m)
            p_ctx = jnp.exp(s_ctx - m)
            denom = jnp.sum(p_win, axis=-1, keepdims=True) + jnp.sum(p_ctx, axis=-1, keepdims=True)
            o = _dot(p_win.astype(BF16), vw) + _dot(p_ctx.astype(BF16), vcs[hh])
            outs.append(o / denom)
        yl[q_rows, :] = jnp.concatenate(outs, axis=1).astype(yl.dtype)
        return carry

    lax.fori_loop(0, NA_ROWS, row_body, 0)

    outs = []
    for hh in range(NA_GROUP):
        sl = slice(hh * NA_HEAD_DIM, (hh + 1) * NA_HEAD_DIM)
        s = _dot_nt(qc[:, sl].astype(BF16), kcs[hh]) * scale
        p = jnp.exp(s - jnp.max(s, axis=-1, keepdims=True))
        outs.append(_dot(p.astype(BF16), vcs[hh]) / jnp.sum(p, axis=-1, keepdims=True))
    yc[...] = jnp.concatenate(outs, axis=1).astype(yc.dtype)


def _na_bias_kernel(rpb_ref, o_ref):
    shape = (LANES, GRID_W * GRID_W)
    n = lax.broadcasted_iota(jnp.int32, shape, 1)
    c = lax.broadcasted_iota(jnp.int32, shape, 0)
    q = jnp.right_shift(n, GRID_W.bit_length() - 1)
    k = jnp.bitwise_and(n, GRID_W - 1)
    dc = jnp.clip(k - q, -(NA_KW - 1), NA_KW - 1) + (NA_KW - 1)
    picked = _dot_exact(rpb_ref[...], (c == dc).astype(F32))
    col_start = jnp.clip(q - NA_KW // 2, 0, GRID_W - NA_KW)
    col_ok = (k >= col_start) & (k < col_start + NA_KW)
    o_ref[...] = jnp.where(col_ok, picked, -jnp.inf)


def _na_bias(rpb):
    assert GRID_W & (GRID_W - 1) == 0 and NA_HEADS * (2 * NA_KH - 1) <= LANES and 2 * NA_KW - 1 <= LANES
    n_dr = 2 * NA_KH - 1
    flat = rpb.astype(F32).reshape(NA_HEADS * n_dr, 2 * NA_KW - 1)
    flat = jnp.pad(flat, ((0, LANES - NA_HEADS * n_dr), (0, LANES - (2 * NA_KW - 1))))
    t = pl.pallas_call(
        _na_bias_kernel,
        out_shape=jax.ShapeDtypeStruct((LANES, GRID_W * GRID_W), F32),
        name="natten_bias",
    )(flat)
    t = t[:NA_HEADS * n_dr].reshape(NA_HEADS, n_dr, GRID_W, GRID_W)
    win = jnp.stack([t[:, o:o + NA_KH] for o in range(NA_KH)], axis=1)
    return jnp.transpose(win, (0, 1, 3, 2, 4)).reshape(NA_HEADS, NA_KH, GRID_W, NA_WIN)


def _na(p_lat, p_ctx, bias):
    gw = NA_GROUP * NA_HEAD_DIM
    cb = lambda col: col // gw

    def lat(col):
        return pl.BlockSpec((SEQ, gw), lambda b, g: (b, cb(col) + g))

    def ctx(col):
        return pl.BlockSpec((CTX_LEN, gw), lambda b, g: (b, cb(col) + g))

    return pl.pallas_call(
        _na_kernel,
        grid=(BATCH, NA_HEADS // NA_GROUP),
        in_specs=[
            lat(COL_NQ), lat(COL_NK), lat(COL_NV), ctx(COL_NQ), ctx(COL_NK), ctx(COL_NV),
            pl.BlockSpec((NA_GROUP, NA_KH, GRID_W, NA_WIN), lambda b, g: (g, 0, 0, 0)),
        ],
        out_specs=[
            pl.BlockSpec((SEQ, gw), lambda b, g: (b, g)),
            pl.BlockSpec((CTX_LEN, gw), lambda b, g: (b, g)),
        ],
        out_shape=[
            jax.ShapeDtypeStruct((BATCH * SEQ, BRANCH_W), BF16),
            jax.ShapeDtypeStruct((BATCH * CTX_LEN, BRANCH_W), BF16),
        ],
        compiler_params=_cparams(("arbitrary", "arbitrary")),
        name="natten",
    )(p_lat, p_lat, p_lat, p_ctx, p_ctx, p_ctx, bias)


LRU_HALO = 2 * SUBLANES


def _lru_kernel(xl, xc, cw, cb, wa, ba, wx, bx, lam, hfl, hbl, hfc, hbc,
                ext_f, ext_b, a_f, u_f, a_b, u_b, carry):
    s = pl.program_id(1)
    ch = LRU_CHUNK
    n_lat = SEQ // ch
    halo = LRU_HALO
    zeros_halo = jnp.zeros((halo, BRANCH_W), F32)

    def fill(ext, chunk):
        @pl.when(s == 0)
        def _():
            ext[0:halo, :] = zeros_halo
            ext[halo:halo + ch, :] = xc[...].astype(F32)
            ext[halo + ch:, :] = zeros_halo

        @pl.when(s > 0)
        def _():
            base = pl.multiple_of(chunk * ch, ch)
            ext[halo:halo + ch, :] = xl[pl.ds(base, ch), :].astype(F32)
            prev = xl[pl.ds(pl.multiple_of(jnp.maximum(base - halo, 0), halo), halo), :].astype(F32)
            nxt = xl[pl.ds(pl.multiple_of(jnp.minimum(base + ch, SEQ - halo), halo), halo), :].astype(F32)
            ext[0:halo, :] = jnp.where(chunk > 0, prev, 0.0)
            ext[halo + ch:, :] = jnp.where(chunk < n_lat - 1, nxt, 0.0)

    chunk_f = jnp.maximum(s - 1, 0)
    chunk_b = jnp.maximum(n_lat - s, 0)
    fill(ext_f, chunk_f)
    fill(ext_b, chunk_b)

    def coeffs(ext, d, a_out, u_out):
        xconv = cb[...] + ext[halo - 1:halo - 1 + ch, :] * cw[0:1, :]
        for j in range(1, LRU_CONV):
            xconv = xconv + ext[halo - 1 + j:halo - 1 + j + ch, :] * cw[j:j + 1, :]
        sp = _softplus(-lam[d:d + 1, :])
        for g in range(LRU_BLOCKS):
            sl = slice(g * LRU_BLOCK_DIM, (g + 1) * LRU_BLOCK_DIM)
            xg = xconv[:, sl]
            xb = xg.astype(BF16)
            r = _sigmoid(_dot(xb, wa[d, g]) + ba[d:d + 1, sl])
            i = _sigmoid(_dot(xb, wx[d, g]) + bx[d:d + 1, sl])
            log_a = -LRU_C * r * sp[:, sl]
            a = jnp.exp(log_a)
            a_out[:, sl] = a
            u_out[:, sl] = jnp.sqrt(jnp.tanh(-log_a) * (a * a + 1.0)) * i * xg

    coeffs(ext_f, 0, a_f, u_f)
    coeffs(ext_b, 1, a_b, u_b)

    @pl.when(s == 0)
    def _():
        carry[...] = jnp.zeros_like(carry)

    row = lax.broadcasted_iota(jnp.int32, (SUBLANES, BRANCH_W), 0)
    n_tiles = ch // SUBLANES

    def scan_tiles(hf_out, hb_out):
        def body(t, hs):
            h_f, h_b = hs
            r0 = pl.multiple_of(t * SUBLANES, SUBLANES)
            a = a_f[pl.ds(r0, SUBLANES), :]
            u = u_f[pl.ds(r0, SUBLANES), :]
            for d in (1, 2, 4):
                ok = row >= d
                u = jnp.where(ok, a * pltpu.roll(u, d, 0) + u, u)
                a = jnp.where(ok, a * pltpu.roll(a, d, 0), a)
            h = u + a * h_f
            hf_out[pl.ds(r0, SUBLANES), :] = h
            h_f = h[SUBLANES - 1:SUBLANES, :]
            r1 = pl.multiple_of((n_tiles - 1 - t) * SUBLANES, SUBLANES)
            a = a_b[pl.ds(r1, SUBLANES), :]
            u = u_b[pl.ds(r1, SUBLANES), :]
            for d in (1, 2, 4):
                ok = row < SUBLANES - d
                u = jnp.where(ok, a * pltpu.roll(u, SUBLANES - d, 0) + u, u)
                a = jnp.where(ok, a * pltpu.roll(a, SUBLANES - d, 0), a)
            h = u + a * h_b
            hb_out[pl.ds(r1, SUBLANES), :] = h
            h_b = h[0:1, :]
            return h_f, h_b

        h_f, h_b = lax.fori_loop(0, n_tiles, body, (carry[0:1, :], carry[1:2, :]))
        carry[0:1, :] = h_f
        carry[1:2, :] = h_b

    @pl.when(s == 0)
    def _():
        scan_tiles(hfc, hbc)

    @pl.when(s > 0)
    def _():
        scan_tiles(hfl, hbl)


def _lru(p_lat, p_ctx, conv_w, conv_b, w_a, b_a, w_x, b_x, lam):
    ch = LRU_CHUNK
    n_lat = SEQ // ch
    full = lambda shape: pl.BlockSpec(shape, lambda b, s: (0,) * len(shape))
    wblk = (2, LRU_BLOCKS, LRU_BLOCK_DIM, LRU_BLOCK_DIM)
    return pl.pallas_call(
        _lru_kernel,
        grid=(BATCH, LRU_STEPS),
        in_specs=[
            pl.BlockSpec((SEQ, BRANCH_W), lambda b, s: (b, COL_LX // BRANCH_W)),
            pl.BlockSpec((CTX_LEN, BRANCH_W), lambda b, s: (b, COL_LX // BRANCH_W)),
            full((LRU_CONV, BRANCH_W)), full((1, BRANCH_W)),
            full(wblk), full((2, BRANCH_W)), full(wblk), full((2, BRANCH_W)), full((2, BRANCH_W)),
        ],
        out_specs=[
            pl.BlockSpec((ch, BRANCH_W), lambda b, s: (b * n_lat + jnp.maximum(s - 1, 0), 0)),
            pl.BlockSpec((ch, BRANCH_W), lambda b, s: (b * n_lat + jnp.minimum(n_lat - s, n_lat - 1), 0)),
            pl.BlockSpec((CTX_LEN, BRANCH_W), lambda b, s: (b, 0)),
            pl.BlockSpec((CTX_LEN, BRANCH_W), lambda b, s: (b, 0)),
        ],
        out_shape=[
            jax.ShapeDtypeStruct((BATCH * SEQ, BRANCH_W), F32),
            jax.ShapeDtypeStruct((BATCH * SEQ, BRANCH_W), F32),
            jax.ShapeDtypeStruct((BATCH * CTX_LEN, BRANCH_W), F32),
            jax.ShapeDtypeStruct((BATCH * CTX_LEN, BRANCH_W), F32),
        ],
        scratch_shapes=[
            pltpu.VMEM((ch + 2 * LRU_HALO, BRANCH_W), F32), pltpu.VMEM((ch + 2 * LRU_HALO, BRANCH_W), F32),
            pltpu.VMEM((ch, BRANCH_W), F32), pltpu.VMEM((ch, BRANCH_W), F32),
            pltpu.VMEM((ch, BRANCH_W), F32), pltpu.VMEM((ch, BRANCH_W), F32),
            pltpu.VMEM((SUBLANES, BRANCH_W), F32),
        ],
        compiler_params=_cparams(("arbitrary", "arbitrary")),
        name="rglru",
    )(p_lat, p_ctx, conv_w, conv_b.reshape(1, BRANCH_W), w_a, b_a, w_x, b_x, lam)


def _merge_kernel(ym, yn, hf, hb, lg, g0, g1, g2, wb, o_ref, yl_s):
    j = pl.program_id(1)

    @pl.when(j == 0)
    def _():
        yl_s[...] = ((hf[...] + hb[...]) * jax.nn.gelu(lg[...].astype(F32))).astype(BF16)

    acc = _sigmoid(g0[...].astype(F32)) * _dot(ym[...], wb[0])
    acc = acc + _sigmoid(g1[...].astype(F32)) * _dot(yn[...], wb[1])
    acc = acc + _sigmoid(g2[...].astype(F32)) * _dot(yl_s[...], wb[2])
    o_ref[...] = acc.astype(BF16)


def _merge(ym, yn, hf, hb, p, w_branch):
    rows = ym.shape[0]
    tn = TN_MERGE
    tm = TM_MERGE
    br = lambda: pl.BlockSpec((tm, BRANCH_W), lambda i, j: (i, 0))

    def gate(n):
        return pl.BlockSpec((tm, tn), lambda i, j: (i, (COL_GX + n * D_MODEL) // tn + j))

    return pl.pallas_call(
        _merge_kernel,
        grid=(rows // tm, D_MODEL // tn),
        in_specs=[
            br(), br(), br(), br(),
            pl.BlockSpec((tm, BRANCH_W), lambda i, j: (i, COL_LG // BRANCH_W)),
            gate(0), gate(1), gate(2),
            pl.BlockSpec((None, N_BRANCH, BRANCH_W, tn), lambda i, j: (j, 0, 0, 0)),
        ],
        out_specs=pl.BlockSpec((tm, tn), lambda i, j: (i, j)),
        out_shape=jax.ShapeDtypeStruct((rows, D_MODEL), BF16),
        scratch_shapes=[pltpu.VMEM((tm, BRANCH_W), BF16)],
        compiler_params=_cparams(("arbitrary", "arbitrary")),
        name="merge",
    )(ym, yn, hf, hb, p, p, p, p, w_branch)


def _outproj_kernel(s_ref, w_ref, x_ref, g_ref, o_ref):
    o_ref[...] = x_ref[...] + g_ref[...] * _dot(s_ref[...], w_ref[...])


def _outproj(s, w_out, x, mod_l, row_fn):
    rows = x.shape[0]
    tn = TN_OUT
    tm = TM_OUT
    row_of_tile = row_fn(tm)
    return pl.pallas_call(
        _outproj_kernel,
        grid=(rows // tm, D_MODEL // tn),
        in_specs=[
            pl.BlockSpec((tm, D_MODEL), lambda i, j: (i, 0)),
            pl.BlockSpec((None, D_MODEL, tn), lambda i, j: (j, 0, 0)),
            pl.BlockSpec((tm, tn), lambda i, j: (i, j)),
            pl.BlockSpec((None, None, 1, tn), lambda i, j: (row_of_tile(i), 5, 0, j)),
        ],
        out_specs=pl.BlockSpec((tm, tn), lambda i, j: (i, j)),
        out_shape=jax.ShapeDtypeStruct((rows, D_MODEL), F32),
        compiler_params=_cparams(("arbitrary", "arbitrary")),
        name="outproj",
    )(s, w_out, x, mod_l)


def _split_w_in(w):
    bw = BRANCH_W
    n_gate = 4 * M_HEADS
    o_mg = 4 * bw
    o_n = o_mg + n_gate
    main = _col_tiles(jnp.concatenate([w[:, :o_mg], w[:, o_n:]], axis=1).astype(BF16), TN_IN)
    mg = w[:, o_mg:o_n].reshape(D_MODEL, 2, 2, M_HEADS)
    mg = jnp.transpose(mg, (0, 3, 1, 2)).reshape(D_MODEL, M_HEADS, 4)
    wg = jnp.pad(mg, ((0, 0), (0, 0), (0, GATE_W - 4))).reshape(D_MODEL, M_HEADS * GATE_W).astype(BF16)
    wgt = jnp.pad(jnp.transpose(mg, (1, 2, 0)), ((0, 0), (0, GATE_R - 4), (0, 0)))
    wgt = wgt.reshape(M_HEADS * GATE_R, D_MODEL).astype(BF16)
    return main, wg, wgt


def _gate_bias(b_i, b_f):
    b = jnp.stack([b_i, b_f], axis=1)
    b = jnp.transpose(b, (2, 0, 1)).reshape(M_HEADS, 4).astype(F32)
    bg = jnp.pad(b, ((0, 0), (0, GATE_W - 4))).reshape(1, M_HEADS * GATE_W)
    bgt = jnp.pad(b, ((0, 0), (0, GATE_R - 4))).reshape(M_HEADS * GATE_R, 1)
    return bg, bgt


def kernel(x, c, ctx, c_ctx, w_mod, b_mod, norm_ffn1, norm_mix, norm_ffn2, ffn1_w_in, ffn1_w_out, ffn2_w_in,
           ffn2_w_out, w_in, mlstm_b_i, mlstm_b_f, mlstm_gn, na_rpb, lru_conv_w, lru_conv_b, lru_w_a, lru_b_a,
           lru_w_x, lru_b_x, lru_lambda, w_branch, w_out, norm_final):
    assert x.shape == (BATCH, SEQ, D_MODEL) and ctx.shape == (BATCH, CTX_LEN, D_MODEL)
    xl = x.reshape(BATCH * SEQ, D_MODEL)
    xc = ctx.reshape(BATCH * CTX_LEN, D_MODEL)
    c_rows = jnp.concatenate([c, c_ctx[None, :], jnp.zeros((MOD_ROWS - BATCH - 1, D_MODEL), F32)], axis=0)
    mod = _modulation(c_rows, w_mod, b_mod).reshape(DEPTH, MOD_ROWS, N_MOD, 1, D_MODEL)
    cos, sin = _rope_tables()

    for li in range(DEPTH):
        last = li == DEPTH - 1
        mod_l = mod[li]
        w1_in, w1_out = _col_tiles(ffn1_w_in[li].astype(BF16), TF), ffn1_w_out[li].astype(BF16)
        w2_in, w2_out = _col_tiles(ffn2_w_in[li].astype(BF16), TF), ffn2_w_out[li].astype(BF16)
        w_main, wg, wgt = _split_w_in(w_in[li])
        bg, bgt = _gate_bias(mlstm_b_i[li], mlstm_b_f[li])
        wb = w_branch[li].astype(BF16).reshape(N_BRANCH, BRANCH_W, D_MODEL // TN_MERGE, TN_MERGE)
        wb = jnp.transpose(wb, (2, 0, 1, 3))
        wo = _col_tiles(w_out[li].astype(BF16), TN_OUT)
        bias = _na_bias(na_rpb[li])
        wa, wx = lru_w_a[li].astype(BF16), lru_w_x[li].astype(BF16)

        xl = _ffn(xl, mod_l, 0, norm_ffn1[li], w1_in, w1_out, _lat_row)
        xc = _ffn(xc, mod_l, 0, norm_ffn1[li], w1_in, w1_out, _ctx_row)

        p_lat, g_lat, gt_lat = _inproj(xl, mod_l, norm_mix[li], w_main, wg, bg, wgt, bgt, _lat_row)
        p_ctx, g_ctx, gt_ctx = _inproj(xc, mod_l, norm_mix[li], w_main, wg, bg, wgt, bgt, _ctx_row)

        ym_l, ym_c = _mlstm(p_lat, p_ctx, g_lat, gt_lat, g_ctx, gt_ctx, cos, sin, mlstm_gn[li])
        yn_l, yn_c = _na(p_lat, p_ctx, bias)
        hf_l, hb_l, hf_c, hb_c = _lru(p_lat, p_ctx, lru_conv_w[li], lru_conv_b[li], wa, lru_b_a[li], wx,
                                      lru_b_x[li], lru_lambda[li])

        s_l = _merge(ym_l, yn_l, hf_l, hb_l, p_lat, wb)
        xl = _outproj(s_l, wo, xl, mod_l, _lat_row)
        xl = _ffn(xl, mod_l, 6, norm_ffn2[li], w2_in, w2_out, _lat_row, norm_final if last else None)
        if not last:
            s_c = _merge(ym_c, yn_c, hf_c, hb_c, p_ctx, wb)
            xc = _outproj(s_c, wo, xc, mod_l, _ctx_row)
            xc = _ffn(xc, mod_l, 6, norm_ffn2[li], w2_in, w2_out, _ctx_row)
    return xl.reshape(BATCH, SEQ, D_MODEL)
```

```python
import functools

import jax
import jax.numpy as jnp
from jax import lax
from jax.experimental import pallas as pl
from jax.experimental.pallas import tpu as pltpu

F32 = jnp.float32
BF16 = jnp.bfloat16

D_MODEL = 2048
BATCH = 4
SEQ = 2048
DEPTH = 2
GRID_W = 64
CTX_LEN = 256
EPS = 1e-6
D_FF = (11 * D_MODEL) // 4
N_MOD = 9
BRANCH_W = D_MODEL // 2
N_BRANCH = 3
M_HEADS = 4
M_HEAD_DIM = BRANCH_W // M_HEADS
ROPE_BASE = 10000.0
NA_HEADS = 8
NA_HEAD_DIM = BRANCH_W // NA_HEADS
NA_KH = 8
NA_KW = 16
LRU_BLOCKS = 8
LRU_BLOCK_DIM = BRANCH_W // LRU_BLOCKS
LRU_CONV = 4
LRU_C = 8.0

SUBLANES = 8
LANES = 128

MOD_ROWS = 8
CTX_MOD_ROW = BATCH
TM_FFN = 512
TM_IN = 1024
TM_MERGE = 1024
TM_OUT = 1024
TF = 512
TN_IN = 1024
TN_MERGE = 512
TN_OUT = 512
TN_MOD = 1024
M_CHUNK = 256
N_CHUNKS = (SEQ + CTX_LEN) // M_CHUNK
LRU_CHUNK = 256
LRU_STEPS = (SEQ + CTX_LEN) // LRU_CHUNK
GATE_R = SUBLANES

COL_MQ, COL_MK, COL_MV, COL_MO = 0, BRANCH_W, 2 * BRANCH_W, 3 * BRANCH_W
COL_NQ, COL_NK, COL_NV = 4 * BRANCH_W, 5 * BRANCH_W, 6 * BRANCH_W
COL_LX, COL_LG = 7 * BRANCH_W, 8 * BRANCH_W
COL_GX = 9 * BRANCH_W
P_COLS = COL_GX + N_BRANCH * D_MODEL

VMEM_BYTES = 64 * 1024 * 1024
VMEM_LIMIT = VMEM_BYTES - 8 * 1024 * 1024


def _cparams(sem):
    return pltpu.CompilerParams(dimension_semantics=sem, vmem_limit_bytes=VMEM_LIMIT)


def _sigmoid(x):
    return jax.nn.sigmoid(x)


def _log_sigmoid(x):
    return jnp.minimum(x, 0.0) - jnp.log1p(jnp.exp(-jnp.abs(x)))


def _softplus(x):
    return jnp.maximum(x, 0.0) + jnp.log1p(jnp.exp(-jnp.abs(x)))


def _dot(a, b):
    return jnp.dot(a, b, preferred_element_type=F32)


def _dot_nt(a, b):
    return lax.dot_general(a, b, (((1,), (1,)), ((), ())), preferred_element_type=F32)


def _dot_tn(a, b):
    return lax.dot_general(a, b, (((0,), (0,)), ((), ())), preferred_element_type=F32)


def _dot_exact(a, b):
    return jnp.dot(a, b, preferred_element_type=F32, precision=lax.Precision.HIGHEST)


def _dot_split(x, m):
    hi = x.astype(BF16)
    r1 = x - hi.astype(F32)
    mid = r1.astype(BF16)
    lo = (r1 - mid.astype(F32)).astype(BF16)
    return _dot(hi, m) + _dot(mid, m) + _dot(lo, m)


def _adaln(x, gain, shift, scale):
    y = x * lax.rsqrt(jnp.mean(x * x, axis=-1, keepdims=True) + EPS)
    return (y * gain) * (1.0 + scale) + shift


def _mod_kernel(c_ref, w_ref, b_ref, o_ref):
    c = c_ref[...]
    act = (c * _sigmoid(c)).astype(BF16)
    o_ref[...] = _dot(act, w_ref[...].astype(BF16)) + b_ref[...]


def _modulation(c_rows, w_mod, b_mod):
    n_cols = N_MOD * D_MODEL
    return pl.pallas_call(
        _mod_kernel,
        grid=(DEPTH, n_cols // TN_MOD),
        in_specs=[
            pl.BlockSpec((MOD_ROWS, D_MODEL), lambda l, j: (0, 0)),
            pl.BlockSpec((None, D_MODEL, TN_MOD), lambda l, j: (l, 0, j)),
            pl.BlockSpec((None, 1, TN_MOD), lambda l, j: (l, 0, j)),
        ],
        out_specs=pl.BlockSpec((None, MOD_ROWS, TN_MOD), lambda l, j: (l, 0, j)),
        out_shape=jax.ShapeDtypeStruct((DEPTH, MOD_ROWS, n_cols), F32),
        compiler_params=_cparams(("arbitrary", "arbitrary")),
        name="modulation",
    )(c_rows, w_mod, b_mod.reshape(DEPTH, 1, n_cols))


def _mod_spec(k, row_of_tile):
    return pl.BlockSpec((None, None, 1, D_MODEL), lambda i, j: (row_of_tile(i), k, 0, 0))


def _lat_row(tm):
    return lambda i: i // (SEQ // tm)


def _ctx_row(tm):
    return lambda i: CTX_MOD_ROW


def _ffn_kernel(*refs, final):
    if final:
        x_ref, sh_ref, sc_ref, g_ref, ng_ref, wg_ref, wu_ref, wo_ref, fg_ref, o_ref, h_scr = refs
    else:
        x_ref, sh_ref, sc_ref, g_ref, ng_ref, wg_ref, wu_ref, wo_ref, o_ref, h_scr = refs
    f = pl.program_id(1)

    @pl.when(f == 0)
    def _():
        h = _adaln(x_ref[...], ng_ref[...], sh_ref[...], sc_ref[...])
        h_scr[...] = h.astype(BF16)
        o_ref[...] = jnp.zeros_like(o_ref)

    h = h_scr[...]
    gte = _dot(h, wg_ref[...])
    up = _dot(h, wu_ref[...])
    act = (gte * _sigmoid(gte) * up).astype(BF16)
    o_ref[...] += _dot(act, wo_ref[...])

    @pl.when(f == pl.num_programs(1) - 1)
    def _():
        out = x_ref[...] + 0.5 * g_ref[...] * o_ref[...]
        if final:
            out = out * lax.rsqrt(jnp.mean(out * out, axis=-1, keepdims=True) + EPS) * fg_ref[...]
        o_ref[...] = out


def _ffn(x, mod_l, k0, norm_g, w_in, w_out, row_fn, final_gain=None):
    rows = x.shape[0]
    nf = D_FF // TF
    tm = TM_FFN
    row_of_tile = row_fn(tm)
    final = final_gain is not None
    in_specs = [
        pl.BlockSpec((tm, D_MODEL), lambda i, f: (i, 0)),
        _mod_spec(k0, row_of_tile), _mod_spec(k0 + 1, row_of_tile), _mod_spec(k0 + 2, row_of_tile),
        pl.BlockSpec((1, D_MODEL), lambda i, f: (0, 0)),
        pl.BlockSpec((D_MODEL, TF), lambda i, f: (0, f)),
        pl.BlockSpec((D_MODEL, TF), lambda i, f: (0, f + nf)),
        pl.BlockSpec((TF, D_MODEL), lambda i, f: (f, 0)),
    ]
    args = [x, mod_l, mod_l, mod_l, norm_g.reshape(1, D_MODEL), w_in, w_in, w_out]
    if final:
        in_specs.append(pl.BlockSpec((1, D_MODEL), lambda i, f: (0, 0)))
        args.append(final_gain.reshape(1, D_MODEL))
    return pl.pallas_call(
        functools.partial(_ffn_kernel, final=final),
        grid=(rows // tm, nf),
        in_specs=in_specs,
        out_specs=pl.BlockSpec((tm, D_MODEL), lambda i, f: (i, 0)),
        out_shape=jax.ShapeDtypeStruct((rows, D_MODEL), F32),
        scratch_shapes=[pltpu.VMEM((tm, D_MODEL), BF16)],
        compiler_params=_cparams(("arbitrary", "arbitrary")),
        name="ffn_final" if final else "ffn",
    )(*args)


N_HEAD_TILES = COL_NQ // TN_IN


def _inproj_kernel(x_ref, sh_ref, sc_ref, ng_ref, wa_ref, wb_ref, wgt_ref, bgt_ref, p_ref, gt_ref, h_scr):
    j = pl.program_id(1)

    @pl.when(j == 0)
    def _():
        h = _adaln(x_ref[...], ng_ref[...], sh_ref[...], sc_ref[...]).astype(BF16)
        h_scr[...] = h
        for s in range(TM_IN // M_CHUNK):
            gt_ref[s] = _dot_nt(wgt_ref[...], h[s * M_CHUNK:(s + 1) * M_CHUNK, :]) + bgt_ref[...]

    @pl.when(j < N_HEAD_TILES)
    def _():
        p_ref[...] = _dot(h_scr[...], wa_ref[...]).astype(p_ref.dtype)

    @pl.when(j >= N_HEAD_TILES)
    def _():
        p_ref[...] = _dot(h_scr[...], wb_ref[...]).astype(p_ref.dtype)


def _inproj(x, mod_l, norm_g, wa, wb, wgt, bgt, row_fn):
    rows = x.shape[0]
    tm = TM_IN
    row_of_tile = row_fn(tm)
    return pl.pallas_call(
        _inproj_kernel,
        grid=(rows // tm, P_COLS // TN_IN),
        in_specs=[
            pl.BlockSpec((tm, D_MODEL), lambda i, j: (i, 0)),
            _mod_spec(3, row_of_tile), _mod_spec(4, row_of_tile),
            pl.BlockSpec((1, D_MODEL), lambda i, j: (0, 0)),
            pl.BlockSpec((D_MODEL, TN_IN), lambda i, j: (0, jnp.minimum(j, N_HEAD_TILES - 1))),
            pl.BlockSpec((D_MODEL, TN_IN), lambda i, j: (0, jnp.maximum(j - N_HEAD_TILES, 0))),
            pl.BlockSpec((M_HEADS * GATE_R, D_MODEL), lambda i, j: (0, 0)),
            pl.BlockSpec((M_HEADS * GATE_R, 1), lambda i, j: (0, 0)),
        ],
        out_specs=[
            pl.BlockSpec((tm, TN_IN), lambda i, j: (i, j)),
            pl.BlockSpec((tm // M_CHUNK, M_HEADS * GATE_R, M_CHUNK), lambda i, j: (i, 0, 0)),
        ],
        out_shape=[
            jax.ShapeDtypeStruct((rows, P_COLS), BF16),
            jax.ShapeDtypeStruct((rows // M_CHUNK, M_HEADS * GATE_R, M_CHUNK), F32),
        ],
        scratch_shapes=[pltpu.VMEM((tm, D_MODEL), BF16)],
        compiler_params=_cparams(("arbitrary", "arbitrary")),
        name="inproj",
    )(x, mod_l, mod_l, norm_g.reshape(1, D_MODEL), wa, wb, wgt, bgt)


def _rope(u, cs, sn):
    halves = []
    for j in range(M_HEAD_DIM // LANES):
        sl = slice(j * LANES, (j + 1) * LANES)
        uj = u[:, sl]
        halves.append(uj * cs[:, sl] + pltpu.roll(uj, LANES // 2, 1) * sn[:, sl])
    return jnp.concatenate(halves, axis=1)


def _mlstm_chunk(q, k, v, i_col, cum_col, i_row, cum_row, mask, last, c_mem, n_mem, m_prev):
    logw = jnp.where(mask, cum_col - cum_row + i_row, -jnp.inf)
    m_row = jnp.maximum(cum_col + m_prev, jnp.max(logw, axis=-1, keepdims=True))
    m_new = m_row[last:last + 1, :]
    cum_last = cum_col[last:last + 1, :]
    w_state = jnp.exp(cum_last - cum_col + i_col - m_new)
    decay = jnp.exp(cum_last + m_prev - m_new)
    inter = jnp.exp(cum_col + m_prev - m_row)
    kf = k.astype(F32)
    sc = _dot_nt(q, k) * jnp.exp(logw - m_row)
    num = inter * _dot_nt(q, c_mem.astype(BF16)) + _dot(sc.astype(BF16), v)
    den = inter * jnp.sum(q.astype(F32) * n_mem, axis=-1, keepdims=True) + jnp.sum(sc, axis=-1, keepdims=True)
    hid = num / jnp.maximum(jnp.abs(den), jnp.exp(-m_row))
    c_new = decay * c_mem + _dot_tn((w_state * v.astype(F32)).astype(BF16), k)
    n_new = decay * n_mem + jnp.sum(w_state * kf, axis=0, keepdims=True)
    return hid, c_new, n_new, m_new


def _mlstm_kernel(ql, kl, vl, ol, qc, kc, vc, oc, gtl, gtc, cos, sin, gn, yl, yc,
                  q_scr, k_scr, hf_scr, hb_scr, grow_scr, gcol_scr):
    ch = M_CHUNK
    kscale = M_HEAD_DIM ** -0.5

    def rows(c):
        return slice(c * ch, (c + 1) * ch)

    t_idx = lax.broadcasted_iota(jnp.int32, (ch, ch), 0)
    s_idx = lax.broadcasted_iota(jnp.int32, (ch, ch), 1)
    lower = s_idx <= t_idx
    upper = s_idx >= t_idx
    lower_b = lower.astype(BF16)
    upper_b = upper.astype(BF16)
    gate_row = lax.broadcasted_iota(jnp.int32, (GATE_R, ch), 0)
    lane_pad = jnp.zeros((LANES - GATE_R, ch), F32)

    for c in range(N_CHUNKS):
        if c == 0:
            q_scr[rows(0), :] = qc[...].astype(BF16)
            k_scr[rows(0), :] = (kc[...].astype(F32) * kscale).astype(BF16)
        else:
            lat = rows(c - 1)
            q_scr[rows(c), :] = _rope(ql[lat, :].astype(F32), cos[lat, :], sin[lat, :]).astype(BF16)
            k_scr[rows(c), :] = (_rope(kl[lat, :].astype(F32), cos[lat, :], sin[lat, :]) * kscale).astype(BF16)
        gt = gtc[0] if c == 0 else gtl[c - 1]
        lf = _log_sigmoid(gt)
        terms = jnp.where(gate_row == 1, _dot_split(lf, upper_b), jnp.where(gate_row == 3, _dot_split(lf, lower_b), gt))
        grow_scr[c] = terms
        gcol_scr[rows(c), :] = jnp.concatenate([terms, lane_pad], axis=0).T

    def values(c):
        return vc[...].astype(BF16) if c == 0 else vl[rows(c - 1), :].astype(BF16)

    def init():
        return (jnp.zeros((M_HEAD_DIM, M_HEAD_DIM), F32), jnp.zeros((1, M_HEAD_DIM), F32),
                jnp.full((1, 1), -jnp.inf, F32))

    def run(c, d, mask, last, state):
        g_rows = grow_scr[c]
        g_cols = gcol_scr[rows(c), :]
        i0, f0 = 2 * d, 2 * d + 1
        return _mlstm_chunk(q_scr[rows(c), :], k_scr[rows(c), :], values(c), g_cols[:, i0:i0 + 1],
                            g_cols[:, f0:f0 + 1], g_rows[i0:i0 + 1, :], g_rows[f0:f0 + 1, :], mask, last, *state)

    st_f, st_b = init(), init()
    for step in range(N_CHUNKS):
        cf = step
        cb = 0 if step == 0 else N_CHUNKS - step
        hid, *st_f = run(cf, 0, lower, ch - 1, st_f)
        hf_scr[rows(cf), :] = hid
        hid, *st_b = run(cb, 1, upper, 0, st_b)
        hb_scr[rows(cb), :] = hid

    gain = gn[...]
    for c in range(N_CHUNKS):
        hsum = hf_scr[rows(c), :] + hb_scr[rows(c), :]
        hn = hsum * lax.rsqrt(jnp.mean(hsum * hsum, axis=-1, keepdims=True) + EPS) * gain
        if c == 0:
            yc[...] = (_sigmoid(oc[...].astype(F32)) * hn).astype(yc.dtype)
        else:
            yl[rows(c - 1), :] = (_sigmoid(ol[rows(c - 1), :].astype(F32)) * hn).astype(yl.dtype)


def _mlstm(p_lat, p_ctx, gt_lat, gt_ctx, cos, sin, gn):
    hd = M_HEAD_DIM
    cb = lambda col: col // hd

    def lat(col):
        return pl.BlockSpec((SEQ, hd), lambda b, h: (b, cb(col) + h))

    def ctx(col):
        return pl.BlockSpec((CTX_LEN, hd), lambda b, h: (b, cb(col) + h))

    seq_chunks = SEQ // M_CHUNK
    ctx_chunks = CTX_LEN // M_CHUNK
    return pl.pallas_call(
        _mlstm_kernel,
        grid=(BATCH, M_HEADS),
        in_specs=[
            lat(COL_MQ), lat(COL_MK), lat(COL_MV), lat(COL_MO),
            ctx(COL_MQ), ctx(COL_MK), ctx(COL_MV), ctx(COL_MO),
            pl.BlockSpec((seq_chunks, GATE_R, M_CHUNK), lambda b, h: (b, h, 0)),
            pl.BlockSpec((ctx_chunks, GATE_R, M_CHUNK), lambda b, h: (b, h, 0)),
            pl.BlockSpec((SEQ, hd), lambda b, h: (0, 0)),
            pl.BlockSpec((SEQ, hd), lambda b, h: (0, 0)),
            pl.BlockSpec((1, hd), lambda b, h: (0, h)),
        ],
        out_specs=[
            pl.BlockSpec((SEQ, hd), lambda b, h: (b, h)),
            pl.BlockSpec((CTX_LEN, hd), lambda b, h: (b, h)),
        ],
        out_shape=[
            jax.ShapeDtypeStruct((BATCH * SEQ, BRANCH_W), BF16),
            jax.ShapeDtypeStruct((BATCH * CTX_LEN, BRANCH_W), BF16),
        ],
        scratch_shapes=[
            pltpu.VMEM((SEQ + CTX_LEN, hd), BF16), pltpu.VMEM((SEQ + CTX_LEN, hd), BF16),
            pltpu.VMEM((SEQ + CTX_LEN, hd), F32), pltpu.VMEM((SEQ + CTX_LEN, hd), F32),
            pltpu.VMEM((N_CHUNKS, GATE_R, M_CHUNK), F32), pltpu.VMEM((SEQ + CTX_LEN, LANES), F32),
        ],
        compiler_params=_cparams(("arbitrary", "arbitrary")),
        name="mlstm",
    )(p_lat, p_lat, p_lat, p_lat, p_ctx, p_ctx, p_ctx, p_ctx, gt_lat, gt_ctx, cos, sin, gn.reshape(1, BRANCH_W))


def _rope_tables():
    pos = jnp.arange(SEQ)
    nf = M_HEAD_DIM // 4
    inv = ROPE_BASE ** (-jnp.arange(nf, dtype=F32) / nf)
    ang_r = (pos // GRID_W).astype(F32)[:, None] * inv[None, :]
    ang_c = (pos % GRID_W).astype(F32)[:, None] * inv[None, :]
    cr, sr, cc, sc = jnp.cos(ang_r), jnp.sin(ang_r), jnp.cos(ang_c), jnp.sin(ang_c)
    return (jnp.concatenate([cr, cr, cc, cc], axis=-1), jnp.concatenate([-sr, sr, -sc, sc], axis=-1))


NA_GROUP = 2
NA_ROWS = SEQ // GRID_W
NA_QROWS = 4
NA_UNION = NA_KH + NA_QROWS
NA_QT = NA_QROWS * GRID_W
NA_KT = NA_UNION * GRID_W
NA_QGROUPS = NA_ROWS // NA_QROWS
NA_TYPES = 3


def _na_union_start(g):
    return max(0, min(NA_QROWS * g - NA_KH // 2, NA_ROWS - NA_UNION))


def _na_kernel(ql, kl, vl, qc, kc, vc, bias, yl, yc):
    scale = NA_HEAD_DIM ** -0.5
    kcs = [kc[:, hh * NA_HEAD_DIM:(hh + 1) * NA_HEAD_DIM].astype(BF16) for hh in range(NA_GROUP)]
    vcs = [vc[:, hh * NA_HEAD_DIM:(hh + 1) * NA_HEAD_DIM].astype(BF16) for hh in range(NA_GROUP)]

    def group_body(g, carry):
        u = jnp.clip(NA_QROWS * g - NA_KH // 2, 0, NA_ROWS - NA_UNION)
        kind = jnp.where(g == 0, 0, jnp.where(g == NA_QGROUPS - 1, NA_TYPES - 1, 1))
        q_rows = pl.ds(pl.multiple_of(g * NA_QT, NA_QT), NA_QT)
        k_rows = pl.ds(pl.multiple_of(u * GRID_W, GRID_W), NA_KT)
        outs = []
        for hh in range(NA_GROUP):
            sl = slice(hh * NA_HEAD_DIM, (hh + 1) * NA_HEAD_DIM)
            q = ql[q_rows, sl].astype(BF16)
            kw = kl[k_rows, sl].astype(BF16)
            vw = vl[k_rows, sl].astype(BF16)
            s_win = _dot_nt(q, kw) * scale + bias[hh, kind]
            s_ctx = _dot_nt(q, kcs[hh]) * scale
            m = jnp.maximum(jnp.max(s_win, axis=-1, keepdims=True), jnp.max(s_ctx, axis=-1, keepdims=True))
            p_win = jnp.exp(s_win - m)
            p_ctx = jnp.exp(s_ctx - m)
            denom = jnp.sum(p_win, axis=-1, keepdims=True) + jnp.sum(p_ctx, axis=-1, keepdims=True)
            o = _dot(p_win.astype(BF16), vw) + _dot(p_ctx.astype(BF16), vcs[hh])
            outs.append(o / denom)
        yl[q_rows, :] = jnp.concatenate(outs, axis=1).astype(yl.dtype)
        return carry

    lax.fori_loop(0, NA_QGROUPS, group_body, 0)

    outs = []
    for hh in range(NA_GROUP):
        sl = slice(hh * NA_HEAD_DIM, (hh + 1) * NA_HEAD_DIM)
        s = _dot_nt(qc[:, sl].astype(BF16), kcs[hh]) * scale
        p = jnp.exp(s - jnp.max(s, axis=-1, keepdims=True))
        outs.append(_dot(p.astype(BF16), vcs[hh]) / jnp.sum(p, axis=-1, keepdims=True))
    yc[...] = jnp.concatenate(outs, axis=1).astype(yc.dtype)


def _na_bias_kernel(rpb_ref, o_ref):
    shape = (LANES, GRID_W * GRID_W)
    n = lax.broadcasted_iota(jnp.int32, shape, 1)
    c = lax.broadcasted_iota(jnp.int32, shape, 0)
    q = jnp.right_shift(n, GRID_W.bit_length() - 1)
    k = jnp.bitwise_and(n, GRID_W - 1)
    dc = jnp.clip(k - q, -(NA_KW - 1), NA_KW - 1) + (NA_KW - 1)
    picked = _dot_exact(rpb_ref[...], (c == dc).astype(F32))
    col_start = jnp.clip(q - NA_KW // 2, 0, GRID_W - NA_KW)
    col_ok = (k >= col_start) & (k < col_start + NA_KW)
    o_ref[...] = jnp.where(col_ok, picked, -jnp.inf)


def _na_bias(rpb):
    assert _na_union_start(1) == 0 and _na_union_start(2) == NA_QROWS
    assert GRID_W & (GRID_W - 1) == 0 and NA_HEADS * (2 * NA_KH - 1) <= LANES and 2 * NA_KW - 1 <= LANES
    n_dr = 2 * NA_KH - 1
    flat = rpb.astype(F32).reshape(NA_HEADS * n_dr, 2 * NA_KW - 1)
    flat = jnp.pad(flat, ((0, LANES - NA_HEADS * n_dr), (0, LANES - (2 * NA_KW - 1))))
    t = pl.pallas_call(
        _na_bias_kernel,
        out_shape=jax.ShapeDtypeStruct((LANES, GRID_W * GRID_W), F32),
        name="natten_bias",
    )(flat)
    t = t[:NA_HEADS * n_dr].reshape(NA_HEADS, n_dr, GRID_W, GRID_W)
    masked = jnp.full((NA_HEADS, GRID_W, GRID_W), -jnp.inf, F32)
    kinds = []
    for g in (0, 1, NA_QGROUPS - 1):
        u = _na_union_start(g)
        q_blocks = []
        for i in range(NA_QROWS):
            r = NA_QROWS * g + i
            start = max(0, min(r - NA_KH // 2, NA_ROWS - NA_KH))
            k_blocks = [t[:, u + j - r + NA_KH - 1] if start <= u + j < start + NA_KH else masked
                        for j in range(NA_UNION)]
            q_blocks.append(jnp.stack(k_blocks, axis=2))
        kinds.append(jnp.stack(q_blocks, axis=1))
    return jnp.stack(kinds, axis=1).reshape(NA_HEADS, NA_TYPES, NA_QT, NA_KT)


def _na(p_lat, p_ctx, bias):
    gw = NA_GROUP * NA_HEAD_DIM
    cb = lambda col: col // gw

    def lat(col):
        return pl.BlockSpec((SEQ, gw), lambda b, g: (b, cb(col) + g))

    def ctx(col):
        return pl.BlockSpec((CTX_LEN, gw), lambda b, g: (b, cb(col) + g))

    return pl.pallas_call(
        _na_kernel,
        grid=(BATCH, NA_HEADS // NA_GROUP),
        in_specs=[
            lat(COL_NQ), lat(COL_NK), lat(COL_NV), ctx(COL_NQ), ctx(COL_NK), ctx(COL_NV),
            pl.BlockSpec((NA_GROUP, NA_TYPES, NA_QT, NA_KT), lambda b, g: (g, 0, 0, 0)),
        ],
        out_specs=[
            pl.BlockSpec((SEQ, gw), lambda b, g: (b, g)),
            pl.BlockSpec((CTX_LEN, gw), lambda b, g: (b, g)),
        ],
        out_shape=[
            jax.ShapeDtypeStruct((BATCH * SEQ, BRANCH_W), BF16),
            jax.ShapeDtypeStruct((BATCH * CTX_LEN, BRANCH_W), BF16),
        ],
        compiler_params=_cparams(("arbitrary", "arbitrary")),
        name="natten",
    )(p_lat, p_lat, p_lat, p_ctx, p_ctx, p_ctx, bias)


LRU_HALO = 2 * SUBLANES


def _lru_kernel(xl, xc, cw, cb, wa, ba, wx, bx, lam, hfl, hbl, hfc, hbc,
                ext_f, ext_b, a_f, u_f, a_b, u_b, carry):
    s = pl.program_id(1)
    ch = LRU_CHUNK
    n_lat = SEQ // ch
    halo = LRU_HALO
    zeros_halo = jnp.zeros((halo, BRANCH_W), F32)

    def fill(ext, chunk):
        @pl.when(s == 0)
        def _():
            ext[0:halo, :] = zeros_halo
            ext[halo:halo + ch, :] = xc[...].astype(F32)
            ext[halo + ch:, :] = zeros_halo

        @pl.when(s > 0)
        def _():
            base = pl.multiple_of(chunk * ch, ch)
            ext[halo:halo + ch, :] = xl[pl.ds(base, ch), :].astype(F32)
            prev = xl[pl.ds(pl.multiple_of(jnp.maximum(base - halo, 0), halo), halo), :].astype(F32)
            nxt = xl[pl.ds(pl.multiple_of(jnp.minimum(base + ch, SEQ - halo), halo), halo), :].astype(F32)
            ext[0:halo, :] = jnp.where(chunk > 0, prev, 0.0)
            ext[halo + ch:, :] = jnp.where(chunk < n_lat - 1, nxt, 0.0)

    chunk_f = jnp.maximum(s - 1, 0)
    chunk_b = jnp.maximum(n_lat - s, 0)
    fill(ext_f, chunk_f)
    fill(ext_b, chunk_b)

    def coeffs(ext, d, a_out, u_out):
        xconv = cb[...] + ext[halo - 1:halo - 1 + ch, :] * cw[0:1, :]
        for j in range(1, LRU_CONV):
            xconv = xconv + ext[halo - 1 + j:halo - 1 + j + ch, :] * cw[j:j + 1, :]
        sp = _softplus(-lam[d:d + 1, :])
        for g in range(LRU_BLOCKS):
            sl = slice(g * LRU_BLOCK_DIM, (g + 1) * LRU_BLOCK_DIM)
            xg = xconv[:, sl]
            xb = xg.astype(BF16)
            r = _sigmoid(_dot(xb, wa[d, g]) + ba[d:d + 1, sl])
            i = _sigmoid(_dot(xb, wx[d, g]) + bx[d:d + 1, sl])
            log_a = -LRU_C * r * sp[:, sl]
            a = jnp.exp(log_a)
            a_out[:, sl] = a
            u_out[:, sl] = jnp.sqrt(jnp.tanh(-log_a) * (a * a + 1.0)) * i * xg

    coeffs(ext_f, 0, a_f, u_f)
    coeffs(ext_b, 1, a_b, u_b)

    @pl.when(s == 0)
    def _():
        carry[...] = jnp.zeros_like(carry)

    row = lax.broadcasted_iota(jnp.int32, (SUBLANES, BRANCH_W), 0)
    n_tiles = ch // SUBLANES

    def scan_tiles(hf_out, hb_out):
        def body(t, hs):
            h_f, h_b = hs
            r0 = pl.multiple_of(t * SUBLANES, SUBLANES)
            a = a_f[pl.ds(r0, SUBLANES), :]
            u = u_f[pl.ds(r0, SUBLANES), :]
            for d in (1, 2, 4):
                ok = row >= d
                u = jnp.where(ok, a * pltpu.roll(u, d, 0) + u, u)
                a = jnp.where(ok, a * pltpu.roll(a, d, 0), a)
            h = u + a * h_f
            hf_out[pl.ds(r0, SUBLANES), :] = h
            h_f = h[SUBLANES - 1:SUBLANES, :]
            r1 = pl.multiple_of((n_tiles - 1 - t) * SUBLANES, SUBLANES)
            a = a_b[pl.ds(r1, SUBLANES), :]
            u = u_b[pl.ds(r1, SUBLANES), :]
            for d in (1, 2, 4):
                ok = row < SUBLANES - d
                u = jnp.where(ok, a * pltpu.roll(u, SUBLANES - d, 0) + u, u)
                a = jnp.where(ok, a * pltpu.roll(a, SUBLANES - d, 0), a)
            h = u + a * h_b
            hb_out[pl.ds(r1, SUBLANES), :] = h
            h_b = h[0:1, :]
            return h_f, h_b

        h_f, h_b = lax.fori_loop(0, n_tiles, body, (carry[0:1, :], carry[1:2, :]))
        carry[0:1, :] = h_f
        carry[1:2, :] = h_b

    @pl.when(s == 0)
    def _():
        scan_tiles(hfc, hbc)

    @pl.when(s > 0)
    def _():
        scan_tiles(hfl, hbl)


def _lru(p_lat, p_ctx, conv_w, conv_b, w_a, b_a, w_x, b_x, lam):
    ch = LRU_CHUNK
    n_lat = SEQ // ch
    full = lambda shape: pl.BlockSpec(shape, lambda b, s: (0,) * len(shape))
    wblk = (2, LRU_BLOCKS, LRU_BLOCK_DIM, LRU_BLOCK_DIM)
    return pl.pallas_call(
        _lru_kernel,
        grid=(BATCH, LRU_STEPS),
        in_specs=[
            pl.BlockSpec((SEQ, BRANCH_W), lambda b, s: (b, COL_LX // BRANCH_W)),
            pl.BlockSpec((CTX_LEN, BRANCH_W), lambda b, s: (b, COL_LX // BRANCH_W)),
            full((LRU_CONV, BRANCH_W)), full((1, BRANCH_W)),
            full(wblk), full((2, BRANCH_W)), full(wblk), full((2, BRANCH_W)), full((2, BRANCH_W)),
        ],
        out_specs=[
            pl.BlockSpec((ch, BRANCH_W), lambda b, s: (b * n_lat + jnp.maximum(s - 1, 0), 0)),
            pl.BlockSpec((ch, BRANCH_W), lambda b, s: (b * n_lat + jnp.minimum(n_lat - s, n_lat - 1), 0)),
            pl.BlockSpec((CTX_LEN, BRANCH_W), lambda b, s: (b, 0)),
            pl.BlockSpec((CTX_LEN, BRANCH_W), lambda b, s: (b, 0)),
        ],
        out_shape=[
            jax.ShapeDtypeStruct((BATCH * SEQ, BRANCH_W), F32),
            jax.ShapeDtypeStruct((BATCH * SEQ, BRANCH_W), F32),
            jax.ShapeDtypeStruct((BATCH * CTX_LEN, BRANCH_W), F32),
            jax.ShapeDtypeStruct((BATCH * CTX_LEN, BRANCH_W), F32),
        ],
        scratch_shapes=[
            pltpu.VMEM((ch + 2 * LRU_HALO, BRANCH_W), F32), pltpu.VMEM((ch + 2 * LRU_HALO, BRANCH_W), F32),
            pltpu.VMEM((ch, BRANCH_W), F32), pltpu.VMEM((ch, BRANCH_W), F32),
            pltpu.VMEM((ch, BRANCH_W), F32), pltpu.VMEM((ch, BRANCH_W), F32),
            pltpu.VMEM((SUBLANES, BRANCH_W), F32),
        ],
        compiler_params=_cparams(("arbitrary", "arbitrary")),
        name="rglru",
    )(p_lat, p_ctx, conv_w, conv_b.reshape(1, BRANCH_W), w_a, b_a, w_x, b_x, lam)


def _merge_kernel(ym, yn, hf, hb, lg, g0, g1, g2, wb, o_ref, yl_s):
    j = pl.program_id(1)

    @pl.when(j == 0)
    def _():
        yl_s[...] = ((hf[...] + hb[...]) * jax.nn.gelu(lg[...].astype(F32))).astype(BF16)

    acc = _sigmoid(g0[...].astype(F32)) * _dot(ym[...], wb[0])
    acc = acc + _sigmoid(g1[...].astype(F32)) * _dot(yn[...], wb[1])
    acc = acc + _sigmoid(g2[...].astype(F32)) * _dot(yl_s[...], wb[2])
    o_ref[...] = acc.astype(BF16)


def _merge(ym, yn, hf, hb, p, w_branch):
    rows = ym.shape[0]
    tn = TN_MERGE
    tm = TM_MERGE
    br = lambda: pl.BlockSpec((tm, BRANCH_W), lambda i, j: (i, 0))

    def gate(n):
        return pl.BlockSpec((tm, tn), lambda i, j: (i, (COL_GX + n * D_MODEL) // tn + j))

    return pl.pallas_call(
        _merge_kernel,
        grid=(rows // tm, D_MODEL // tn),
        in_specs=[
            br(), br(), br(), br(),
            pl.BlockSpec((tm, BRANCH_W), lambda i, j: (i, COL_LG // BRANCH_W)),
            gate(0), gate(1), gate(2),
            pl.BlockSpec((N_BRANCH, BRANCH_W, tn), lambda i, j: (0, 0, j)),
        ],
        out_specs=pl.BlockSpec((tm, tn), lambda i, j: (i, j)),
        out_shape=jax.ShapeDtypeStruct((rows, D_MODEL), BF16),
        scratch_shapes=[pltpu.VMEM((tm, BRANCH_W), BF16)],
        compiler_params=_cparams(("arbitrary", "arbitrary")),
        name="merge",
    )(ym, yn, hf, hb, p, p, p, p, w_branch)


def _outproj_kernel(s_ref, w_ref, x_ref, g_ref, o_ref):
    o_ref[...] = x_ref[...] + g_ref[...] * _dot(s_ref[...], w_ref[...])


def _outproj(s, w_out, x, mod_l, row_fn):
    rows = x.shape[0]
    tn = TN_OUT
    tm = TM_OUT
    row_of_tile = row_fn(tm)
    return pl.pallas_call(
        _outproj_kernel,
        grid=(rows // tm, D_MODEL // tn),
        in_specs=[
            pl.BlockSpec((tm, D_MODEL), lambda i, j: (i, 0)),
            pl.BlockSpec((D_MODEL, tn), lambda i, j: (0, j)),
            pl.BlockSpec((tm, tn), lambda i, j: (i, j)),
            pl.BlockSpec((None, None, 1, tn), lambda i, j: (row_of_tile(i), 5, 0, j)),
        ],
        out_specs=pl.BlockSpec((tm, tn), lambda i, j: (i, j)),
        out_shape=jax.ShapeDtypeStruct((rows, D_MODEL), F32),
        compiler_params=_cparams(("arbitrary", "arbitrary")),
        name="outproj",
    )(s, w_out, x, mod_l)


def _split_w_in(w):
    n_gate = 4 * M_HEADS
    o_mg = COL_NQ
    wa = w[:, :o_mg].astype(BF16)
    wb = w[:, o_mg + n_gate:].astype(BF16)
    mg = w[:, o_mg:o_mg + n_gate].reshape(D_MODEL, 2, 2, M_HEADS)
    mg = jnp.transpose(mg, (3, 1, 2, 0)).reshape(M_HEADS, 4, D_MODEL)
    wgt = jnp.pad(mg, ((0, 0), (0, GATE_R - 4), (0, 0))).reshape(M_HEADS * GATE_R, D_MODEL).astype(BF16)
    return wa, wb, wgt


def _gate_bias(b_i, b_f):
    b = jnp.stack([b_i, b_f], axis=1)
    b = jnp.transpose(b, (2, 0, 1)).reshape(M_HEADS, 4).astype(F32)
    return jnp.pad(b, ((0, 0), (0, GATE_R - 4))).reshape(M_HEADS * GATE_R, 1)


def kernel(x, c, ctx, c_ctx, w_mod, b_mod, norm_ffn1, norm_mix, norm_ffn2, ffn1_w_in, ffn1_w_out, ffn2_w_in,
           ffn2_w_out, w_in, mlstm_b_i, mlstm_b_f, mlstm_gn, na_rpb, lru_conv_w, lru_conv_b, lru_w_a, lru_b_a,
           lru_w_x, lru_b_x, lru_lambda, w_branch, w_out, norm_final):
    assert x.shape == (BATCH, SEQ, D_MODEL) and ctx.shape == (BATCH, CTX_LEN, D_MODEL)
    xl = x.reshape(BATCH * SEQ, D_MODEL)
    xc = ctx.reshape(BATCH * CTX_LEN, D_MODEL)
    c_rows = jnp.concatenate([c, c_ctx[None, :], jnp.zeros((MOD_ROWS - BATCH - 1, D_MODEL), F32)], axis=0)
    mod = _modulation(c_rows, w_mod, b_mod).reshape(DEPTH, MOD_ROWS, N_MOD, 1, D_MODEL)
    cos, sin = _rope_tables()

    for li in range(DEPTH):
        last = li == DEPTH - 1
        mod_l = mod[li]
        w1_in, w1_out = ffn1_w_in[li].astype(BF16), ffn1_w_out[li].astype(BF16)
        w2_in, w2_out = ffn2_w_in[li].astype(BF16), ffn2_w_out[li].astype(BF16)
        w_head, w_tail, wgt = _split_w_in(w_in[li])
        bgt = _gate_bias(mlstm_b_i[li], mlstm_b_f[li])
        wb = w_branch[li].astype(BF16)
        wo = w_out[li].astype(BF16)
        bias = _na_bias(na_rpb[li])
        wa, wx = lru_w_a[li].astype(BF16), lru_w_x[li].astype(BF16)

        xl = _ffn(xl, mod_l, 0, norm_ffn1[li], w1_in, w1_out, _lat_row)
        xc = _ffn(xc, mod_l, 0, norm_ffn1[li], w1_in, w1_out, _ctx_row)

        p_lat, gt_lat = _inproj(xl, mod_l, norm_mix[li], w_head, w_tail, wgt, bgt, _lat_row)
        p_ctx, gt_ctx = _inproj(xc, mod_l, norm_mix[li], w_head, w_tail, wgt, bgt, _ctx_row)

        ym_l, ym_c = _mlstm(p_lat, p_ctx, gt_lat, gt_ctx, cos, sin, mlstm_gn[li])
        yn_l, yn_c = _na(p_lat, p_ctx, bias)
        hf_l, hb_l, hf_c, hb_c = _lru(p_lat, p_ctx, lru_conv_w[li], lru_conv_b[li], wa, lru_b_a[li], wx,
                                      lru_b_x[li], lru_lambda[li])

        s_l = _merge(ym_l, yn_l, hf_l, hb_l, p_lat, wb)
        xl = _outproj(s_l, wo, xl, mod_l, _lat_row)
        xl = _ffn(xl, mod_l, 6, norm_ffn2[li], w2_in, w2_out, _lat_row, norm_final if last else None)
        if not last:
            s_c = _merge(ym_c, yn_c, hf_c, hb_c, p_ctx, wb)
            xc = _outproj(s_c, wo, xc, mod_l, _ctx_row)
            xc = _ffn(xc, mod_l, 6, norm_ffn2[li], w2_in, w2_out, _ctx_row)
    return xl.reshape(BATCH, SEQ, D_MODEL)
```

```python
import functools

import jax
import jax.numpy as jnp
from jax import lax
from jax.experimental import pallas as pl
from jax.experimental.pallas import tpu as pltpu

F32 = jnp.float32
BF16 = jnp.bfloat16

D_MODEL = 2048
BATCH = 4
SEQ = 2048
DEPTH = 2
GRID_W = 64
CTX_LEN = 256
EPS = 1e-6
D_FF = (11 * D_MODEL) // 4
N_MOD = 9
BRANCH_W = D_MODEL // 2
N_BRANCH = 3
M_HEADS = 4
M_HEAD_DIM = BRANCH_W // M_HEADS
ROPE_BASE = 10000.0
NA_HEADS = 8
NA_HEAD_DIM = BRANCH_W // NA_HEADS
NA_KH = 8
NA_KW = 16
LRU_BLOCKS = 8
LRU_BLOCK_DIM = BRANCH_W // LRU_BLOCKS
LRU_CONV = 4
LRU_C = 8.0

SUBLANES = 8
LANES = 128

MOD_ROWS = 8
CTX_MOD_ROW = BATCH
TM_FFN = 512
TM_IN = 1024
TM_MERGE = 1024
TM_OUT = 1024
TF = 512
FFN_SUB = 256
TN_IN = 1024
TN_MERGE = 512
TN_OUT = 512
TN_MOD = 1024
M_CHUNK = 256
N_CHUNKS = (SEQ + CTX_LEN) // M_CHUNK
LRU_CHUNK = 256
LRU_STEPS = (SEQ + CTX_LEN) // LRU_CHUNK
GATE_R = SUBLANES

COL_MQ, COL_MK, COL_MV, COL_MO = 0, BRANCH_W, 2 * BRANCH_W, 3 * BRANCH_W
COL_NQ, COL_NK, COL_NV = 4 * BRANCH_W, 5 * BRANCH_W, 6 * BRANCH_W
COL_LX, COL_LG = 7 * BRANCH_W, 8 * BRANCH_W
COL_GX = 9 * BRANCH_W
P_COLS = COL_GX + N_BRANCH * D_MODEL

VMEM_BYTES = 64 * 1024 * 1024
VMEM_LIMIT = VMEM_BYTES - 8 * 1024 * 1024


def _cparams(sem):
    return pltpu.CompilerParams(dimension_semantics=sem, vmem_limit_bytes=VMEM_LIMIT)


def _sigmoid(x):
    return jax.nn.sigmoid(x)


def _log_sigmoid(x):
    return jnp.minimum(x, 0.0) - jnp.log1p(jnp.exp(-jnp.abs(x)))


def _softplus(x):
    return jnp.maximum(x, 0.0) + jnp.log1p(jnp.exp(-jnp.abs(x)))


def _dot(a, b):
    return jnp.dot(a, b, preferred_element_type=F32)


def _dot_nt(a, b):
    return lax.dot_general(a, b, (((1,), (1,)), ((), ())), preferred_element_type=F32)


def _dot_tn(a, b):
    return lax.dot_general(a, b, (((0,), (0,)), ((), ())), preferred_element_type=F32)


def _dot_exact(a, b):
    return jnp.dot(a, b, preferred_element_type=F32, precision=lax.Precision.HIGHEST)


def _dot_split(x, m):
    hi = x.astype(BF16)
    r1 = x - hi.astype(F32)
    mid = r1.astype(BF16)
    lo = (r1 - mid.astype(F32)).astype(BF16)
    return _dot(hi, m) + _dot(mid, m) + _dot(lo, m)


def _adaln(x, gain, shift, scale):
    y = x * lax.rsqrt(jnp.mean(x * x, axis=-1, keepdims=True) + EPS)
    return (y * gain) * (1.0 + scale) + shift


def _mod_kernel(c_ref, w_ref, b_ref, o_ref):
    c = c_ref[...]
    act = (c * _sigmoid(c)).astype(BF16)
    o_ref[...] = _dot(act, w_ref[...].astype(BF16)) + b_ref[...]


def _modulation(c_rows, w_mod, b_mod):
    n_cols = N_MOD * D_MODEL
    return pl.pallas_call(
        _mod_kernel,
        grid=(DEPTH, n_cols // TN_MOD),
        in_specs=[
            pl.BlockSpec((MOD_ROWS, D_MODEL), lambda l, j: (0, 0)),
            pl.BlockSpec((None, D_MODEL, TN_MOD), lambda l, j: (l, 0, j)),
            pl.BlockSpec((None, 1, TN_MOD), lambda l, j: (l, 0, j)),
        ],
        out_specs=pl.BlockSpec((None, MOD_ROWS, TN_MOD), lambda l, j: (l, 0, j)),
        out_shape=jax.ShapeDtypeStruct((DEPTH, MOD_ROWS, n_cols), F32),
        compiler_params=_cparams(("arbitrary", "arbitrary")),
        name="modulation",
    )(c_rows, w_mod, b_mod.reshape(DEPTH, 1, n_cols))


def _mod_spec(li, k, row_of_tile):
    return pl.BlockSpec((None, None, None, 1, D_MODEL), lambda i, j: (li, row_of_tile(i), k, 0, 0))


def _lat_row(tm):
    return lambda i: i // (SEQ // tm)


def _ctx_row(tm):
    return lambda i: CTX_MOD_ROW


def _ffn_kernel(*refs, final):
    if final:
        x_ref, sh_ref, sc_ref, g_ref, ng_ref, wg_ref, wu_ref, wo_ref, fg_ref, o_ref, h_scr = refs
    else:
        x_ref, sh_ref, sc_ref, g_ref, ng_ref, wg_ref, wu_ref, wo_ref, o_ref, h_scr = refs
    f = pl.program_id(1)

    @pl.when(f == 0)
    def _():
        h = _adaln(x_ref[...], ng_ref[...], sh_ref[...], sc_ref[...])
        h_scr[...] = h.astype(BF16)
        o_ref[...] = jnp.zeros_like(o_ref)

    for r in range(TM_FFN // FFN_SUB):
        rows = slice(r * FFN_SUB, (r + 1) * FFN_SUB)
        h = h_scr[rows, :]
        gte = _dot(h, wg_ref[...])
        up = _dot(h, wu_ref[...])
        act = (gte * _sigmoid(gte) * up).astype(BF16)
        o_ref[rows, :] += _dot(act, wo_ref[...])

    @pl.when(f == pl.num_programs(1) - 1)
    def _():
        out = x_ref[...] + 0.5 * g_ref[...] * o_ref[...]
        if final:
            out = out * lax.rsqrt(jnp.mean(out * out, axis=-1, keepdims=True) + EPS) * fg_ref[...]
        o_ref[...] = out


def _ffn(x, li, mod, k0, norm_g, w_in, w_out, row_fn, final_gain=None):
    rows = x.shape[0]
    nf = D_FF // TF
    tm = TM_FFN
    row_of_tile = row_fn(tm)
    final = final_gain is not None
    in_specs = [
        pl.BlockSpec((tm, D_MODEL), lambda i, f: (i, 0)),
        _mod_spec(li, k0, row_of_tile), _mod_spec(li, k0 + 1, row_of_tile), _mod_spec(li, k0 + 2, row_of_tile),
        pl.BlockSpec((1, D_MODEL), lambda i, f: (0, 0)),
        pl.BlockSpec((None, D_MODEL, TF), lambda i, f: (li, 0, f)),
        pl.BlockSpec((None, D_MODEL, TF), lambda i, f: (li, 0, f + nf)),
        pl.BlockSpec((None, TF, D_MODEL), lambda i, f: (li, f, 0)),
    ]
    args = [x, mod, mod, mod, norm_g.reshape(1, D_MODEL), w_in, w_in, w_out]
    if final:
        in_specs.append(pl.BlockSpec((1, D_MODEL), lambda i, f: (0, 0)))
        args.append(final_gain.reshape(1, D_MODEL))
    return pl.pallas_call(
        functools.partial(_ffn_kernel, final=final),
        grid=(rows // tm, nf),
        in_specs=in_specs,
        out_specs=pl.BlockSpec((tm, D_MODEL), lambda i, f: (i, 0)),
        out_shape=jax.ShapeDtypeStruct((rows, D_MODEL), F32),
        scratch_shapes=[pltpu.VMEM((tm, D_MODEL), BF16)],
        compiler_params=_cparams(("arbitrary", "arbitrary")),
        name="ffn_final" if final else "ffn",
    )(*args)


N_HEAD_TILES = COL_NQ // TN_IN


def _inproj_kernel(x_ref, sh_ref, sc_ref, ng_ref, wa_ref, wb_ref, wgt_ref, bgt_ref, p_ref, gt_ref, h_scr):
    j = pl.program_id(1)

    @pl.when(j == 0)
    def _():
        h = _adaln(x_ref[...], ng_ref[...], sh_ref[...], sc_ref[...]).astype(BF16)
        h_scr[...] = h
        for s in range(TM_IN // M_CHUNK):
            gt_ref[s] = _dot_nt(wgt_ref[...], h[s * M_CHUNK:(s + 1) * M_CHUNK, :]) + bgt_ref[...]

    @pl.when(j < N_HEAD_TILES)
    def _():
        p_ref[...] = _dot(h_scr[...], wa_ref[...]).astype(p_ref.dtype)

    @pl.when(j >= N_HEAD_TILES)
    def _():
        p_ref[...] = _dot(h_scr[...], wb_ref[...]).astype(p_ref.dtype)


def _inproj(x, li, mod, norm_g, wa, wb, wgt, bgt, row_fn):
    rows = x.shape[0]
    tm = TM_IN
    row_of_tile = row_fn(tm)
    return pl.pallas_call(
        _inproj_kernel,
        grid=(rows // tm, P_COLS // TN_IN),
        in_specs=[
            pl.BlockSpec((tm, D_MODEL), lambda i, j: (i, 0)),
            _mod_spec(li, 3, row_of_tile), _mod_spec(li, 4, row_of_tile),
            pl.BlockSpec((1, D_MODEL), lambda i, j: (0, 0)),
            pl.BlockSpec((None, D_MODEL, TN_IN), lambda i, j: (li, 0, jnp.minimum(j, N_HEAD_TILES - 1))),
            pl.BlockSpec((None, D_MODEL, TN_IN), lambda i, j: (li, 0, jnp.maximum(j - N_HEAD_TILES, 0))),
            pl.BlockSpec((M_HEADS * GATE_R, D_MODEL), lambda i, j: (0, 0)),
            pl.BlockSpec((M_HEADS * GATE_R, 1), lambda i, j: (0, 0)),
        ],
        out_specs=[
            pl.BlockSpec((tm, TN_IN), lambda i, j: (i, j)),
            pl.BlockSpec((tm // M_CHUNK, M_HEADS * GATE_R, M_CHUNK), lambda i, j: (i, 0, 0)),
        ],
        out_shape=[
            jax.ShapeDtypeStruct((rows, P_COLS), BF16),
            jax.ShapeDtypeStruct((rows // M_CHUNK, M_HEADS * GATE_R, M_CHUNK), F32),
        ],
        scratch_shapes=[pltpu.VMEM((tm, D_MODEL), BF16)],
        compiler_params=_cparams(("arbitrary", "arbitrary")),
        name="inproj",
    )(x, mod, mod, norm_g.reshape(1, D_MODEL), wa, wb, wgt, bgt)


def _rope(u, cs, sn):
    halves = []
    for j in range(M_HEAD_DIM // LANES):
        sl = slice(j * LANES, (j + 1) * LANES)
        uj = u[:, sl]
        halves.append(uj * cs[:, sl] + pltpu.roll(uj, LANES // 2, 1) * sn[:, sl])
    return jnp.concatenate(halves, axis=1)


def _mlstm_chunk(q, k, v, i_col, cum_col, i_row, cum_row, mask, last, c_mem, n_mem, m_prev):
    logw = jnp.where(mask, cum_col - cum_row + i_row, -jnp.inf)
    m_row = jnp.maximum(cum_col + m_prev, jnp.max(logw, axis=-1, keepdims=True))
    m_new = m_row[last:last + 1, :]
    cum_last = cum_col[last:last + 1, :]
    w_state = jnp.exp(cum_last - cum_col + i_col - m_new)
    decay = jnp.exp(cum_last + m_prev - m_new)
    inter = jnp.exp(cum_col + m_prev - m_row)
    kf = k.astype(F32)
    sc = _dot_nt(q, k) * jnp.exp(logw - m_row)
    num = inter * _dot_nt(q, c_mem.astype(BF16)) + _dot(sc.astype(BF16), v)
    den = inter * jnp.sum(q.astype(F32) * n_mem, axis=-1, keepdims=True) + jnp.sum(sc, axis=-1, keepdims=True)
    hid = num / jnp.maximum(jnp.abs(den), jnp.exp(-m_row))
    c_new = decay * c_mem + _dot_tn((w_state * v.astype(F32)).astype(BF16), k)
    n_new = decay * n_mem + jnp.sum(w_state * kf, axis=0, keepdims=True)
    return hid, c_new, n_new, m_new


def _mlstm_kernel(ql, kl, vl, ol, qc, kc, vc, oc, gtl, gtc, cos, sin, gn, yl, yc,
                  q_scr, k_scr, hf_scr, hb_scr, grow_scr, gcol_scr):
    ch = M_CHUNK
    kscale = M_HEAD_DIM ** -0.5

    def rows(c):
        return slice(c * ch, (c + 1) * ch)

    t_idx = lax.broadcasted_iota(jnp.int32, (ch, ch), 0)
    s_idx = lax.broadcasted_iota(jnp.int32, (ch, ch), 1)
    lower = s_idx <= t_idx
    upper = s_idx >= t_idx
    lower_b = lower.astype(BF16)
    upper_b = upper.astype(BF16)
    gate_row = lax.broadcasted_iota(jnp.int32, (GATE_R, ch), 0)
    lane_pad = jnp.zeros((LANES - GATE_R, ch), F32)

    for c in range(N_CHUNKS):
        if c == 0:
            q_scr[rows(0), :] = qc[...].astype(BF16)
            k_scr[rows(0), :] = (kc[...].astype(F32) * kscale).astype(BF16)
        else:
            lat = rows(c - 1)
            q_scr[rows(c), :] = _rope(ql[lat, :].astype(F32), cos[lat, :], sin[lat, :]).astype(BF16)
            k_scr[rows(c), :] = (_rope(kl[lat, :].astype(F32), cos[lat, :], sin[lat, :]) * kscale).astype(BF16)
        gt = gtc[0] if c == 0 else gtl[c - 1]
        lf = _log_sigmoid(gt)
        terms = jnp.where(gate_row == 1, _dot_split(lf, upper_b), jnp.where(gate_row == 3, _dot_split(lf, lower_b), gt))
        grow_scr[c] = terms
        gcol_scr[rows(c), :] = jnp.concatenate([terms, lane_pad], axis=0).T

    def values(c):
        return vc[...].astype(BF16) if c == 0 else vl[rows(c - 1), :].astype(BF16)

    def init():
        return (jnp.zeros((M_HEAD_DIM, M_HEAD_DIM), F32), jnp.zeros((1, M_HEAD_DIM), F32),
                jnp.full((1, 1), -jnp.inf, F32))

    def run(c, d, mask, last, state):
        g_rows = grow_scr[c]
        g_cols = gcol_scr[rows(c), :]
        i0, f0 = 2 * d, 2 * d + 1
        return _mlstm_chunk(q_scr[rows(c), :], k_scr[rows(c), :], values(c), g_cols[:, i0:i0 + 1],
                            g_cols[:, f0:f0 + 1], g_rows[i0:i0 + 1, :], g_rows[f0:f0 + 1, :], mask, last, *state)

    st_f, st_b = init(), init()
    for step in range(N_CHUNKS):
        cf = step
        cb = 0 if step == 0 else N_CHUNKS - step
        hid, *st_f = run(cf, 0, lower, ch - 1, st_f)
        hf_scr[rows(cf), :] = hid
        hid, *st_b = run(cb, 1, upper, 0, st_b)
        hb_scr[rows(cb), :] = hid

    gain = gn[...]
    for c in range(N_CHUNKS):
        hsum = hf_scr[rows(c), :] + hb_scr[rows(c), :]
        hn = hsum * lax.rsqrt(jnp.mean(hsum * hsum, axis=-1, keepdims=True) + EPS) * gain
        if c == 0:
            yc[...] = (_sigmoid(oc[...].astype(F32)) * hn).astype(yc.dtype)
        else:
            yl[rows(c - 1), :] = (_sigmoid(ol[rows(c - 1), :].astype(F32)) * hn).astype(yl.dtype)


def _mlstm(p_lat, p_ctx, gt_lat, gt_ctx, cos, sin, gn):
    hd = M_HEAD_DIM
    cb = lambda col: col // hd

    def lat(col):
        return pl.BlockSpec((SEQ, hd), lambda b, h: (b, cb(col) + h))

    def ctx(col):
        return pl.BlockSpec((CTX_LEN, hd), lambda b, h: (b, cb(col) + h))

    seq_chunks = SEQ // M_CHUNK
    ctx_chunks = CTX_LEN // M_CHUNK
    return pl.pallas_call(
        _mlstm_kernel,
        grid=(BATCH, M_HEADS),
        in_specs=[
            lat(COL_MQ), lat(COL_MK), lat(COL_MV), lat(COL_MO),
            ctx(COL_MQ), ctx(COL_MK), ctx(COL_MV), ctx(COL_MO),
            pl.BlockSpec((seq_chunks, GATE_R, M_CHUNK), lambda b, h: (b, h, 0)),
            pl.BlockSpec((ctx_chunks, GATE_R, M_CHUNK), lambda b, h: (b, h, 0)),
            pl.BlockSpec((SEQ, hd), lambda b, h: (0, 0)),
            pl.BlockSpec((SEQ, hd), lambda b, h: (0, 0)),
            pl.BlockSpec((1, hd), lambda b, h: (0, h)),
        ],
        out_specs=[
            pl.BlockSpec((SEQ, hd), lambda b, h: (b, h)),
            pl.BlockSpec((CTX_LEN, hd), lambda b, h: (b, h)),
        ],
        out_shape=[
            jax.ShapeDtypeStruct((BATCH * SEQ, BRANCH_W), BF16),
            jax.ShapeDtypeStruct((BATCH * CTX_LEN, BRANCH_W), BF16),
        ],
        scratch_shapes=[
            pltpu.VMEM((SEQ + CTX_LEN, hd), BF16), pltpu.VMEM((SEQ + CTX_LEN, hd), BF16),
            pltpu.VMEM((SEQ + CTX_LEN, hd), F32), pltpu.VMEM((SEQ + CTX_LEN, hd), F32),
            pltpu.VMEM((N_CHUNKS, GATE_R, M_CHUNK), F32), pltpu.VMEM((SEQ + CTX_LEN, LANES), F32),
        ],
        compiler_params=_cparams(("arbitrary", "arbitrary")),
        name="mlstm",
    )(p_lat, p_lat, p_lat, p_lat, p_ctx, p_ctx, p_ctx, p_ctx, gt_lat, gt_ctx, cos, sin, gn.reshape(1, BRANCH_W))


def _rope_tables():
    pos = jnp.arange(SEQ)
    nf = M_HEAD_DIM // 4
    inv = ROPE_BASE ** (-jnp.arange(nf, dtype=F32) / nf)
    ang_r = (pos // GRID_W).astype(F32)[:, None] * inv[None, :]
    ang_c = (pos % GRID_W).astype(F32)[:, None] * inv[None, :]
    cr, sr, cc, sc = jnp.cos(ang_r), jnp.sin(ang_r), jnp.cos(ang_c), jnp.sin(ang_c)
    return (jnp.concatenate([cr, cr, cc, cc], axis=-1), jnp.concatenate([-sr, sr, -sc, sc], axis=-1))


NA_GROUP = 2
NA_ROWS = SEQ // GRID_W
NA_QROWS = 4
NA_UNION = NA_KH + NA_QROWS
NA_QT = NA_QROWS * GRID_W
NA_KT = NA_UNION * GRID_W
NA_QGROUPS = NA_ROWS // NA_QROWS
NA_TYPES = 3


def _na_union_start(g):
    return max(0, min(NA_QROWS * g - NA_KH // 2, NA_ROWS - NA_UNION))


def _na_kernel(ql, kl, vl, qc, kc, vc, bias, yl, yc):
    scale = NA_HEAD_DIM ** -0.5
    kcs = [kc[:, hh * NA_HEAD_DIM:(hh + 1) * NA_HEAD_DIM].astype(BF16) for hh in range(NA_GROUP)]
    vcs = [vc[:, hh * NA_HEAD_DIM:(hh + 1) * NA_HEAD_DIM].astype(BF16) for hh in range(NA_GROUP)]

    def group_body(g, carry):
        u = jnp.clip(NA_QROWS * g - NA_KH // 2, 0, NA_ROWS - NA_UNION)
        kind = jnp.where(g == 0, 0, jnp.where(g == NA_QGROUPS - 1, NA_TYPES - 1, 1))
        q_rows = pl.ds(pl.multiple_of(g * NA_QT, NA_QT), NA_QT)
        k_rows = pl.ds(pl.multiple_of(u * GRID_W, GRID_W), NA_KT)
        outs = []
        for hh in range(NA_GROUP):
            sl = slice(hh * NA_HEAD_DIM, (hh + 1) * NA_HEAD_DIM)
            q = ql[q_rows, sl].astype(BF16)
            kw = kl[k_rows, sl].astype(BF16)
            vw = vl[k_rows, sl].astype(BF16)
            s_win = _dot_nt(q, kw) * scale + bias[hh, kind]
            s_ctx = _dot_nt(q, kcs[hh]) * scale
            m = jnp.maximum(jnp.max(s_win, axis=-1, keepdims=True), jnp.max(s_ctx, axis=-1, keepdims=True))
            p_win = jnp.exp(s_win - m)
            p_ctx = jnp.exp(s_ctx - m)
            denom = jnp.sum(p_win, axis=-1, keepdims=True) + jnp.sum(p_ctx, axis=-1, keepdims=True)
            o = _dot(p_win.astype(BF16), vw) + _dot(p_ctx.astype(BF16), vcs[hh])
            outs.append(o / denom)
        yl[q_rows, :] = jnp.concatenate(outs, axis=1).astype(yl.dtype)
        return carry

    lax.fori_loop(0, NA_QGROUPS, group_body, 0)

    outs = []
    for hh in range(NA_GROUP):
        sl = slice(hh * NA_HEAD_DIM, (hh + 1) * NA_HEAD_DIM)
        s = _dot_nt(qc[:, sl].astype(BF16), kcs[hh]) * scale
        p = jnp.exp(s - jnp.max(s, axis=-1, keepdims=True))
        outs.append(_dot(p.astype(BF16), vcs[hh]) / jnp.sum(p, axis=-1, keepdims=True))
    yc[...] = jnp.concatenate(outs, axis=1).astype(yc.dtype)


def _na_bias_kernel(rpb_ref, o_ref):
    shape = (LANES, GRID_W * GRID_W)
    n = lax.broadcasted_iota(jnp.int32, shape, 1)
    c = lax.broadcasted_iota(jnp.int32, shape, 0)
    q = jnp.right_shift(n, GRID_W.bit_length() - 1)
    k = jnp.bitwise_and(n, GRID_W - 1)
    dc = jnp.clip(k - q, -(NA_KW - 1), NA_KW - 1) + (NA_KW - 1)
    picked = _dot_exact(rpb_ref[...], (c == dc).astype(F32))
    col_start = jnp.clip(q - NA_KW // 2, 0, GRID_W - NA_KW)
    col_ok = (k >= col_start) & (k < col_start + NA_KW)
    o_ref[...] = jnp.where(col_ok, picked, -jnp.inf)


def _na_bias(rpb):
    assert _na_union_start(1) == 0 and _na_union_start(2) == NA_QROWS
    assert GRID_W & (GRID_W - 1) == 0 and NA_HEADS * (2 * NA_KH - 1) <= LANES and 2 * NA_KW - 1 <= LANES
    n_dr = 2 * NA_KH - 1
    flat = rpb.astype(F32).reshape(NA_HEADS * n_dr, 2 * NA_KW - 1)
    flat = jnp.pad(flat, ((0, LANES - NA_HEADS * n_dr), (0, LANES - (2 * NA_KW - 1))))
    t = pl.pallas_call(
        _na_bias_kernel,
        out_shape=jax.ShapeDtypeStruct((LANES, GRID_W * GRID_W), F32),
        name="natten_bias",
    )(flat)
    t = t[:NA_HEADS * n_dr].reshape(NA_HEADS, n_dr, GRID_W, GRID_W)
    tp = jnp.pad(t, ((0, 0), (NA_UNION, NA_UNION), (0, 0), (0, 0)), constant_values=-jnp.inf)
    runs = []
    for g in (0, 1, NA_QGROUPS - 1):
        u = _na_union_start(g)
        for i in range(NA_QROWS):
            r = NA_QROWS * g + i
            start = max(0, min(r - NA_KH // 2, NA_ROWS - NA_KH))
            dr0 = u - r + NA_KH - 1 + NA_UNION
            in_window = jnp.array([start <= u + j < start + NA_KH for j in range(NA_UNION)])
            runs.append(jnp.where(in_window[None, :, None, None], tp[:, dr0:dr0 + NA_UNION], -jnp.inf))
    table = jnp.stack(runs, axis=1).reshape(NA_HEADS, NA_TYPES, NA_QROWS, NA_UNION, GRID_W, GRID_W)
    return jnp.transpose(table, (0, 1, 2, 4, 3, 5)).reshape(NA_HEADS, NA_TYPES, NA_QT, NA_KT)


def _na(p_lat, p_ctx, bias):
    gw = NA_GROUP * NA_HEAD_DIM
    cb = lambda col: col // gw

    def lat(col):
        return pl.BlockSpec((SEQ, gw), lambda b, g: (b, cb(col) + g))

    def ctx(col):
        return pl.BlockSpec((CTX_LEN, gw), lambda b, g: (b, cb(col) + g))

    return pl.pallas_call(
        _na_kernel,
        grid=(BATCH, NA_HEADS // NA_GROUP),
        in_specs=[
            lat(COL_NQ), lat(COL_NK), lat(COL_NV), ctx(COL_NQ), ctx(COL_NK), ctx(COL_NV),
            pl.BlockSpec((NA_GROUP, NA_TYPES, NA_QT, NA_KT), lambda b, g: (g, 0, 0, 0)),
        ],
        out_specs=[
            pl.BlockSpec((SEQ, gw), lambda b, g: (b, g)),
            pl.BlockSpec((CTX_LEN, gw), lambda b, g: (b, g)),
        ],
        out_shape=[
            jax.ShapeDtypeStruct((BATCH * SEQ, BRANCH_W), BF16),
            jax.ShapeDtypeStruct((BATCH * CTX_LEN, BRANCH_W), BF16),
        ],
        compiler_params=_cparams(("arbitrary", "arbitrary")),
        name="natten",
    )(p_lat, p_lat, p_lat, p_ctx, p_ctx, p_ctx, bias)


LRU_HALO = 2 * SUBLANES


def _lru_kernel(xl, xc, cw, cb, wa, ba, wx, bx, lam, hfl, hbl, hfc, hbc,
                ext_f, ext_b, a_f, u_f, a_b, u_b, carry):
    s = pl.program_id(1)
    ch = LRU_CHUNK
    n_lat = SEQ // ch
    halo = LRU_HALO
    zeros_halo = jnp.zeros((halo, BRANCH_W), F32)

    def fill(ext, chunk):
        @pl.when(s == 0)
        def _():
            ext[0:halo, :] = zeros_halo
            ext[halo:halo + ch, :] = xc[...].astype(F32)
            ext[halo + ch:, :] = zeros_halo

        @pl.when(s > 0)
        def _():
            base = pl.multiple_of(chunk * ch, ch)
            ext[halo:halo + ch, :] = xl[pl.ds(base, ch), :].astype(F32)
            prev = xl[pl.ds(pl.multiple_of(jnp.maximum(base - halo, 0), halo), halo), :].astype(F32)
            nxt = xl[pl.ds(pl.multiple_of(jnp.minimum(base + ch, SEQ - halo), halo), halo), :].astype(F32)
            ext[0:halo, :] = jnp.where(chunk > 0, prev, 0.0)
            ext[halo + ch:, :] = jnp.where(chunk < n_lat - 1, nxt, 0.0)

    chunk_f = jnp.maximum(s - 1, 0)
    chunk_b = jnp.maximum(n_lat - s, 0)
    fill(ext_f, chunk_f)
    fill(ext_b, chunk_b)

    def coeffs(ext, d, a_out, u_out):
        xconv = cb[...] + ext[halo - 1:halo - 1 + ch, :] * cw[0:1, :]
        for j in range(1, LRU_CONV):
            xconv = xconv + ext[halo - 1 + j:halo - 1 + j + ch, :] * cw[j:j + 1, :]
        sp = _softplus(-lam[d:d + 1, :])
        for g in range(LRU_BLOCKS):
            sl = slice(g * LRU_BLOCK_DIM, (g + 1) * LRU_BLOCK_DIM)
            xg = xconv[:, sl]
            xb = xg.astype(BF16)
            r = _sigmoid(_dot(xb, wa[d, g]) + ba[d:d + 1, sl])
            i = _sigmoid(_dot(xb, wx[d, g]) + bx[d:d + 1, sl])
            log_a = -LRU_C * r * sp[:, sl]
            a = jnp.exp(log_a)
            a_out[:, sl] = a
            u_out[:, sl] = jnp.sqrt(jnp.tanh(-log_a) * (a * a + 1.0)) * i * xg

    coeffs(ext_f, 0, a_f, u_f)
    coeffs(ext_b, 1, a_b, u_b)

    @pl.when(s == 0)
    def _():
        carry[...] = jnp.zeros_like(carry)

    row = lax.broadcasted_iota(jnp.int32, (SUBLANES, BRANCH_W), 0)
    n_tiles = ch // SUBLANES

    def scan_tiles(hf_out, hb_out):
        def body(t, hs):
            h_f, h_b = hs
            r0 = pl.multiple_of(t * SUBLANES, SUBLANES)
            a = a_f[pl.ds(r0, SUBLANES), :]
            u = u_f[pl.ds(r0, SUBLANES), :]
            for d in (1, 2, 4):
                ok = row >= d
                u = jnp.where(ok, a * pltpu.roll(u, d, 0) + u, u)
                a = jnp.where(ok, a * pltpu.roll(a, d, 0), a)
            h = u + a * h_f
            hf_out[pl.ds(r0, SUBLANES), :] = h
            h_f = h[SUBLANES - 1:SUBLANES, :]
            r1 = pl.multiple_of((n_tiles - 1 - t) * SUBLANES, SUBLANES)
            a = a_b[pl.ds(r1, SUBLANES), :]
            u = u_b[pl.ds(r1, SUBLANES), :]
            for d in (1, 2, 4):
                ok = row < SUBLANES - d
                u = jnp.where(ok, a * pltpu.roll(u, SUBLANES - d, 0) + u, u)
                a = jnp.where(ok, a * pltpu.roll(a, SUBLANES - d, 0), a)
            h = u + a * h_b
            hb_out[pl.ds(r1, SUBLANES), :] = h
            h_b = h[0:1, :]
            return h_f, h_b

        h_f, h_b = lax.fori_loop(0, n_tiles, body, (carry[0:1, :], carry[1:2, :]))
        carry[0:1, :] = h_f
        carry[1:2, :] = h_b

    @pl.when(s == 0)
    def _():
        scan_tiles(hfc, hbc)

    @pl.when(s > 0)
    def _():
        scan_tiles(hfl, hbl)


def _lru(p_lat, p_ctx, conv_w, conv_b, w_a, b_a, w_x, b_x, lam):
    ch = LRU_CHUNK
    n_lat = SEQ // ch
    full = lambda shape: pl.BlockSpec(shape, lambda b, s: (0,) * len(shape))
    wblk = (2, LRU_BLOCKS, LRU_BLOCK_DIM, LRU_BLOCK_DIM)
    return pl.pallas_call(
        _lru_kernel,
        grid=(BATCH, LRU_STEPS),
        in_specs=[
            pl.BlockSpec((SEQ, BRANCH_W), lambda b, s: (b, COL_LX // BRANCH_W)),
            pl.BlockSpec((CTX_LEN, BRANCH_W), lambda b, s: (b, COL_LX // BRANCH_W)),
            full((LRU_CONV, BRANCH_W)), full((1, BRANCH_W)),
            full(wblk), full((2, BRANCH_W)), full(wblk), full((2, BRANCH_W)), full((2, BRANCH_W)),
        ],
        out_specs=[
            pl.BlockSpec((ch, BRANCH_W), lambda b, s: (b * n_lat + jnp.maximum(s - 1, 0), 0)),
            pl.BlockSpec((ch, BRANCH_W), lambda b, s: (b * n_lat + jnp.minimum(n_lat - s, n_lat - 1), 0)),
            pl.BlockSpec((CTX_LEN, BRANCH_W), lambda b, s: (b, 0)),
            pl.BlockSpec((CTX_LEN, BRANCH_W), lambda b, s: (b, 0)),
        ],
        out_shape=[
            jax.ShapeDtypeStruct((BATCH * SEQ, BRANCH_W), F32),
            jax.ShapeDtypeStruct((BATCH * SEQ, BRANCH_W), F32),
            jax.ShapeDtypeStruct((BATCH * CTX_LEN, BRANCH_W), F32),
            jax.ShapeDtypeStruct((BATCH * CTX_LEN, BRANCH_W), F32),
        ],
        scratch_shapes=[
            pltpu.VMEM((ch + 2 * LRU_HALO, BRANCH_W), F32), pltpu.VMEM((ch + 2 * LRU_HALO, BRANCH_W), F32),
            pltpu.VMEM((ch, BRANCH_W), F32), pltpu.VMEM((ch, BRANCH_W), F32),
            pltpu.VMEM((ch, BRANCH_W), F32), pltpu.VMEM((ch, BRANCH_W), F32),
            pltpu.VMEM((SUBLANES, BRANCH_W), F32),
        ],
        compiler_params=_cparams(("arbitrary", "arbitrary")),
        name="rglru",
    )(p_lat, p_ctx, conv_w, conv_b.reshape(1, BRANCH_W), w_a, b_a, w_x, b_x, lam)


def _merge_kernel(ym, yn, hf, hb, lg, g0, g1, g2, wb, o_ref, yl_s):
    j = pl.program_id(1)

    @pl.when(j == 0)
    def _():
        yl_s[...] = ((hf[...] + hb[...]) * jax.nn.gelu(lg[...].astype(F32))).astype(BF16)

    acc = _sigmoid(g0[...].astype(F32)) * _dot(ym[...], wb[0])
    acc = acc + _sigmoid(g1[...].astype(F32)) * _dot(yn[...], wb[1])
    acc = acc + _sigmoid(g2[...].astype(F32)) * _dot(yl_s[...], wb[2])
    o_ref[...] = acc.astype(BF16)


def _merge(ym, yn, hf, hb, p, li, w_branch):
    rows = ym.shape[0]
    tn = TN_MERGE
    tm = TM_MERGE
    br = lambda: pl.BlockSpec((tm, BRANCH_W), lambda i, j: (i, 0))

    def gate(n):
        return pl.BlockSpec((tm, tn), lambda i, j: (i, (COL_GX + n * D_MODEL) // tn + j))

    return pl.pallas_call(
        _merge_kernel,
        grid=(rows // tm, D_MODEL // tn),
        in_specs=[
            br(), br(), br(), br(),
            pl.BlockSpec((tm, BRANCH_W), lambda i, j: (i, COL_LG // BRANCH_W)),
            gate(0), gate(1), gate(2),
            pl.BlockSpec((None, N_BRANCH, BRANCH_W, tn), lambda i, j: (li, 0, 0, j)),
        ],
        out_specs=pl.BlockSpec((tm, tn), lambda i, j: (i, j)),
        out_shape=jax.ShapeDtypeStruct((rows, D_MODEL), BF16),
        scratch_shapes=[pltpu.VMEM((tm, BRANCH_W), BF16)],
        compiler_params=_cparams(("arbitrary", "arbitrary")),
        name="merge",
    )(ym, yn, hf, hb, p, p, p, p, w_branch)


def _outproj_kernel(s_ref, w_ref, x_ref, g_ref, o_ref):
    o_ref[...] = x_ref[...] + g_ref[...] * _dot(s_ref[...], w_ref[...])


def _outproj(s, li, w_out, x, mod, row_fn):
    rows = x.shape[0]
    tn = TN_OUT
    tm = TM_OUT
    row_of_tile = row_fn(tm)
    return pl.pallas_call(
        _outproj_kernel,
        grid=(rows // tm, D_MODEL // tn),
        in_specs=[
            pl.BlockSpec((tm, D_MODEL), lambda i, j: (i, 0)),
            pl.BlockSpec((None, D_MODEL, tn), lambda i, j: (li, 0, j)),
            pl.BlockSpec((tm, tn), lambda i, j: (i, j)),
            pl.BlockSpec((None, None, None, 1, tn), lambda i, j: (li, row_of_tile(i), 5, 0, j)),
        ],
        out_specs=pl.BlockSpec((tm, tn), lambda i, j: (i, j)),
        out_shape=jax.ShapeDtypeStruct((rows, D_MODEL), F32),
        compiler_params=_cparams(("arbitrary", "arbitrary")),
        name="outproj",
    )(s, w_out, x, mod)


def _split_w_in(w):
    n_gate = 4 * M_HEADS
    o_mg = COL_NQ
    wa = w[:, :, :o_mg].astype(BF16)
    wb = w[:, :, o_mg + n_gate:].astype(BF16)
    mg = w[:, :, o_mg:o_mg + n_gate].reshape(DEPTH, D_MODEL, 2, 2, M_HEADS)
    mg = jnp.transpose(mg, (0, 4, 2, 3, 1)).reshape(DEPTH, M_HEADS, 4, D_MODEL)
    wgt = jnp.pad(mg, ((0, 0), (0, 0), (0, GATE_R - 4), (0, 0)))
    return wa, wb, wgt.reshape(DEPTH, M_HEADS * GATE_R, D_MODEL).astype(BF16)


def _gate_bias(b_i, b_f):
    b = jnp.stack([b_i, b_f], axis=1)
    b = jnp.transpose(b, (2, 0, 1)).reshape(M_HEADS, 4).astype(F32)
    return jnp.pad(b, ((0, 0), (0, GATE_R - 4))).reshape(M_HEADS * GATE_R, 1)


def kernel(x, c, ctx, c_ctx, w_mod, b_mod, norm_ffn1, norm_mix, norm_ffn2, ffn1_w_in, ffn1_w_out, ffn2_w_in,
           ffn2_w_out, w_in, mlstm_b_i, mlstm_b_f, mlstm_gn, na_rpb, lru_conv_w, lru_conv_b, lru_w_a, lru_b_a,
           lru_w_x, lru_b_x, lru_lambda, w_branch, w_out, norm_final):
    assert x.shape == (BATCH, SEQ, D_MODEL) and ctx.shape == (BATCH, CTX_LEN, D_MODEL)
    xl = x.reshape(BATCH * SEQ, D_MODEL)
    xc = ctx.reshape(BATCH * CTX_LEN, D_MODEL)
    c_rows = jnp.concatenate([c, c_ctx[None, :], jnp.zeros((MOD_ROWS - BATCH - 1, D_MODEL), F32)], axis=0)
    mod = _modulation(c_rows, w_mod, b_mod).reshape(DEPTH, MOD_ROWS, N_MOD, 1, D_MODEL)
    cos, sin = _rope_tables()
    w1_in, w1_out = ffn1_w_in.astype(BF16), ffn1_w_out.astype(BF16)
    w2_in, w2_out = ffn2_w_in.astype(BF16), ffn2_w_out.astype(BF16)
    w_head, w_tail, w_gate = _split_w_in(w_in)
    wb = w_branch.astype(BF16)
    wo = w_out.astype(BF16)

    for li in range(DEPTH):
        last = li == DEPTH - 1
        wgt = w_gate[li]
        bgt = _gate_bias(mlstm_b_i[li], mlstm_b_f[li])
        bias = _na_bias(na_rpb[li])
        wa, wx = lru_w_a[li].astype(BF16), lru_w_x[li].astype(BF16)

        xl = _ffn(xl, li, mod, 0, norm_ffn1[li], w1_in, w1_out, _lat_row)
        xc = _ffn(xc, li, mod, 0, norm_ffn1[li], w1_in, w1_out, _ctx_row)

        p_lat, gt_lat = _inproj(xl, li, mod, norm_mix[li], w_head, w_tail, wgt, bgt, _lat_row)
        p_ctx, gt_ctx = _inproj(xc, li, mod, norm_mix[li], w_head, w_tail, wgt, bgt, _ctx_row)

        ym_l, ym_c = _mlstm(p_lat, p_ctx, gt_lat, gt_ctx, cos, sin, mlstm_gn[li])
        yn_l, yn_c = _na(p_lat, p_ctx, bias)
        hf_l, hb_l, hf_c, hb_c = _lru(p_lat, p_ctx, lru_conv_w[li], lru_conv_b[li], wa, lru_b_a[li], wx,
                                      lru_b_x[li], lru_lambda[li])

        s_l = _merge(ym_l, yn_l, hf_l, hb_l, p_lat, li, wb)
        xl = _outproj(s_l, li, wo, xl, mod, _lat_row)
        xl = _ffn(xl, li, mod, 6, norm_ffn2[li], w2_in, w2_out, _lat_row, norm_final if last else None)
        if not last:
            s_c = _merge(ym_c, yn_c, hf_c, hb_c, p_ctx, li, wb)
            xc = _outproj(s_c, li, wo, xc, mod, _ctx_row)
            xc = _ffn(xc, li, mod, 6, norm_ffn2[li], w2_in, w2_out, _ctx_row)
    return xl.reshape(BATCH, SEQ, D_MODEL)
```

```python
import functools

import jax
import jax.numpy as jnp
from jax import lax
from jax.experimental import pallas as pl
from jax.experimental.pallas import tpu as pltpu

F32 = jnp.float32
BF16 = jnp.bfloat16

D_MODEL = 2048
BATCH = 4
SEQ = 2048
DEPTH = 2
GRID_W = 64
CTX_LEN = 256
EPS = 1e-6
D_FF = (11 * D_MODEL) // 4
N_MOD = 9
BRANCH_W = D_MODEL // 2
N_BRANCH = 3
M_HEADS = 4
M_HEAD_DIM = BRANCH_W // M_HEADS
ROPE_BASE = 10000.0
NA_HEADS = 8
NA_HEAD_DIM = BRANCH_W // NA_HEADS
NA_KH = 8
NA_KW = 16
LRU_BLOCKS = 8
LRU_BLOCK_DIM = BRANCH_W // LRU_BLOCKS
LRU_CONV = 4
LRU_C = 8.0

SUBLANES = 8
LANES = 128

MOD_ROWS = 8
CTX_MOD_ROW = BATCH
TM_FFN = 512
TM_IN = 1024
TM_MERGE = 1024
TM_OUT = 1024
TF = 512
TN_IN = 1024
TN_MERGE = 512
TN_OUT = 512
TN_MOD = 1024
M_CHUNK = 256
N_CHUNKS = (SEQ + CTX_LEN) // M_CHUNK
LRU_CHUNK = 256
LRU_STEPS = (SEQ + CTX_LEN) // LRU_CHUNK
GATE_R = SUBLANES

COL_MQ, COL_MK, COL_MV, COL_MO = 0, BRANCH_W, 2 * BRANCH_W, 3 * BRANCH_W
COL_NQ, COL_NK, COL_NV = 4 * BRANCH_W, 5 * BRANCH_W, 6 * BRANCH_W
COL_LX, COL_LG = 7 * BRANCH_W, 8 * BRANCH_W
COL_GX = 9 * BRANCH_W
P_COLS = COL_GX + N_BRANCH * D_MODEL

VMEM_BYTES = 64 * 1024 * 1024
VMEM_LIMIT = VMEM_BYTES - 8 * 1024 * 1024


def _cparams(sem):
    return pltpu.CompilerParams(dimension_semantics=sem, vmem_limit_bytes=VMEM_LIMIT)


def _sigmoid(x):
    return jax.nn.sigmoid(x)


def _log_sigmoid(x):
    return jnp.minimum(x, 0.0) - jnp.log1p(jnp.exp(-jnp.abs(x)))


def _softplus(x):
    return jnp.maximum(x, 0.0) + jnp.log1p(jnp.exp(-jnp.abs(x)))


def _dot(a, b):
    return jnp.dot(a, b, preferred_element_type=F32)


def _dot_nt(a, b):
    return lax.dot_general(a, b, (((1,), (1,)), ((), ())), preferred_element_type=F32)


def _dot_tn(a, b):
    return lax.dot_general(a, b, (((0,), (0,)), ((), ())), preferred_element_type=F32)


def _dot_exact(a, b):
    return jnp.dot(a, b, preferred_element_type=F32, precision=lax.Precision.HIGHEST)


def _dot_split(x, m):
    hi = x.astype(BF16)
    r1 = x - hi.astype(F32)
    mid = r1.astype(BF16)
    lo = (r1 - mid.astype(F32)).astype(BF16)
    return _dot(hi, m) + _dot(mid, m) + _dot(lo, m)


ADALN_ROWS = 16


def _adaln_into(h_ref, x_ref, gain, shift, scale):
    gs = gain * (1.0 + scale)

    def body(i, carry):
        rows = pl.ds(pl.multiple_of(i * ADALN_ROWS, ADALN_ROWS), ADALN_ROWS)
        x = x_ref[rows, :]
        y = x * lax.rsqrt(jnp.mean(x * x, axis=-1, keepdims=True) + EPS)
        h_ref[rows, :] = (y * gs + shift).astype(h_ref.dtype)
        return carry

    lax.fori_loop(0, x_ref.shape[0] // ADALN_ROWS, body, 0, unroll=4)


def _mod_kernel(c_ref, w_ref, b_ref, o_ref):
    c = c_ref[...]
    act = (c * _sigmoid(c)).astype(BF16)
    o_ref[...] = _dot(act, w_ref[...].astype(BF16)) + b_ref[...]


def _modulation(c_rows, w_mod, b_mod):
    n_cols = N_MOD * D_MODEL
    return pl.pallas_call(
        _mod_kernel,
        grid=(DEPTH, n_cols // TN_MOD),
        in_specs=[
            pl.BlockSpec((MOD_ROWS, D_MODEL), lambda l, j: (0, 0)),
            pl.BlockSpec((None, D_MODEL, TN_MOD), lambda l, j: (l, 0, j)),
            pl.BlockSpec((None, 1, TN_MOD), lambda l, j: (l, 0, j)),
        ],
        out_specs=pl.BlockSpec((None, MOD_ROWS, TN_MOD), lambda l, j: (l, 0, j)),
        out_shape=jax.ShapeDtypeStruct((DEPTH, MOD_ROWS, n_cols), F32),
        compiler_params=_cparams(("arbitrary", "arbitrary")),
        name="modulation",
    )(c_rows, w_mod, b_mod.reshape(DEPTH, 1, n_cols))


def _mod_spec(li, k, row_of_tile):
    return pl.BlockSpec((None, None, None, 1, D_MODEL), lambda i, j: (li, row_of_tile(i), k, 0, 0))


def _lat_row(tm):
    return lambda i: i // (SEQ // tm)


def _ctx_row(tm):
    return lambda i: CTX_MOD_ROW


def _ffn_kernel(*refs, final):
    if final:
        x_ref, sh_ref, sc_ref, g_ref, ng_ref, wg_ref, wu_ref, wo_ref, fg_ref, o_ref, h_scr = refs
    else:
        x_ref, sh_ref, sc_ref, g_ref, ng_ref, wg_ref, wu_ref, wo_ref, o_ref, h_scr = refs
    f = pl.program_id(1)

    @pl.when(f == 0)
    def _():
        _adaln_into(h_scr, x_ref, ng_ref[...], sh_ref[...], sc_ref[...])
        o_ref[...] = jnp.zeros_like(o_ref)

    h = h_scr[...]
    gte = _dot(h, wg_ref[...])
    up = _dot(h, wu_ref[...])
    act = (gte * _sigmoid(gte) * up).astype(BF16)
    o_ref[...] += _dot(act, wo_ref[...])

    @pl.when(f == pl.num_programs(1) - 1)
    def _():
        out = x_ref[...] + 0.5 * g_ref[...] * o_ref[...]
        if final:
            out = out * lax.rsqrt(jnp.mean(out * out, axis=-1, keepdims=True) + EPS) * fg_ref[...]
        o_ref[...] = out


def _ffn(x, li, mod, k0, norm_g, w_in, w_out, row_fn, final_gain=None):
    rows = x.shape[0]
    nf = D_FF // TF
    tm = TM_FFN
    row_of_tile = row_fn(tm)
    final = final_gain is not None
    in_specs = [
        pl.BlockSpec((tm, D_MODEL), lambda i, f: (i, 0)),
        _mod_spec(li, k0, row_of_tile), _mod_spec(li, k0 + 1, row_of_tile), _mod_spec(li, k0 + 2, row_of_tile),
        pl.BlockSpec((1, D_MODEL), lambda i, f: (0, 0)),
        pl.BlockSpec((None, D_MODEL, TF), lambda i, f: (li, 0, f)),
        pl.BlockSpec((None, D_MODEL, TF), lambda i, f: (li, 0, f + nf)),
        pl.BlockSpec((None, TF, D_MODEL), lambda i, f: (li, f, 0)),
    ]
    args = [x, mod, mod, mod, norm_g.reshape(1, D_MODEL), w_in, w_in, w_out]
    if final:
        in_specs.append(pl.BlockSpec((1, D_MODEL), lambda i, f: (0, 0)))
        args.append(final_gain.reshape(1, D_MODEL))
    return pl.pallas_call(
        functools.partial(_ffn_kernel, final=final),
        grid=(rows // tm, nf),
        in_specs=in_specs,
        out_specs=pl.BlockSpec((tm, D_MODEL), lambda i, f: (i, 0)),
        out_shape=jax.ShapeDtypeStruct((rows, D_MODEL), F32),
        scratch_shapes=[pltpu.VMEM((tm, D_MODEL), BF16)],
        compiler_params=_cparams(("arbitrary", "arbitrary")),
        name="ffn_final" if final else "ffn",
    )(*args)


N_HEAD_TILES = COL_NQ // TN_IN


def _inproj_kernel(x_ref, sh_ref, sc_ref, ng_ref, wa_ref, wb_ref, wgt_ref, bgt_ref, p_ref, gt_ref, h_scr):
    j = pl.program_id(1)

    @pl.when(j == 0)
    def _():
        _adaln_into(h_scr, x_ref, ng_ref[...], sh_ref[...], sc_ref[...])
        for s in range(TM_IN // M_CHUNK):
            gt_ref[s] = _dot_nt(wgt_ref[...], h_scr[s * M_CHUNK:(s + 1) * M_CHUNK, :]) + bgt_ref[...]

    @pl.when(j < N_HEAD_TILES)
    def _():
        p_ref[...] = _dot(h_scr[...], wa_ref[...]).astype(p_ref.dtype)

    @pl.when(j >= N_HEAD_TILES)
    def _():
        p_ref[...] = _dot(h_scr[...], wb_ref[...]).astype(p_ref.dtype)


def _inproj(x, li, mod, norm_g, wa, wb, wgt, bgt, row_fn):
    rows = x.shape[0]
    tm = TM_IN
    row_of_tile = row_fn(tm)
    return pl.pallas_call(
        _inproj_kernel,
        grid=(rows // tm, P_COLS // TN_IN),
        in_specs=[
            pl.BlockSpec((tm, D_MODEL), lambda i, j: (i, 0)),
            _mod_spec(li, 3, row_of_tile), _mod_spec(li, 4, row_of_tile),
            pl.BlockSpec((1, D_MODEL), lambda i, j: (0, 0)),
            pl.BlockSpec((None, D_MODEL, TN_IN), lambda i, j: (li, 0, jnp.minimum(j, N_HEAD_TILES - 1))),
            pl.BlockSpec((None, D_MODEL, TN_IN), lambda i, j: (li, 0, jnp.maximum(j - N_HEAD_TILES, 0))),
            pl.BlockSpec((M_HEADS * GATE_R, D_MODEL), lambda i, j: (0, 0)),
            pl.BlockSpec((M_HEADS * GATE_R, 1), lambda i, j: (0, 0)),
        ],
        out_specs=[
            pl.BlockSpec((tm, TN_IN), lambda i, j: (i, j)),
            pl.BlockSpec((tm // M_CHUNK, M_HEADS * GATE_R, M_CHUNK), lambda i, j: (i, 0, 0)),
        ],
        out_shape=[
            jax.ShapeDtypeStruct((rows, P_COLS), BF16),
            jax.ShapeDtypeStruct((rows // M_CHUNK, M_HEADS * GATE_R, M_CHUNK), F32),
        ],
        scratch_shapes=[pltpu.VMEM((tm, D_MODEL), BF16)],
        compiler_params=_cparams(("arbitrary", "arbitrary")),
        name="inproj",
    )(x, mod, mod, norm_g.reshape(1, D_MODEL), wa, wb, wgt, bgt)


def _rope(u, cs, sn):
    halves = []
    for j in range(M_HEAD_DIM // LANES):
        sl = slice(j * LANES, (j + 1) * LANES)
        uj = u[:, sl]
        halves.append(uj * cs[:, sl] + pltpu.roll(uj, LANES // 2, 1) * sn[:, sl])
    return jnp.concatenate(halves, axis=1)


def _mlstm_chunk(q, k, v, i_col, cum_col, i_row, cum_row, mask, last, c_mem, n_mem, m_prev):
    logw = jnp.where(mask, cum_col - cum_row + i_row, -jnp.inf)
    m_row = jnp.maximum(cum_col + m_prev, jnp.max(logw, axis=-1, keepdims=True))
    m_new = m_row[last:last + 1, :]
    cum_last = cum_col[last:last + 1, :]
    w_state = jnp.exp(cum_last - cum_col + i_col - m_new)
    decay = jnp.exp(cum_last + m_prev - m_new)
    inter = jnp.exp(cum_col + m_prev - m_row)
    kf = k.astype(F32)
    sc = _dot_nt(q, k) * jnp.exp(logw - m_row)
    num = inter * _dot_nt(q, c_mem.astype(BF16)) + _dot(sc.astype(BF16), v)
    den = inter * jnp.sum(q.astype(F32) * n_mem, axis=-1, keepdims=True) + jnp.sum(sc, axis=-1, keepdims=True)
    hid = num / jnp.maximum(jnp.abs(den), jnp.exp(-m_row))
    c_new = decay * c_mem + _dot_tn((w_state * v.astype(F32)).astype(BF16), k)
    n_new = decay * n_mem + jnp.sum(w_state * kf, axis=0, keepdims=True)
    return hid, c_new, n_new, m_new


def _mlstm_kernel(ql, kl, vl, ol, qc, kc, vc, oc, gtl, gtc, cos, sin, gn, yl, yc,
                  q_scr, k_scr, hf_scr, hb_scr, grow_scr, gcol_scr):
    ch = M_CHUNK
    kscale = M_HEAD_DIM ** -0.5

    def rows(c):
        return slice(c * ch, (c + 1) * ch)

    t_idx = lax.broadcasted_iota(jnp.int32, (ch, ch), 0)
    s_idx = lax.broadcasted_iota(jnp.int32, (ch, ch), 1)
    lower = s_idx <= t_idx
    upper = s_idx >= t_idx
    lower_b = lower.astype(BF16)
    upper_b = upper.astype(BF16)
    gate_row = lax.broadcasted_iota(jnp.int32, (GATE_R, ch), 0)
    lane_pad = jnp.zeros((LANES - GATE_R, ch), F32)

    for c in range(N_CHUNKS):
        if c == 0:
            q_scr[rows(0), :] = qc[...].astype(BF16)
            k_scr[rows(0), :] = (kc[...].astype(F32) * kscale).astype(BF16)
        else:
            lat = rows(c - 1)
            q_scr[rows(c), :] = _rope(ql[lat, :].astype(F32), cos[lat, :], sin[lat, :]).astype(BF16)
            k_scr[rows(c), :] = (_rope(kl[lat, :].astype(F32), cos[lat, :], sin[lat, :]) * kscale).astype(BF16)
        gt = gtc[0] if c == 0 else gtl[c - 1]
        lf = _log_sigmoid(gt)
        terms = jnp.where(gate_row == 1, _dot_split(lf, upper_b), jnp.where(gate_row == 3, _dot_split(lf, lower_b), gt))
        grow_scr[c] = terms
        gcol_scr[rows(c), :] = jnp.concatenate([terms, lane_pad], axis=0).T

    def values(c):
        return vc[...].astype(BF16) if c == 0 else vl[rows(c - 1), :].astype(BF16)

    def init():
        return (jnp.zeros((M_HEAD_DIM, M_HEAD_DIM), F32), jnp.zeros((1, M_HEAD_DIM), F32),
                jnp.full((1, 1), -jnp.inf, F32))

    def run(c, d, mask, last, state):
        g_rows = grow_scr[c]
        g_cols = gcol_scr[rows(c), :]
        i0, f0 = 2 * d, 2 * d + 1
        return _mlstm_chunk(q_scr[rows(c), :], k_scr[rows(c), :], values(c), g_cols[:, i0:i0 + 1],
                            g_cols[:, f0:f0 + 1], g_rows[i0:i0 + 1, :], g_rows[f0:f0 + 1, :], mask, last, *state)

    st_f, st_b = init(), init()
    for step in range(N_CHUNKS):
        cf = step
        cb = 0 if step == 0 else N_CHUNKS - step
        hid, *st_f = run(cf, 0, lower, ch - 1, st_f)
        hf_scr[rows(cf), :] = hid
        hid, *st_b = run(cb, 1, upper, 0, st_b)
        hb_scr[rows(cb), :] = hid

    gain = gn[...]
    for c in range(N_CHUNKS):
        hsum = hf_scr[rows(c), :] + hb_scr[rows(c), :]
        hn = hsum * lax.rsqrt(jnp.mean(hsum * hsum, axis=-1, keepdims=True) + EPS) * gain
        if c == 0:
            yc[...] = (_sigmoid(oc[...].astype(F32)) * hn).astype(yc.dtype)
        else:
            yl[rows(c - 1), :] = (_sigmoid(ol[rows(c - 1), :].astype(F32)) * hn).astype(yl.dtype)


def _mlstm(p_lat, p_ctx, gt_lat, gt_ctx, cos, sin, gn):
    hd = M_HEAD_DIM
    cb = lambda col: col // hd

    def lat(col):
        return pl.BlockSpec((SEQ, hd), lambda b, h: (b, cb(col) + h))

    def ctx(col):
        return pl.BlockSpec((CTX_LEN, hd), lambda b, h: (b, cb(col) + h))

    seq_chunks = SEQ // M_CHUNK
    ctx_chunks = CTX_LEN // M_CHUNK
    return pl.pallas_call(
        _mlstm_kernel,
        grid=(BATCH, M_HEADS),
        in_specs=[
            lat(COL_MQ), lat(COL_MK), lat(COL_MV), lat(COL_MO),
            ctx(COL_MQ), ctx(COL_MK), ctx(COL_MV), ctx(COL_MO),
            pl.BlockSpec((seq_chunks, GATE_R, M_CHUNK), lambda b, h: (b, h, 0)),
            pl.BlockSpec((ctx_chunks, GATE_R, M_CHUNK), lambda b, h: (b, h, 0)),
            pl.BlockSpec((SEQ, hd), lambda b, h: (0, 0)),
            pl.BlockSpec((SEQ, hd), lambda b, h: (0, 0)),
            pl.BlockSpec((1, hd), lambda b, h: (0, h)),
        ],
        out_specs=[
            pl.BlockSpec((SEQ, hd), lambda b, h: (b, h)),
            pl.BlockSpec((CTX_LEN, hd), lambda b, h: (b, h)),
        ],
        out_shape=[
            jax.ShapeDtypeStruct((BATCH * SEQ, BRANCH_W), BF16),
            jax.ShapeDtypeStruct((BATCH * CTX_LEN, BRANCH_W), BF16),
        ],
        scratch_shapes=[
            pltpu.VMEM((SEQ + CTX_LEN, hd), BF16), pltpu.VMEM((SEQ + CTX_LEN, hd), BF16),
            pltpu.VMEM((SEQ + CTX_LEN, hd), F32), pltpu.VMEM((SEQ + CTX_LEN, hd), F32),
            pltpu.VMEM((N_CHUNKS, GATE_R, M_CHUNK), F32), pltpu.VMEM((SEQ + CTX_LEN, LANES), F32),
        ],
        compiler_params=_cparams(("arbitrary", "arbitrary")),
        name="mlstm",
    )(p_lat, p_lat, p_lat, p_lat, p_ctx, p_ctx, p_ctx, p_ctx, gt_lat, gt_ctx, cos, sin, gn.reshape(1, BRANCH_W))


def _rope_tables():
    pos = jnp.arange(SEQ)
    nf = M_HEAD_DIM // 4
    inv = ROPE_BASE ** (-jnp.arange(nf, dtype=F32) / nf)
    ang_r = (pos // GRID_W).astype(F32)[:, None] * inv[None, :]
    ang_c = (pos % GRID_W).astype(F32)[:, None] * inv[None, :]
    cr, sr, cc, sc = jnp.cos(ang_r), jnp.sin(ang_r), jnp.cos(ang_c), jnp.sin(ang_c)
    return (jnp.concatenate([cr, cr, cc, cc], axis=-1), jnp.concatenate([-sr, sr, -sc, sc], axis=-1))


NA_GROUP = 2
NA_ROWS = SEQ // GRID_W
NA_QROWS = 4
NA_UNION = NA_KH + NA_QROWS
NA_QT = NA_QROWS * GRID_W
NA_KT = NA_UNION * GRID_W
NA_QGROUPS = NA_ROWS // NA_QROWS
NA_TYPES = 3


def _na_union_start(g):
    return max(0, min(NA_QROWS * g - NA_KH // 2, NA_ROWS - NA_UNION))


def _na_kernel(ql, kl, vl, qc, kc, vc, bias, yl, yc):
    scale = NA_HEAD_DIM ** -0.5
    kcs = [kc[:, hh * NA_HEAD_DIM:(hh + 1) * NA_HEAD_DIM].astype(BF16) for hh in range(NA_GROUP)]
    vcs = [vc[:, hh * NA_HEAD_DIM:(hh + 1) * NA_HEAD_DIM].astype(BF16) for hh in range(NA_GROUP)]

    def group_body(g, carry):
        u = jnp.clip(NA_QROWS * g - NA_KH // 2, 0, NA_ROWS - NA_UNION)
        kind = jnp.where(g == 0, 0, jnp.where(g == NA_QGROUPS - 1, NA_TYPES - 1, 1))
        q_rows = pl.ds(pl.multiple_of(g * NA_QT, NA_QT), NA_QT)
        k_rows = pl.ds(pl.multiple_of(u * GRID_W, GRID_W), NA_KT)
        outs = []
        for hh in range(NA_GROUP):
            sl = slice(hh * NA_HEAD_DIM, (hh + 1) * NA_HEAD_DIM)
            q = ql[q_rows, sl].astype(BF16)
            kw = kl[k_rows, sl].astype(BF16)
            vw = vl[k_rows, sl].astype(BF16)
            s_win = _dot_nt(q, kw) * scale + bias[hh, kind]
            s_ctx = _dot_nt(q, kcs[hh]) * scale
            m = jnp.maximum(jnp.max(s_win, axis=-1, keepdims=True), jnp.max(s_ctx, axis=-1, keepdims=True))
            p_win = jnp.exp(s_win - m)
            p_ctx = jnp.exp(s_ctx - m)
            denom = jnp.sum(p_win, axis=-1, keepdims=True) + jnp.sum(p_ctx, axis=-1, keepdims=True)
            o = _dot(p_win.astype(BF16), vw) + _dot(p_ctx.astype(BF16), vcs[hh])
            outs.append(o / denom)
        yl[q_rows, :] = jnp.concatenate(outs, axis=1).astype(yl.dtype)
        return carry

    lax.fori_loop(0, NA_QGROUPS, group_body, 0)

    outs = []
    for hh in range(NA_GROUP):
        sl = slice(hh * NA_HEAD_DIM, (hh + 1) * NA_HEAD_DIM)
        s = _dot_nt(qc[:, sl].astype(BF16), kcs[hh]) * scale
        p = jnp.exp(s - jnp.max(s, axis=-1, keepdims=True))
        outs.append(_dot(p.astype(BF16), vcs[hh]) / jnp.sum(p, axis=-1, keepdims=True))
    yc[...] = jnp.concatenate(outs, axis=1).astype(yc.dtype)


def _na_bias_kernel(rpb_ref, o_ref):
    shape = (LANES, GRID_W * GRID_W)
    n = lax.broadcasted_iota(jnp.int32, shape, 1)
    c = lax.broadcasted_iota(jnp.int32, shape, 0)
    q = jnp.right_shift(n, GRID_W.bit_length() - 1)
    k = jnp.bitwise_and(n, GRID_W - 1)
    dc = jnp.clip(k - q, -(NA_KW - 1), NA_KW - 1) + (NA_KW - 1)
    picked = _dot_exact(rpb_ref[...], (c == dc).astype(F32))
    col_start = jnp.clip(q - NA_KW // 2, 0, GRID_W - NA_KW)
    col_ok = (k >= col_start) & (k < col_start + NA_KW)
    o_ref[...] = jnp.where(col_ok, picked, -jnp.inf)


def _na_bias(rpb):
    assert _na_union_start(1) == 0 and _na_union_start(2) == NA_QROWS
    assert GRID_W & (GRID_W - 1) == 0 and NA_HEADS * (2 * NA_KH - 1) <= LANES and 2 * NA_KW - 1 <= LANES
    n_dr = 2 * NA_KH - 1
    flat = rpb.astype(F32).reshape(NA_HEADS * n_dr, 2 * NA_KW - 1)
    flat = jnp.pad(flat, ((0, LANES - NA_HEADS * n_dr), (0, LANES - (2 * NA_KW - 1))))
    t = pl.pallas_call(
        _na_bias_kernel,
        out_shape=jax.ShapeDtypeStruct((LANES, GRID_W * GRID_W), F32),
        name="natten_bias",
    )(flat)
    t = t[:NA_HEADS * n_dr].reshape(NA_HEADS, n_dr, GRID_W, GRID_W)
    tp = jnp.pad(t, ((0, 0), (NA_UNION, NA_UNION), (0, 0), (0, 0)), constant_values=-jnp.inf)
    runs = []
    for g in (0, 1, NA_QGROUPS - 1):
        u = _na_union_start(g)
        for i in range(NA_QROWS):
            r = NA_QROWS * g + i
            start = max(0, min(r - NA_KH // 2, NA_ROWS - NA_KH))
            dr0 = u - r + NA_KH - 1 + NA_UNION
            in_window = jnp.array([start <= u + j < start + NA_KH for j in range(NA_UNION)])
            runs.append(jnp.where(in_window[None, :, None, None], tp[:, dr0:dr0 + NA_UNION], -jnp.inf))
    table = jnp.stack(runs, axis=1).reshape(NA_HEADS, NA_TYPES, NA_QROWS, NA_UNION, GRID_W, GRID_W)
    return jnp.transpose(table, (0, 1, 2, 4, 3, 5)).reshape(NA_HEADS, NA_TYPES, NA_QT, NA_KT)


def _na(p_lat, p_ctx, bias):
    gw = NA_GROUP * NA_HEAD_DIM
    cb = lambda col: col // gw

    def lat(col):
        return pl.BlockSpec((SEQ, gw), lambda b, g: (b, cb(col) + g))

    def ctx(col):
        return pl.BlockSpec((CTX_LEN, gw), lambda b, g: (b, cb(col) + g))

    return pl.pallas_call(
        _na_kernel,
        grid=(BATCH, NA_HEADS // NA_GROUP),
        in_specs=[
            lat(COL_NQ), lat(COL_NK), lat(COL_NV), ctx(COL_NQ), ctx(COL_NK), ctx(COL_NV),
            pl.BlockSpec((NA_GROUP, NA_TYPES, NA_QT, NA_KT), lambda b, g: (g, 0, 0, 0)),
        ],
        out_specs=[
            pl.BlockSpec((SEQ, gw), lambda b, g: (b, g)),
            pl.BlockSpec((CTX_LEN, gw), lambda b, g: (b, g)),
        ],
        out_shape=[
            jax.ShapeDtypeStruct((BATCH * SEQ, BRANCH_W), BF16),
            jax.ShapeDtypeStruct((BATCH * CTX_LEN, BRANCH_W), BF16),
        ],
        compiler_params=_cparams(("arbitrary", "arbitrary")),
        name="natten",
    )(p_lat, p_lat, p_lat, p_ctx, p_ctx, p_ctx, bias)


LRU_HALO = 2 * SUBLANES


def _lru_kernel(xl, xc, cw, cb, wa, ba, wx, bx, lam, hfl, hbl, hfc, hbc,
                ext_f, ext_b, a_f, u_f, a_b, u_b, carry):
    s = pl.program_id(1)
    ch = LRU_CHUNK
    n_lat = SEQ // ch
    halo = LRU_HALO
    zeros_halo = jnp.zeros((halo, BRANCH_W), F32)

    def fill(ext, chunk):
        @pl.when(s == 0)
        def _():
            ext[0:halo, :] = zeros_halo
            ext[halo:halo + ch, :] = xc[...].astype(F32)
            ext[halo + ch:, :] = zeros_halo

        @pl.when(s > 0)
        def _():
            base = pl.multiple_of(chunk * ch, ch)
            ext[halo:halo + ch, :] = xl[pl.ds(base, ch), :].astype(F32)
            prev = xl[pl.ds(pl.multiple_of(jnp.maximum(base - halo, 0), halo), halo), :].astype(F32)
            nxt = xl[pl.ds(pl.multiple_of(jnp.minimum(base + ch, SEQ - halo), halo), halo), :].astype(F32)
            ext[0:halo, :] = jnp.where(chunk > 0, prev, 0.0)
            ext[halo + ch:, :] = jnp.where(chunk < n_lat - 1, nxt, 0.0)

    chunk_f = jnp.maximum(s - 1, 0)
    chunk_b = jnp.maximum(n_lat - s, 0)
    fill(ext_f, chunk_f)
    fill(ext_b, chunk_b)

    def coeffs(ext, d, a_out, u_out):
        xconv = cb[...] + ext[halo - 1:halo - 1 + ch, :] * cw[0:1, :]
        for j in range(1, LRU_CONV):
            xconv = xconv + ext[halo - 1 + j:halo - 1 + j + ch, :] * cw[j:j + 1, :]
        sp = _softplus(-lam[d:d + 1, :])
        for g in range(LRU_BLOCKS):
            sl = slice(g * LRU_BLOCK_DIM, (g + 1) * LRU_BLOCK_DIM)
            xg = xconv[:, sl]
            xb = xg.astype(BF16)
            r = _sigmoid(_dot(xb, wa[d, g]) + ba[d:d + 1, sl])
            i = _sigmoid(_dot(xb, wx[d, g]) + bx[d:d + 1, sl])
            log_a = -LRU_C * r * sp[:, sl]
            a = jnp.exp(log_a)
            a_out[:, sl] = a
            u_out[:, sl] = jnp.sqrt(jnp.tanh(-log_a) * (a * a + 1.0)) * i * xg

    coeffs(ext_f, 0, a_f, u_f)
    coeffs(ext_b, 1, a_b, u_b)

    @pl.when(s == 0)
    def _():
        carry[...] = jnp.zeros_like(carry)

    row = lax.broadcasted_iota(jnp.int32, (SUBLANES, BRANCH_W), 0)
    n_tiles = ch // SUBLANES

    def scan_tiles(hf_out, hb_out):
        def body(t, hs):
            h_f, h_b = hs
            r0 = pl.multiple_of(t * SUBLANES, SUBLANES)
            a = a_f[pl.ds(r0, SUBLANES), :]
            u = u_f[pl.ds(r0, SUBLANES), :]
            for d in (1, 2, 4):
                ok = row >= d
                u = jnp.where(ok, a * pltpu.roll(u, d, 0) + u, u)
                a = jnp.where(ok, a * pltpu.roll(a, d, 0), a)
            h = u + a * h_f
            hf_out[pl.ds(r0, SUBLANES), :] = h
            h_f = h[SUBLANES - 1:SUBLANES, :]
            r1 = pl.multiple_of((n_tiles - 1 - t) * SUBLANES, SUBLANES)
            a = a_b[pl.ds(r1, SUBLANES), :]
            u = u_b[pl.ds(r1, SUBLANES), :]
            for d in (1, 2, 4):
                ok = row < SUBLANES - d
                u = jnp.where(ok, a * pltpu.roll(u, SUBLANES - d, 0) + u, u)
                a = jnp.where(ok, a * pltpu.roll(a, SUBLANES - d, 0), a)
            h = u + a * h_b
            hb_out[pl.ds(r1, SUBLANES), :] = h
            h_b = h[0:1, :]
            return h_f, h_b

        h_f, h_b = lax.fori_loop(0, n_tiles, body, (carry[0:1, :], carry[1:2, :]))
        carry[0:1, :] = h_f
        carry[1:2, :] = h_b

    @pl.when(s == 0)
    def _():
        scan_tiles(hfc, hbc)

    @pl.when(s > 0)
    def _():
        scan_tiles(hfl, hbl)


def _lru(p_lat, p_ctx, conv_w, conv_b, w_a, b_a, w_x, b_x, lam):
    ch = LRU_CHUNK
    n_lat = SEQ // ch
    full = lambda shape: pl.BlockSpec(shape, lambda b, s: (0,) * len(shape))
    wblk = (2, LRU_BLOCKS, LRU_BLOCK_DIM, LRU_BLOCK_DIM)
    return pl.pallas_call(
        _lru_kernel,
        grid=(BATCH, LRU_STEPS),
        in_specs=[
            pl.BlockSpec((SEQ, BRANCH_W), lambda b, s: (b, COL_LX // BRANCH_W)),
            pl.BlockSpec((CTX_LEN, BRANCH_W), lambda b, s: (b, COL_LX // BRANCH_W)),
            full((LRU_CONV, BRANCH_W)), full((1, BRANCH_W)),
            full(wblk), full((2, BRANCH_W)), full(wblk), full((2, BRANCH_W)), full((2, BRANCH_W)),
        ],
        out_specs=[
            pl.BlockSpec((ch, BRANCH_W), lambda b, s: (b * n_lat + jnp.maximum(s - 1, 0), 0)),
            pl.BlockSpec((ch, BRANCH_W), lambda b, s: (b * n_lat + jnp.minimum(n_lat - s, n_lat - 1), 0)),
            pl.BlockSpec((CTX_LEN, BRANCH_W), lambda b, s: (b, 0)),
            pl.BlockSpec((CTX_LEN, BRANCH_W), lambda b, s: (b, 0)),
        ],
        out_shape=[
            jax.ShapeDtypeStruct((BATCH * SEQ, BRANCH_W), F32),
            jax.ShapeDtypeStruct((BATCH * SEQ, BRANCH_W), F32),
            jax.ShapeDtypeStruct((BATCH * CTX_LEN, BRANCH_W), F32),
            jax.ShapeDtypeStruct((BATCH * CTX_LEN, BRANCH_W), F32),
        ],
        scratch_shapes=[
            pltpu.VMEM((ch + 2 * LRU_HALO, BRANCH_W), F32), pltpu.VMEM((ch + 2 * LRU_HALO, BRANCH_W), F32),
            pltpu.VMEM((ch, BRANCH_W), F32), pltpu.VMEM((ch, BRANCH_W), F32),
            pltpu.VMEM((ch, BRANCH_W), F32), pltpu.VMEM((ch, BRANCH_W), F32),
            pltpu.VMEM((SUBLANES, BRANCH_W), F32),
        ],
        compiler_params=_cparams(("arbitrary", "arbitrary")),
        name="rglru",
    )(p_lat, p_ctx, conv_w, conv_b.reshape(1, BRANCH_W), w_a, b_a, w_x, b_x, lam)


def _merge_kernel(ym, yn, hf, hb, lg, g0, g1, g2, wb, o_ref, yl_s):
    j = pl.program_id(1)

    @pl.when(j == 0)
    def _():
        yl_s[...] = ((hf[...] + hb[...]) * jax.nn.gelu(lg[...].astype(F32))).astype(BF16)

    acc = _sigmoid(g0[...].astype(F32)) * _dot(ym[...], wb[0])
    acc = acc + _sigmoid(g1[...].astype(F32)) * _dot(yn[...], wb[1])
    acc = acc + _sigmoid(g2[...].astype(F32)) * _dot(yl_s[...], wb[2])
    o_ref[...] = acc.astype(BF16)


def _merge(ym, yn, hf, hb, p, li, w_branch):
    rows = ym.shape[0]
    tn = TN_MERGE
    tm = TM_MERGE
    br = lambda: pl.BlockSpec((tm, BRANCH_W), lambda i, j: (i, 0))

    def gate(n):
        return pl.BlockSpec((tm, tn), lambda i, j: (i, (COL_GX + n * D_MODEL) // tn + j))

    return pl.pallas_call(
        _merge_kernel,
        grid=(rows // tm, D_MODEL // tn),
        in_specs=[
            br(), br(), br(), br(),
            pl.BlockSpec((tm, BRANCH_W), lambda i, j: (i, COL_LG // BRANCH_W)),
            gate(0), gate(1), gate(2),
            pl.BlockSpec((None, N_BRANCH, BRANCH_W, tn), lambda i, j: (li, 0, 0, j)),
        ],
        out_specs=pl.BlockSpec((tm, tn), lambda i, j: (i, j)),
        out_shape=jax.ShapeDtypeStruct((rows, D_MODEL), BF16),
        scratch_shapes=[pltpu.VMEM((tm, BRANCH_W), BF16)],
        compiler_params=_cparams(("arbitrary", "arbitrary")),
        name="merge",
    )(ym, yn, hf, hb, p, p, p, p, w_branch)


def _outproj_kernel(s_ref, w_ref, x_ref, g_ref, o_ref):
    o_ref[...] = x_ref[...] + g_ref[...] * _dot(s_ref[...], w_ref[...])


def _outproj(s, li, w_out, x, mod, row_fn):
    rows = x.shape[0]
    tn = TN_OUT
    tm = TM_OUT
    row_of_tile = row_fn(tm)
    return pl.pallas_call(
        _outproj_kernel,
        grid=(rows // tm, D_MODEL // tn),
        in_specs=[
            pl.BlockSpec((tm, D_MODEL), lambda i, j: (i, 0)),
            pl.BlockSpec((None, D_MODEL, tn), lambda i, j: (li, 0, j)),
            pl.BlockSpec((tm, tn), lambda i, j: (i, j)),
            pl.BlockSpec((None, None, None, 1, tn), lambda i, j: (li, row_of_tile(i), 5, 0, j)),
        ],
        out_specs=pl.BlockSpec((tm, tn), lambda i, j: (i, j)),
        out_shape=jax.ShapeDtypeStruct((rows, D_MODEL), F32),
        compiler_params=_cparams(("arbitrary", "arbitrary")),
        name="outproj",
    )(s, w_out, x, mod)


W_IN_ROWS = 64
N_GATE = 4 * M_HEADS


def _split_kernel(w_ref, wa_ref, wb_ref, wg_ref):
    w = w_ref[...]
    wa_ref[...] = w[:, :COL_NQ].astype(BF16)
    wg_ref[...] = w[:, COL_NQ:COL_NQ + N_GATE]
    wb_ref[...] = w[:, COL_NQ + N_GATE:].astype(BF16)


def _split_w_in(w):
    p_in = w.shape[-1]
    n_tail = p_in - COL_NQ - N_GATE
    wa, wb, mg = pl.pallas_call(
        _split_kernel,
        grid=(DEPTH, D_MODEL // W_IN_ROWS),
        in_specs=[pl.BlockSpec((None, W_IN_ROWS, p_in), lambda l, i: (l, i, 0))],
        out_specs=[
            pl.BlockSpec((None, W_IN_ROWS, COL_NQ), lambda l, i: (l, i, 0)),
            pl.BlockSpec((None, W_IN_ROWS, n_tail), lambda l, i: (l, i, 0)),
            pl.BlockSpec((None, W_IN_ROWS, N_GATE), lambda l, i: (l, i, 0)),
        ],
        out_shape=[
            jax.ShapeDtypeStruct((DEPTH, D_MODEL, COL_NQ), BF16),
            jax.ShapeDtypeStruct((DEPTH, D_MODEL, n_tail), BF16),
            jax.ShapeDtypeStruct((DEPTH, D_MODEL, N_GATE), F32),
        ],
        compiler_params=_cparams(("arbitrary", "arbitrary")),
        name="split_w_in",
    )(w)
    mg = mg.reshape(DEPTH, D_MODEL, 2, 2, M_HEADS)
    mg = jnp.transpose(mg, (0, 4, 2, 3, 1)).reshape(DEPTH, M_HEADS, 4, D_MODEL)
    wgt = jnp.pad(mg, ((0, 0), (0, 0), (0, GATE_R - 4), (0, 0)))
    return wa, wb, wgt.reshape(DEPTH, M_HEADS * GATE_R, D_MODEL).astype(BF16)


def _gate_bias(b_i, b_f):
    b = jnp.stack([b_i, b_f], axis=1)
    b = jnp.transpose(b, (2, 0, 1)).reshape(M_HEADS, 4).astype(F32)
    return jnp.pad(b, ((0, 0), (0, GATE_R - 4))).reshape(M_HEADS * GATE_R, 1)


def kernel(x, c, ctx, c_ctx, w_mod, b_mod, norm_ffn1, norm_mix, norm_ffn2, ffn1_w_in, ffn1_w_out, ffn2_w_in,
           ffn2_w_out, w_in, mlstm_b_i, mlstm_b_f, mlstm_gn, na_rpb, lru_conv_w, lru_conv_b, lru_w_a, lru_b_a,
           lru_w_x, lru_b_x, lru_lambda, w_branch, w_out, norm_final):
    assert x.shape == (BATCH, SEQ, D_MODEL) and ctx.shape == (BATCH, CTX_LEN, D_MODEL)
    xl = x.reshape(BATCH * SEQ, D_MODEL)
    xc = ctx.reshape(BATCH * CTX_LEN, D_MODEL)
    c_rows = jnp.concatenate([c, c_ctx[None, :], jnp.zeros((MOD_ROWS - BATCH - 1, D_MODEL), F32)], axis=0)
    mod = _modulation(c_rows, w_mod, b_mod).reshape(DEPTH, MOD_ROWS, N_MOD, 1, D_MODEL)
    cos, sin = _rope_tables()
    w1_in, w1_out = ffn1_w_in.astype(BF16), ffn1_w_out.astype(BF16)
    w2_in, w2_out = ffn2_w_in.astype(BF16), ffn2_w_out.astype(BF16)
    w_head, w_tail, w_gate = _split_w_in(w_in)
    wb = w_branch.astype(BF16)
    wo = w_out.astype(BF16)

    for li in range(DEPTH):
        last = li == DEPTH - 1
        wgt = w_gate[li]
        bgt = _gate_bias(mlstm_b_i[li], mlstm_b_f[li])
        bias = _na_bias(na_rpb[li])
        wa, wx = lru_w_a[li].astype(BF16), lru_w_x[li].astype(BF16)

        xl = _ffn(xl, li, mod, 0, norm_ffn1[li], w1_in, w1_out, _lat_row)
        xc = _ffn(xc, li, mod, 0, norm_ffn1[li], w1_in, w1_out, _ctx_row)

        p_lat, gt_lat = _inproj(xl, li, mod, norm_mix[li], w_head, w_tail, wgt, bgt, _lat_row)
        p_ctx, gt_ctx = _inproj(xc, li, mod, norm_mix[li], w_head, w_tail, wgt, bgt, _ctx_row)

        ym_l, ym_c = _mlstm(p_lat, p_ctx, gt_lat, gt_ctx, cos, sin, mlstm_gn[li])
        yn_l, yn_c = _na(p_lat, p_ctx, bias)
        hf_l, hb_l, hf_c, hb_c = _lru(p_lat, p_ctx, lru_conv_w[li], lru_conv_b[li], wa, lru_b_a[li], wx,
                                      lru_b_x[li], lru_lambda[li])

        s_l = _merge(ym_l, yn_l, hf_l, hb_l, p_lat, li, wb)
        xl = _outproj(s_l, li, wo, xl, mod, _lat_row)
        xl = _ffn(xl, li, mod, 6, norm_ffn2[li], w2_in, w2_out, _lat_row, norm_final if last else None)
        if not last:
            s_c = _merge(ym_c, yn_c, hf_c, hb_c, p_ctx, li, wb)
            xc = _outproj(s_c, li, wo, xc, mod, _ctx_row)
            xc = _ffn(xc, li, mod, 6, norm_ffn2[li], w2_in, w2_out, _ctx_row)
    return xl.reshape(BATCH, SEQ, D_MODEL)
```

```python
import functools

import jax
import jax.numpy as jnp
from jax import lax
from jax.experimental import pallas as pl
from jax.experimental.pallas import tpu as pltpu

F32 = jnp.float32
BF16 = jnp.bfloat16

D_MODEL = 2048
BATCH = 4
SEQ = 2048
DEPTH = 2
GRID_W = 64
CTX_LEN = 256
EPS = 1e-6
D_FF = (11 * D_MODEL) // 4
N_MOD = 9
BRANCH_W = D_MODEL // 2
N_BRANCH = 3
M_HEADS = 4
M_HEAD_DIM = BRANCH_W // M_HEADS
ROPE_BASE = 10000.0
NA_HEADS = 8
NA_HEAD_DIM = BRANCH_W // NA_HEADS
NA_KH = 8
NA_KW = 16
LRU_BLOCKS = 8
LRU_BLOCK_DIM = BRANCH_W // LRU_BLOCKS
LRU_CONV = 4
LRU_C = 8.0

SUBLANES = 8
LANES = 128

MOD_ROWS = 8
CTX_MOD_ROW = BATCH
TM_FFN = 512
TM_IN = 1024
TM_MERGE = 1024
TM_OUT = 1024
TF = 512
TN_IN = 1024
TN_MERGE = 512
TN_OUT = 512
TN_MOD = 1024
M_CHUNK = 256
N_CHUNKS = (SEQ + CTX_LEN) // M_CHUNK
LRU_CHUNK = 256
LRU_STEPS = (SEQ + CTX_LEN) // LRU_CHUNK
GATE_R = SUBLANES

COL_MQ, COL_MK, COL_MV, COL_MO = 0, BRANCH_W, 2 * BRANCH_W, 3 * BRANCH_W
COL_NQ, COL_NK, COL_NV = 4 * BRANCH_W, 5 * BRANCH_W, 6 * BRANCH_W
COL_LX, COL_LG = 7 * BRANCH_W, 8 * BRANCH_W
COL_GX = 9 * BRANCH_W
P_COLS = COL_GX + N_BRANCH * D_MODEL

VMEM_BYTES = 64 * 1024 * 1024
VMEM_LIMIT = VMEM_BYTES - 8 * 1024 * 1024


def _cparams(sem):
    return pltpu.CompilerParams(dimension_semantics=sem, vmem_limit_bytes=VMEM_LIMIT)


def _sigmoid(x):
    return jax.nn.sigmoid(x)


def _log_sigmoid(x):
    return jnp.minimum(x, 0.0) - jnp.log1p(jnp.exp(-jnp.abs(x)))


def _softplus(x):
    return jnp.maximum(x, 0.0) + jnp.log1p(jnp.exp(-jnp.abs(x)))


def _dot(a, b):
    return jnp.dot(a, b, preferred_element_type=F32)


def _dot_nt(a, b):
    return lax.dot_general(a, b, (((1,), (1,)), ((), ())), preferred_element_type=F32)


def _dot_tn(a, b):
    return lax.dot_general(a, b, (((0,), (0,)), ((), ())), preferred_element_type=F32)


def _dot_exact(a, b):
    return jnp.dot(a, b, preferred_element_type=F32, precision=lax.Precision.HIGHEST)


def _dot_split(x, m):
    hi = x.astype(BF16)
    r1 = x - hi.astype(F32)
    mid = r1.astype(BF16)
    lo = (r1 - mid.astype(F32)).astype(BF16)
    return _dot(hi, m) + _dot(mid, m) + _dot(lo, m)


def _adaln(x, gain, shift, scale):
    y = x * lax.rsqrt(jnp.mean(x * x, axis=-1, keepdims=True) + EPS)
    return (y * gain) * (1.0 + scale) + shift


ADALN_ROWS = 16


def _adaln_into(h_ref, x_ref, gain, shift, scale):
    gs = gain * (1.0 + scale)

    def body(i, carry):
        rows = pl.ds(pl.multiple_of(i * ADALN_ROWS, ADALN_ROWS), ADALN_ROWS)
        x = x_ref[rows, :]
        y = x * lax.rsqrt(jnp.mean(x * x, axis=-1, keepdims=True) + EPS)
        h_ref[rows, :] = (y * gs + shift).astype(h_ref.dtype)
        return carry

    lax.fori_loop(0, x_ref.shape[0] // ADALN_ROWS, body, 0, unroll=4)


def _mod_kernel(c_ref, w_ref, b_ref, o_ref):
    c = c_ref[...]
    act = (c * _sigmoid(c)).astype(BF16)
    o_ref[...] = _dot(act, w_ref[...].astype(BF16)) + b_ref[...]


def _modulation(c_rows, w_mod, b_mod):
    n_cols = N_MOD * D_MODEL
    return pl.pallas_call(
        _mod_kernel,
        grid=(DEPTH, n_cols // TN_MOD),
        in_specs=[
            pl.BlockSpec((MOD_ROWS, D_MODEL), lambda l, j: (0, 0)),
            pl.BlockSpec((None, D_MODEL, TN_MOD), lambda l, j: (l, 0, j)),
            pl.BlockSpec((None, 1, TN_MOD), lambda l, j: (l, 0, j)),
        ],
        out_specs=pl.BlockSpec((None, MOD_ROWS, TN_MOD), lambda l, j: (l, 0, j)),
        out_shape=jax.ShapeDtypeStruct((DEPTH, MOD_ROWS, n_cols), F32),
        compiler_params=_cparams(("arbitrary", "arbitrary")),
        name="modulation",
    )(c_rows, w_mod, b_mod.reshape(DEPTH, 1, n_cols))


def _mod_spec(li, k, row_of_tile):
    return pl.BlockSpec((None, None, None, 1, D_MODEL), lambda i, j: (li, row_of_tile(i), k, 0, 0))


def _lat_row(tm):
    return lambda i: i // (SEQ // tm)


def _ctx_row(tm):
    return lambda i: CTX_MOD_ROW


def _ffn_kernel(*refs, final, cast):
    refs = list(refs)
    x_ref, sh_ref, sc_ref, g_ref, ng_ref, wg_ref, wu_ref, wo_ref = refs[:8]
    del refs[:8]
    fg_ref = refs.pop(0) if final else None
    if cast:
        src_in, src_out, o_ref, dst_in, dst_out, h_scr = refs
        dst_in[...] = src_in[...].astype(BF16)
        dst_out[...] = src_out[...].astype(BF16)
    else:
        o_ref, h_scr = refs
    f = pl.program_id(1)

    @pl.when(f == 0)
    def _():
        _adaln_into(h_scr, x_ref, ng_ref[...], sh_ref[...], sc_ref[...])
        o_ref[...] = jnp.zeros_like(o_ref)

    h = h_scr[...]
    gte = _dot(h, wg_ref[...])
    up = _dot(h, wu_ref[...])
    act = (gte * _sigmoid(gte) * up).astype(BF16)
    o_ref[...] += _dot(act, wo_ref[...])

    @pl.when(f == pl.num_programs(1) - 1)
    def _():
        out = x_ref[...] + 0.5 * g_ref[...] * o_ref[...]
        if final:
            out = out * lax.rsqrt(jnp.mean(out * out, axis=-1, keepdims=True) + EPS) * fg_ref[...]
        o_ref[...] = out


def _ffn(x, li, mod, k0, norm_g, w_in, w_out, row_fn, final_gain=None, cast_job=None):
    rows = x.shape[0]
    nf = D_FF // TF
    tm = TM_FFN
    nt = rows // tm
    row_of_tile = row_fn(tm)
    final = final_gain is not None
    cast = cast_job is not None
    in_specs = [
        pl.BlockSpec((tm, D_MODEL), lambda i, f: (i, 0)),
        _mod_spec(li, k0, row_of_tile), _mod_spec(li, k0 + 1, row_of_tile), _mod_spec(li, k0 + 2, row_of_tile),
        pl.BlockSpec((1, D_MODEL), lambda i, f: (0, 0)),
        pl.BlockSpec((D_MODEL, TF), lambda i, f: (0, f)),
        pl.BlockSpec((D_MODEL, TF), lambda i, f: (0, f + nf)),
        pl.BlockSpec((TF, D_MODEL), lambda i, f: (f, 0)),
    ]
    args = [x, mod, mod, mod, norm_g.reshape(1, D_MODEL), w_in, w_in, w_out]
    if final:
        in_specs.append(pl.BlockSpec((1, D_MODEL), lambda i, f: (0, 0)))
        args.append(final_gain.reshape(1, D_MODEL))
    out_specs = [pl.BlockSpec((tm, D_MODEL), lambda i, f: (i, 0))]
    out_shape = [jax.ShapeDtypeStruct((rows, D_MODEL), F32)]
    if cast:
        src_in, src_out, lc = cast_job
        in_blk = (D_MODEL // nt, 2 * D_FF // nf)
        out_blk = (D_FF // nf, D_MODEL // nt)
        in_specs += [pl.BlockSpec((None,) + in_blk, lambda i, f: (lc, i, f)),
                     pl.BlockSpec((None,) + out_blk, lambda i, f: (lc, f, i))]
        args += [src_in, src_out]
        out_specs += [pl.BlockSpec(in_blk, lambda i, f: (i, f)), pl.BlockSpec(out_blk, lambda i, f: (f, i))]
        out_shape += [jax.ShapeDtypeStruct((D_MODEL, 2 * D_FF), BF16), jax.ShapeDtypeStruct((D_FF, D_MODEL), BF16)]
    res = pl.pallas_call(
        functools.partial(_ffn_kernel, final=final, cast=cast),
        grid=(nt, nf),
        in_specs=in_specs,
        out_specs=out_specs,
        out_shape=out_shape,
        scratch_shapes=[pltpu.VMEM((tm, D_MODEL), BF16)],
        compiler_params=_cparams(("arbitrary", "arbitrary")),
        name="ffn_final" if final else "ffn",
    )(*args)
    return res if cast else res[0]


N_HEAD_TILES = COL_NQ // TN_IN


def _inproj_kernel(x_ref, sh_ref, sc_ref, ng_ref, wa_ref, wb_ref, wgt_ref, bgt_ref, p_ref, gt_ref, h_scr):
    j = pl.program_id(1)

    @pl.when(j == 0)
    def _():
        h = _adaln(x_ref[...], ng_ref[...], sh_ref[...], sc_ref[...]).astype(BF16)
        h_scr[...] = h
        for s in range(TM_IN // M_CHUNK):
            gt_ref[s] = _dot_nt(wgt_ref[...], h[s * M_CHUNK:(s + 1) * M_CHUNK, :]) + bgt_ref[...]

    @pl.when(j < N_HEAD_TILES)
    def _():
        p_ref[...] = _dot(h_scr[...], wa_ref[...]).astype(p_ref.dtype)

    @pl.when(j >= N_HEAD_TILES)
    def _():
        p_ref[...] = _dot(h_scr[...], wb_ref[...]).astype(p_ref.dtype)


def _inproj(x, li, mod, norm_g, wa, wb, wgt, bgt, row_fn):
    rows = x.shape[0]
    tm = TM_IN
    row_of_tile = row_fn(tm)
    return pl.pallas_call(
        _inproj_kernel,
        grid=(rows // tm, P_COLS // TN_IN),
        in_specs=[
            pl.BlockSpec((tm, D_MODEL), lambda i, j: (i, 0)),
            _mod_spec(li, 3, row_of_tile), _mod_spec(li, 4, row_of_tile),
            pl.BlockSpec((1, D_MODEL), lambda i, j: (0, 0)),
            pl.BlockSpec((None, D_MODEL, TN_IN), lambda i, j: (li, 0, jnp.minimum(j, N_HEAD_TILES - 1))),
            pl.BlockSpec((None, D_MODEL, TN_IN), lambda i, j: (li, 0, jnp.maximum(j - N_HEAD_TILES, 0))),
            pl.BlockSpec((M_HEADS * GATE_R, D_MODEL), lambda i, j: (0, 0)),
            pl.BlockSpec((M_HEADS * GATE_R, 1), lambda i, j: (0, 0)),
        ],
        out_specs=[
            pl.BlockSpec((tm, TN_IN), lambda i, j: (i, j)),
            pl.BlockSpec((tm // M_CHUNK, M_HEADS * GATE_R, M_CHUNK), lambda i, j: (i, 0, 0)),
        ],
        out_shape=[
            jax.ShapeDtypeStruct((rows, P_COLS), BF16),
            jax.ShapeDtypeStruct((rows // M_CHUNK, M_HEADS * GATE_R, M_CHUNK), F32),
        ],
        scratch_shapes=[pltpu.VMEM((tm, D_MODEL), BF16)],
        compiler_params=_cparams(("arbitrary", "arbitrary")),
        name="inproj",
    )(x, mod, mod, norm_g.reshape(1, D_MODEL), wa, wb, wgt, bgt)


def _rope(u, cs, sn):
    halves = []
    for j in range(M_HEAD_DIM // LANES):
        sl = slice(j * LANES, (j + 1) * LANES)
        uj = u[:, sl]
        halves.append(uj * cs[:, sl] + pltpu.roll(uj, LANES // 2, 1) * sn[:, sl])
    return jnp.concatenate(halves, axis=1)


def _mlstm_chunk(q, k, v, i_col, cum_col, i_row, cum_row, mask, last, c_mem, n_mem, m_prev):
    logw = jnp.where(mask, cum_col - cum_row + i_row, -jnp.inf)
    m_row = jnp.maximum(cum_col + m_prev, jnp.max(logw, axis=-1, keepdims=True))
    m_new = m_row[last:last + 1, :]
    cum_last = cum_col[last:last + 1, :]
    w_state = jnp.exp(cum_last - cum_col + i_col - m_new)
    decay = jnp.exp(cum_last + m_prev - m_new)
    inter = jnp.exp(cum_col + m_prev - m_row)
    kf = k.astype(F32)
    sc = _dot_nt(q, k) * jnp.exp(logw - m_row)
    num = inter * _dot_nt(q, c_mem.astype(BF16)) + _dot(sc.astype(BF16), v)
    den = inter * jnp.sum(q.astype(F32) * n_mem, axis=-1, keepdims=True) + jnp.sum(sc, axis=-1, keepdims=True)
    hid = num / jnp.maximum(jnp.abs(den), jnp.exp(-m_row))
    c_new = decay * c_mem + _dot_tn((w_state * v.astype(F32)).astype(BF16), k)
    n_new = decay * n_mem + jnp.sum(w_state * kf, axis=0, keepdims=True)
    return hid, c_new, n_new, m_new


def _mlstm_kernel(ql, kl, vl, ol, qc, kc, vc, oc, gtl, gtc, cos, sin, gn, yl, yc,
                  q_scr, k_scr, hf_scr, hb_scr, grow_scr, gcol_scr):
    ch = M_CHUNK
    kscale = M_HEAD_DIM ** -0.5

    def rows(c):
        return slice(c * ch, (c + 1) * ch)

    t_idx = lax.broadcasted_iota(jnp.int32, (ch, ch), 0)
    s_idx = lax.broadcasted_iota(jnp.int32, (ch, ch), 1)
    lower = s_idx <= t_idx
    upper = s_idx >= t_idx
    lower_b = lower.astype(BF16)
    upper_b = upper.astype(BF16)
    gate_row = lax.broadcasted_iota(jnp.int32, (GATE_R, ch), 0)
    lane_pad = jnp.zeros((LANES - GATE_R, ch), F32)

    for c in range(N_CHUNKS):
        if c == 0:
            q_scr[rows(0), :] = qc[...].astype(BF16)
            k_scr[rows(0), :] = (kc[...].astype(F32) * kscale).astype(BF16)
        else:
            lat = rows(c - 1)
            q_scr[rows(c), :] = _rope(ql[lat, :].astype(F32), cos[lat, :], sin[lat, :]).astype(BF16)
            k_scr[rows(c), :] = (_rope(kl[lat, :].astype(F32), cos[lat, :], sin[lat, :]) * kscale).astype(BF16)
        gt = gtc[0] if c == 0 else gtl[c - 1]
        lf = _log_sigmoid(gt)
        terms = jnp.where(gate_row == 1, _dot_split(lf, upper_b), jnp.where(gate_row == 3, _dot_split(lf, lower_b), gt))
        grow_scr[c] = terms
        gcol_scr[rows(c), :] = jnp.concatenate([terms, lane_pad], axis=0).T

    def values(c):
        return vc[...].astype(BF16) if c == 0 else vl[rows(c - 1), :].astype(BF16)

    def init():
        return (jnp.zeros((M_HEAD_DIM, M_HEAD_DIM), F32), jnp.zeros((1, M_HEAD_DIM), F32),
                jnp.full((1, 1), -jnp.inf, F32))

    def run(c, d, mask, last, state):
        g_rows = grow_scr[c]
        g_cols = gcol_scr[rows(c), :]
        i0, f0 = 2 * d, 2 * d + 1
        return _mlstm_chunk(q_scr[rows(c), :], k_scr[rows(c), :], values(c), g_cols[:, i0:i0 + 1],
                            g_cols[:, f0:f0 + 1], g_rows[i0:i0 + 1, :], g_rows[f0:f0 + 1, :], mask, last, *state)

    st_f, st_b = init(), init()
    for step in range(N_CHUNKS):
        cf = step
        cb = 0 if step == 0 else N_CHUNKS - step
        hid, *st_f = run(cf, 0, lower, ch - 1, st_f)
        hf_scr[rows(cf), :] = hid
        hid, *st_b = run(cb, 1, upper, 0, st_b)
        hb_scr[rows(cb), :] = hid

    gain = gn[...]
    for c in range(N_CHUNKS):
        hsum = hf_scr[rows(c), :] + hb_scr[rows(c), :]
        hn = hsum * lax.rsqrt(jnp.mean(hsum * hsum, axis=-1, keepdims=True) + EPS) * gain
        if c == 0:
            yc[...] = (_sigmoid(oc[...].astype(F32)) * hn).astype(yc.dtype)
        else:
            yl[rows(c - 1), :] = (_sigmoid(ol[rows(c - 1), :].astype(F32)) * hn).astype(yl.dtype)


def _mlstm(p_lat, p_ctx, gt_lat, gt_ctx, cos, sin, gn):
    hd = M_HEAD_DIM
    cb = lambda col: col // hd

    def lat(col):
        return pl.BlockSpec((SEQ, hd), lambda b, h: (b, cb(col) + h))

    def ctx(col):
        return pl.BlockSpec((CTX_LEN, hd), lambda b, h: (b, cb(col) + h))

    seq_chunks = SEQ // M_CHUNK
    ctx_chunks = CTX_LEN // M_CHUNK
    return pl.pallas_call(
        _mlstm_kernel,
        grid=(BATCH, M_HEADS),
        in_specs=[
            lat(COL_MQ), lat(COL_MK), lat(COL_MV), lat(COL_MO),
            ctx(COL_MQ), ctx(COL_MK), ctx(COL_MV), ctx(COL_MO),
            pl.BlockSpec((seq_chunks, GATE_R, M_CHUNK), lambda b, h: (b, h, 0)),
            pl.BlockSpec((ctx_chunks, GATE_R, M_CHUNK), lambda b, h: (b, h, 0)),
            pl.BlockSpec((SEQ, hd), lambda b, h: (0, 0)),
            pl.BlockSpec((SEQ, hd), lambda b, h: (0, 0)),
            pl.BlockSpec((1, hd), lambda b, h: (0, h)),
        ],
        out_specs=[
            pl.BlockSpec((SEQ, hd), lambda b, h: (b, h)),
            pl.BlockSpec((CTX_LEN, hd), lambda b, h: (b, h)),
        ],
        out_shape=[
            jax.ShapeDtypeStruct((BATCH * SEQ, BRANCH_W), BF16),
            jax.ShapeDtypeStruct((BATCH * CTX_LEN, BRANCH_W), BF16),
        ],
        scratch_shapes=[
            pltpu.VMEM((SEQ + CTX_LEN, hd), BF16), pltpu.VMEM((SEQ + CTX_LEN, hd), BF16),
            pltpu.VMEM((SEQ + CTX_LEN, hd), F32), pltpu.VMEM((SEQ + CTX_LEN, hd), F32),
            pltpu.VMEM((N_CHUNKS, GATE_R, M_CHUNK), F32), pltpu.VMEM((SEQ + CTX_LEN, LANES), F32),
        ],
        compiler_params=_cparams(("arbitrary", "arbitrary")),
        name="mlstm",
    )(p_lat, p_lat, p_lat, p_lat, p_ctx, p_ctx, p_ctx, p_ctx, gt_lat, gt_ctx, cos, sin, gn.reshape(1, BRANCH_W))


def _rope_tables():
    pos = jnp.arange(SEQ)
    nf = M_HEAD_DIM // 4
    inv = ROPE_BASE ** (-jnp.arange(nf, dtype=F32) / nf)
    ang_r = (pos // GRID_W).astype(F32)[:, None] * inv[None, :]
    ang_c = (pos % GRID_W).astype(F32)[:, None] * inv[None, :]
    cr, sr, cc, sc = jnp.cos(ang_r), jnp.sin(ang_r), jnp.cos(ang_c), jnp.sin(ang_c)
    return (jnp.concatenate([cr, cr, cc, cc], axis=-1), jnp.concatenate([-sr, sr, -sc, sc], axis=-1))


NA_GROUP = 2
NA_ROWS = SEQ // GRID_W
NA_QROWS = 4
NA_UNION = NA_KH + NA_QROWS
NA_QT = NA_QROWS * GRID_W
NA_KT = NA_UNION * GRID_W
NA_QGROUPS = NA_ROWS // NA_QROWS
NA_TYPES = 3


def _na_union_start(g):
    return max(0, min(NA_QROWS * g - NA_KH // 2, NA_ROWS - NA_UNION))


def _na_kernel(ql, kl, vl, qc, kc, vc, bias, yl, yc):
    scale = NA_HEAD_DIM ** -0.5
    kcs = [kc[:, hh * NA_HEAD_DIM:(hh + 1) * NA_HEAD_DIM].astype(BF16) for hh in range(NA_GROUP)]
    vcs = [vc[:, hh * NA_HEAD_DIM:(hh + 1) * NA_HEAD_DIM].astype(BF16) for hh in range(NA_GROUP)]

    def group_body(g, carry):
        u = jnp.clip(NA_QROWS * g - NA_KH // 2, 0, NA_ROWS - NA_UNION)
        kind = jnp.where(g == 0, 0, jnp.where(g == NA_QGROUPS - 1, NA_TYPES - 1, 1))
        q_rows = pl.ds(pl.multiple_of(g * NA_QT, NA_QT), NA_QT)
        k_rows = pl.ds(pl.multiple_of(u * GRID_W, GRID_W), NA_KT)
        outs = []
        for hh in range(NA_GROUP):
            sl = slice(hh * NA_HEAD_DIM, (hh + 1) * NA_HEAD_DIM)
            q = ql[q_rows, sl].astype(BF16)
            kw = kl[k_rows, sl].astype(BF16)
            vw = vl[k_rows, sl].astype(BF16)
            s_win = _dot_nt(q, kw) * scale + bias[hh, kind]
            s_ctx = _dot_nt(q, kcs[hh]) * scale
            m = jnp.maximum(jnp.max(s_win, axis=-1, keepdims=True), jnp.max(s_ctx, axis=-1, keepdims=True))
            p_win = jnp.exp(s_win - m)
            p_ctx = jnp.exp(s_ctx - m)
            denom = jnp.sum(p_win, axis=-1, keepdims=True) + jnp.sum(p_ctx, axis=-1, keepdims=True)
            o = _dot(p_win.astype(BF16), vw) + _dot(p_ctx.astype(BF16), vcs[hh])
            outs.append(o / denom)
        yl[q_rows, :] = jnp.concatenate(outs, axis=1).astype(yl.dtype)
        return carry

    lax.fori_loop(0, NA_QGROUPS, group_body, 0)

    outs = []
    for hh in range(NA_GROUP):
        sl = slice(hh * NA_HEAD_DIM, (hh + 1) * NA_HEAD_DIM)
        s = _dot_nt(qc[:, sl].astype(BF16), kcs[hh]) * scale
        p = jnp.exp(s - jnp.max(s, axis=-1, keepdims=True))
        outs.append(_dot(p.astype(BF16), vcs[hh]) / jnp.sum(p, axis=-1, keepdims=True))
    yc[...] = jnp.concatenate(outs, axis=1).astype(yc.dtype)


def _na_bias_kernel(rpb_ref, o_ref):
    shape = (LANES, GRID_W * GRID_W)
    n = lax.broadcasted_iota(jnp.int32, shape, 1)
    c = lax.broadcasted_iota(jnp.int32, shape, 0)
    q = jnp.right_shift(n, GRID_W.bit_length() - 1)
    k = jnp.bitwise_and(n, GRID_W - 1)
    dc = jnp.clip(k - q, -(NA_KW - 1), NA_KW - 1) + (NA_KW - 1)
    picked = _dot_exact(rpb_ref[...], (c == dc).astype(F32))
    col_start = jnp.clip(q - NA_KW // 2, 0, GRID_W - NA_KW)
    col_ok = (k >= col_start) & (k < col_start + NA_KW)
    o_ref[...] = jnp.where(col_ok, picked, -jnp.inf)


def _na_bias(rpb):
    assert _na_union_start(1) == 0 and _na_union_start(2) == NA_QROWS
    assert GRID_W & (GRID_W - 1) == 0 and NA_HEADS * (2 * NA_KH - 1) <= LANES and 2 * NA_KW - 1 <= LANES
    n_dr = 2 * NA_KH - 1
    flat = rpb.astype(F32).reshape(NA_HEADS * n_dr, 2 * NA_KW - 1)
    flat = jnp.pad(flat, ((0, LANES - NA_HEADS * n_dr), (0, LANES - (2 * NA_KW - 1))))
    t = pl.pallas_call(
        _na_bias_kernel,
        out_shape=jax.ShapeDtypeStruct((LANES, GRID_W * GRID_W), F32),
        name="natten_bias",
    )(flat)
    t = t[:NA_HEADS * n_dr].reshape(NA_HEADS, n_dr, GRID_W, GRID_W)
    tp = jnp.pad(t, ((0, 0), (NA_UNION, NA_UNION), (0, 0), (0, 0)), constant_values=-jnp.inf)
    runs = []
    for g in (0, 1, NA_QGROUPS - 1):
        u = _na_union_start(g)
        for i in range(NA_QROWS):
            r = NA_QROWS * g + i
            start = max(0, min(r - NA_KH // 2, NA_ROWS - NA_KH))
            dr0 = u - r + NA_KH - 1 + NA_UNION
            in_window = jnp.array([start <= u + j < start + NA_KH for j in range(NA_UNION)])
            runs.append(jnp.where(in_window[None, :, None, None], tp[:, dr0:dr0 + NA_UNION], -jnp.inf))
    table = jnp.stack(runs, axis=1).reshape(NA_HEADS, NA_TYPES, NA_QROWS, NA_UNION, GRID_W, GRID_W)
    return jnp.transpose(table, (0, 1, 2, 4, 3, 5)).reshape(NA_HEADS, NA_TYPES, NA_QT, NA_KT)


def _na(p_lat, p_ctx, bias):
    gw = NA_GROUP * NA_HEAD_DIM
    cb = lambda col: col // gw

    def lat(col):
        return pl.BlockSpec((SEQ, gw), lambda b, g: (b, cb(col) + g))

    def ctx(col):
        return pl.BlockSpec((CTX_LEN, gw), lambda b, g: (b, cb(col) + g))

    return pl.pallas_call(
        _na_kernel,
        grid=(BATCH, NA_HEADS // NA_GROUP),
        in_specs=[
            lat(COL_NQ), lat(COL_NK), lat(COL_NV), ctx(COL_NQ), ctx(COL_NK), ctx(COL_NV),
            pl.BlockSpec((NA_GROUP, NA_TYPES, NA_QT, NA_KT), lambda b, g: (g, 0, 0, 0)),
        ],
        out_specs=[
            pl.BlockSpec((SEQ, gw), lambda b, g: (b, g)),
            pl.BlockSpec((CTX_LEN, gw), lambda b, g: (b, g)),
        ],
        out_shape=[
            jax.ShapeDtypeStruct((BATCH * SEQ, BRANCH_W), BF16),
            jax.ShapeDtypeStruct((BATCH * CTX_LEN, BRANCH_W), BF16),
        ],
        compiler_params=_cparams(("arbitrary", "arbitrary")),
        name="natten",
    )(p_lat, p_lat, p_lat, p_ctx, p_ctx, p_ctx, bias)


LRU_HALO = 2 * SUBLANES


def _lru_kernel(xl, xc, cw, cb, wa, ba, wx, bx, lam, hfl, hbl, hfc, hbc,
                ext_f, ext_b, a_f, u_f, a_b, u_b, carry):
    s = pl.program_id(1)
    ch = LRU_CHUNK
    n_lat = SEQ // ch
    halo = LRU_HALO
    zeros_halo = jnp.zeros((halo, BRANCH_W), F32)

    def fill(ext, chunk):
        @pl.when(s == 0)
        def _():
            ext[0:halo, :] = zeros_halo
            ext[halo:halo + ch, :] = xc[...].astype(F32)
            ext[halo + ch:, :] = zeros_halo

        @pl.when(s > 0)
        def _():
            base = pl.multiple_of(chunk * ch, ch)
            ext[halo:halo + ch, :] = xl[pl.ds(base, ch), :].astype(F32)
            prev = xl[pl.ds(pl.multiple_of(jnp.maximum(base - halo, 0), halo), halo), :].astype(F32)
            nxt = xl[pl.ds(pl.multiple_of(jnp.minimum(base + ch, SEQ - halo), halo), halo), :].astype(F32)
            ext[0:halo, :] = jnp.where(chunk > 0, prev, 0.0)
            ext[halo + ch:, :] = jnp.where(chunk < n_lat - 1, nxt, 0.0)

    chunk_f = jnp.maximum(s - 1, 0)
    chunk_b = jnp.maximum(n_lat - s, 0)
    fill(ext_f, chunk_f)
    fill(ext_b, chunk_b)

    def coeffs(ext, d, a_out, u_out):
        xconv = cb[...] + ext[halo - 1:halo - 1 + ch, :] * cw[0:1, :]
        for j in range(1, LRU_CONV):
            xconv = xconv + ext[halo - 1 + j:halo - 1 + j + ch, :] * cw[j:j + 1, :]
        sp = _softplus(-lam[d:d + 1, :])
        for g in range(LRU_BLOCKS):
            sl = slice(g * LRU_BLOCK_DIM, (g + 1) * LRU_BLOCK_DIM)
            xg = xconv[:, sl]
            xb = xg.astype(BF16)
            r = _sigmoid(_dot(xb, wa[d, g]) + ba[d:d + 1, sl])
            i = _sigmoid(_dot(xb, wx[d, g]) + bx[d:d + 1, sl])
            log_a = -LRU_C * r * sp[:, sl]
            a = jnp.exp(log_a)
            a_out[:, sl] = a
            u_out[:, sl] = jnp.sqrt(jnp.tanh(-log_a) * (a * a + 1.0)) * i * xg

    coeffs(ext_f, 0, a_f, u_f)
    coeffs(ext_b, 1, a_b, u_b)

    @pl.when(s == 0)
    def _():
        carry[...] = jnp.zeros_like(carry)

    row = lax.broadcasted_iota(jnp.int32, (SUBLANES, BRANCH_W), 0)
    n_tiles = ch // SUBLANES

    def scan_tiles(hf_out, hb_out):
        def body(t, hs):
            h_f, h_b = hs
            r0 = pl.multiple_of(t * SUBLANES, SUBLANES)
            a = a_f[pl.ds(r0, SUBLANES), :]
            u = u_f[pl.ds(r0, SUBLANES), :]
            for d in (1, 2, 4):
                ok = row >= d
                u = jnp.where(ok, a * pltpu.roll(u, d, 0) + u, u)
                a = jnp.where(ok, a * pltpu.roll(a, d, 0), a)
            h = u + a * h_f
            hf_out[pl.ds(r0, SUBLANES), :] = h
            h_f = h[SUBLANES - 1:SUBLANES, :]
            r1 = pl.multiple_of((n_tiles - 1 - t) * SUBLANES, SUBLANES)
            a = a_b[pl.ds(r1, SUBLANES), :]
            u = u_b[pl.ds(r1, SUBLANES), :]
            for d in (1, 2, 4):
                ok = row < SUBLANES - d
                u = jnp.where(ok, a * pltpu.roll(u, SUBLANES - d, 0) + u, u)
                a = jnp.where(ok, a * pltpu.roll(a, SUBLANES - d, 0), a)
            h = u + a * h_b
            hb_out[pl.ds(r1, SUBLANES), :] = h
            h_b = h[0:1, :]
            return h_f, h_b

        h_f, h_b = lax.fori_loop(0, n_tiles, body, (carry[0:1, :], carry[1:2, :]))
        carry[0:1, :] = h_f
        carry[1:2, :] = h_b

    @pl.when(s == 0)
    def _():
        scan_tiles(hfc, hbc)

    @pl.when(s > 0)
    def _():
        scan_tiles(hfl, hbl)


def _lru(p_lat, p_ctx, conv_w, conv_b, w_a, b_a, w_x, b_x, lam):
    ch = LRU_CHUNK
    n_lat = SEQ // ch
    full = lambda shape: pl.BlockSpec(shape, lambda b, s: (0,) * len(shape))
    wblk = (2, LRU_BLOCKS, LRU_BLOCK_DIM, LRU_BLOCK_DIM)
    return pl.pallas_call(
        _lru_kernel,
        grid=(BATCH, LRU_STEPS),
        in_specs=[
            pl.BlockSpec((SEQ, BRANCH_W), lambda b, s: (b, COL_LX // BRANCH_W)),
            pl.BlockSpec((CTX_LEN, BRANCH_W), lambda b, s: (b, COL_LX // BRANCH_W)),
            full((LRU_CONV, BRANCH_W)), full((1, BRANCH_W)),
            full(wblk), full((2, BRANCH_W)), full(wblk), full((2, BRANCH_W)), full((2, BRANCH_W)),
        ],
        out_specs=[
            pl.BlockSpec((ch, BRANCH_W), lambda b, s: (b * n_lat + jnp.maximum(s - 1, 0), 0)),
            pl.BlockSpec((ch, BRANCH_W), lambda b, s: (b * n_lat + jnp.minimum(n_lat - s, n_lat - 1), 0)),
            pl.BlockSpec((CTX_LEN, BRANCH_W), lambda b, s: (b, 0)),
            pl.BlockSpec((CTX_LEN, BRANCH_W), lambda b, s: (b, 0)),
        ],
        out_shape=[
            jax.ShapeDtypeStruct((BATCH * SEQ, BRANCH_W), F32),
            jax.ShapeDtypeStruct((BATCH * SEQ, BRANCH_W), F32),
            jax.ShapeDtypeStruct((BATCH * CTX_LEN, BRANCH_W), F32),
            jax.ShapeDtypeStruct((BATCH * CTX_LEN, BRANCH_W), F32),
        ],
        scratch_shapes=[
            pltpu.VMEM((ch + 2 * LRU_HALO, BRANCH_W), F32), pltpu.VMEM((ch + 2 * LRU_HALO, BRANCH_W), F32),
            pltpu.VMEM((ch, BRANCH_W), F32), pltpu.VMEM((ch, BRANCH_W), F32),
            pltpu.VMEM((ch, BRANCH_W), F32), pltpu.VMEM((ch, BRANCH_W), F32),
            pltpu.VMEM((SUBLANES, BRANCH_W), F32),
        ],
        compiler_params=_cparams(("arbitrary", "arbitrary")),
        name="rglru",
    )(p_lat, p_ctx, conv_w, conv_b.reshape(1, BRANCH_W), w_a, b_a, w_x, b_x, lam)


def _merge_kernel(ym, yn, hf, hb, lg, g0, g1, g2, wb, o_ref, yl_s):
    j = pl.program_id(1)

    @pl.when(j == 0)
    def _():
        yl_s[...] = ((hf[...] + hb[...]) * jax.nn.gelu(lg[...].astype(F32))).astype(BF16)

    acc = _sigmoid(g0[...].astype(F32)) * _dot(ym[...], wb[0])
    acc = acc + _sigmoid(g1[...].astype(F32)) * _dot(yn[...], wb[1])
    acc = acc + _sigmoid(g2[...].astype(F32)) * _dot(yl_s[...], wb[2])
    o_ref[...] = acc.astype(BF16)


def _merge(ym, yn, hf, hb, p, li, w_branch):
    rows = ym.shape[0]
    tn = TN_MERGE
    tm = TM_MERGE
    br = lambda: pl.BlockSpec((tm, BRANCH_W), lambda i, j: (i, 0))

    def gate(n):
        return pl.BlockSpec((tm, tn), lambda i, j: (i, (COL_GX + n * D_MODEL) // tn + j))

    return pl.pallas_call(
        _merge_kernel,
        grid=(rows // tm, D_MODEL // tn),
        in_specs=[
            br(), br(), br(), br(),
            pl.BlockSpec((tm, BRANCH_W), lambda i, j: (i, COL_LG // BRANCH_W)),
            gate(0), gate(1), gate(2),
            pl.BlockSpec((None, N_BRANCH, BRANCH_W, tn), lambda i, j: (li, 0, 0, j)),
        ],
        out_specs=pl.BlockSpec((tm, tn), lambda i, j: (i, j)),
        out_shape=jax.ShapeDtypeStruct((rows, D_MODEL), BF16),
        scratch_shapes=[pltpu.VMEM((tm, BRANCH_W), BF16)],
        compiler_params=_cparams(("arbitrary", "arbitrary")),
        name="merge",
    )(ym, yn, hf, hb, p, p, p, p, w_branch)


def _outproj_kernel(s_ref, w_ref, x_ref, g_ref, o_ref):
    o_ref[...] = x_ref[...] + g_ref[...] * _dot(s_ref[...], w_ref[...])


def _outproj(s, li, w_out, x, mod, row_fn):
    rows = x.shape[0]
    tn = TN_OUT
    tm = TM_OUT
    row_of_tile = row_fn(tm)
    return pl.pallas_call(
        _outproj_kernel,
        grid=(rows // tm, D_MODEL // tn),
        in_specs=[
            pl.BlockSpec((tm, D_MODEL), lambda i, j: (i, 0)),
            pl.BlockSpec((None, D_MODEL, tn), lambda i, j: (li, 0, j)),
            pl.BlockSpec((tm, tn), lambda i, j: (i, j)),
            pl.BlockSpec((None, None, None, 1, tn), lambda i, j: (li, row_of_tile(i), 5, 0, j)),
        ],
        out_specs=pl.BlockSpec((tm, tn), lambda i, j: (i, j)),
        out_shape=jax.ShapeDtypeStruct((rows, D_MODEL), F32),
        compiler_params=_cparams(("arbitrary", "arbitrary")),
        name="outproj",
    )(s, w_out, x, mod)


def _split_w_in(w):
    n_gate = 4 * M_HEADS
    o_mg = COL_NQ
    wbf = w.astype(BF16)
    wa = wbf[:, :, :o_mg]
    wb = wbf[:, :, o_mg + n_gate:]
    mg = w[:, :, o_mg:o_mg + n_gate].reshape(DEPTH, D_MODEL, 2, 2, M_HEADS)
    mg = jnp.transpose(mg, (0, 4, 2, 3, 1)).reshape(DEPTH, M_HEADS, 4, D_MODEL)
    wgt = jnp.pad(mg, ((0, 0), (0, 0), (0, GATE_R - 4), (0, 0)))
    return wa, wb, wgt.reshape(DEPTH, M_HEADS * GATE_R, D_MODEL).astype(BF16)


def _gate_bias(b_i, b_f):
    b = jnp.stack([b_i, b_f], axis=1)
    b = jnp.transpose(b, (2, 0, 1)).reshape(M_HEADS, 4).astype(F32)
    return jnp.pad(b, ((0, 0), (0, GATE_R - 4))).reshape(M_HEADS * GATE_R, 1)


def kernel(x, c, ctx, c_ctx, w_mod, b_mod, norm_ffn1, norm_mix, norm_ffn2, ffn1_w_in, ffn1_w_out, ffn2_w_in,
           ffn2_w_out, w_in, mlstm_b_i, mlstm_b_f, mlstm_gn, na_rpb, lru_conv_w, lru_conv_b, lru_w_a, lru_b_a,
           lru_w_x, lru_b_x, lru_lambda, w_branch, w_out, norm_final):
    assert x.shape == (BATCH, SEQ, D_MODEL) and ctx.shape == (BATCH, CTX_LEN, D_MODEL)
    xl = x.reshape(BATCH * SEQ, D_MODEL)
    xc = ctx.reshape(BATCH * CTX_LEN, D_MODEL)
    c_rows = jnp.concatenate([c, c_ctx[None, :], jnp.zeros((MOD_ROWS - BATCH - 1, D_MODEL), F32)], axis=0)
    mod = _modulation(c_rows, w_mod, b_mod).reshape(DEPTH, MOD_ROWS, N_MOD, 1, D_MODEL)
    cos, sin = _rope_tables()
    w1_in, w1_out = ffn1_w_in[0].astype(BF16), ffn1_w_out[0].astype(BF16)
    w_head, w_tail, w_gate = _split_w_in(w_in)
    wb = w_branch.astype(BF16)
    wo = w_out.astype(BF16)

    for li in range(DEPTH):
        last = li == DEPTH - 1
        wgt = w_gate[li]
        bgt = _gate_bias(mlstm_b_i[li], mlstm_b_f[li])
        bias = _na_bias(na_rpb[li])
        wa, wx = lru_w_a[li].astype(BF16), lru_w_x[li].astype(BF16)

        xl, w2_in, w2_out = _ffn(xl, li, mod, 0, norm_ffn1[li], w1_in, w1_out, _lat_row,
                                 cast_job=(ffn2_w_in, ffn2_w_out, li))
        xc = _ffn(xc, li, mod, 0, norm_ffn1[li], w1_in, w1_out, _ctx_row)

        p_lat, gt_lat = _inproj(xl, li, mod, norm_mix[li], w_head, w_tail, wgt, bgt, _lat_row)
        p_ctx, gt_ctx = _inproj(xc, li, mod, norm_mix[li], w_head, w_tail, wgt, bgt, _ctx_row)

        ym_l, ym_c = _mlstm(p_lat, p_ctx, gt_lat, gt_ctx, cos, sin, mlstm_gn[li])
        yn_l, yn_c = _na(p_lat, p_ctx, bias)
        hf_l, hb_l, hf_c, hb_c = _lru(p_lat, p_ctx, lru_conv_w[li], lru_conv_b[li], wa, lru_b_a[li], wx,
                                      lru_b_x[li], lru_lambda[li])

        s_l = _merge(ym_l, yn_l, hf_l, hb_l, p_lat, li, wb)
        xl = _outproj(s_l, li, wo, xl, mod, _lat_row)
        if last:
            xl = _ffn(xl, li, mod, 6, norm_ffn2[li], w2_in, w2_out, _lat_row, norm_final)
        else:
            xl, w1_in, w1_out = _ffn(xl, li, mod, 6, norm_ffn2[li], w2_in, w2_out, _lat_row,
                                     cast_job=(ffn1_w_in, ffn1_w_out, li + 1))
            s_c = _merge(ym_c, yn_c, hf_c, hb_c, p_ctx, li, wb)
            xc = _outproj(s_c, li, wo, xc, mod, _ctx_row)
            xc = _ffn(xc, li, mod, 6, norm_ffn2[li], w2_in, w2_out, _ctx_row)
    return xl.reshape(BATCH, SEQ, D_MODEL)
```

```python
import functools

import jax
import jax.numpy as jnp
from jax import lax
from jax.experimental import pallas as pl
from jax.experimental.pallas import tpu as pltpu

F32 = jnp.float32
BF16 = jnp.bfloat16

D_MODEL = 2048
BATCH = 4
SEQ = 2048
DEPTH = 2
GRID_W = 64
CTX_LEN = 256
EPS = 1e-6
D_FF = (11 * D_MODEL) // 4
N_MOD = 9
BRANCH_W = D_MODEL // 2
N_BRANCH = 3
M_HEADS = 4
M_HEAD_DIM = BRANCH_W // M_HEADS
ROPE_BASE = 10000.0
NA_HEADS = 8
NA_HEAD_DIM = BRANCH_W // NA_HEADS
NA_KH = 8
NA_KW = 16
LRU_BLOCKS = 8
LRU_BLOCK_DIM = BRANCH_W // LRU_BLOCKS
LRU_CONV = 4
LRU_C = 8.0

SUBLANES = 8
LANES = 128

MOD_ROWS = 8
CTX_MOD_ROW = BATCH
TM_FFN = 512
TM_IN = 1024
TM_MERGE = 1024
TM_OUT = 1024
TF = 512
TN_IN = 1024
TN_MERGE = 512
TN_OUT = 512
TN_MOD = 1024
M_CHUNK = 256
N_CHUNKS = (SEQ + CTX_LEN) // M_CHUNK
LRU_CHUNK = 256
LRU_STEPS = (SEQ + CTX_LEN) // LRU_CHUNK
GATE_R = SUBLANES

COL_MQ, COL_MK, COL_MV, COL_MO = 0, BRANCH_W, 2 * BRANCH_W, 3 * BRANCH_W
COL_NQ, COL_NK, COL_NV = 4 * BRANCH_W, 5 * BRANCH_W, 6 * BRANCH_W
COL_LX, COL_LG = 7 * BRANCH_W, 8 * BRANCH_W
COL_GX = 9 * BRANCH_W
P_COLS = COL_GX + N_BRANCH * D_MODEL

VMEM_BYTES = 64 * 1024 * 1024
VMEM_LIMIT = VMEM_BYTES - 8 * 1024 * 1024


def _cparams(sem):
    return pltpu.CompilerParams(dimension_semantics=sem, vmem_limit_bytes=VMEM_LIMIT)


def _sigmoid(x):
    return jax.nn.sigmoid(x)


def _log_sigmoid(x):
    return jnp.minimum(x, 0.0) - jnp.log1p(jnp.exp(-jnp.abs(x)))


def _softplus(x):
    return jnp.maximum(x, 0.0) + jnp.log1p(jnp.exp(-jnp.abs(x)))


def _dot(a, b):
    return jnp.dot(a, b, preferred_element_type=F32)


def _dot_nt(a, b):
    return lax.dot_general(a, b, (((1,), (1,)), ((), ())), preferred_element_type=F32)


def _dot_tn(a, b):
    return lax.dot_general(a, b, (((0,), (0,)), ((), ())), preferred_element_type=F32)


def _dot_exact(a, b):
    return jnp.dot(a, b, preferred_element_type=F32, precision=lax.Precision.HIGHEST)


def _dot_split(x, m):
    hi = x.astype(BF16)
    r1 = x - hi.astype(F32)
    mid = r1.astype(BF16)
    lo = (r1 - mid.astype(F32)).astype(BF16)
    return _dot(hi, m) + _dot(mid, m) + _dot(lo, m)


def _adaln(x, gain, shift, scale):
    y = x * lax.rsqrt(jnp.mean(x * x, axis=-1, keepdims=True) + EPS)
    return (y * gain) * (1.0 + scale) + shift


ADALN_ROWS = 16


def _adaln_into(h_ref, x_ref, gain, shift, scale):
    gs = gain * (1.0 + scale)

    def body(i, carry):
        rows = pl.ds(pl.multiple_of(i * ADALN_ROWS, ADALN_ROWS), ADALN_ROWS)
        x = x_ref[rows, :]
        y = x * lax.rsqrt(jnp.mean(x * x, axis=-1, keepdims=True) + EPS)
        h_ref[rows, :] = (y * gs + shift).astype(h_ref.dtype)
        return carry

    lax.fori_loop(0, x_ref.shape[0] // ADALN_ROWS, body, 0, unroll=4)


def _mod_kernel(c_ref, w_ref, b_ref, o_ref):
    c = c_ref[...]
    act = (c * _sigmoid(c)).astype(BF16)
    o_ref[...] = _dot(act, w_ref[...].astype(BF16)) + b_ref[...]


def _modulation(c_rows, w_mod, b_mod):
    n_cols = N_MOD * D_MODEL
    return pl.pallas_call(
        _mod_kernel,
        grid=(DEPTH, n_cols // TN_MOD),
        in_specs=[
            pl.BlockSpec((MOD_ROWS, D_MODEL), lambda l, j: (0, 0)),
            pl.BlockSpec((None, D_MODEL, TN_MOD), lambda l, j: (l, 0, j)),
            pl.BlockSpec((None, 1, TN_MOD), lambda l, j: (l, 0, j)),
        ],
        out_specs=pl.BlockSpec((None, MOD_ROWS, TN_MOD), lambda l, j: (l, 0, j)),
        out_shape=jax.ShapeDtypeStruct((DEPTH, MOD_ROWS, n_cols), F32),
        compiler_params=_cparams(("arbitrary", "arbitrary")),
        name="modulation",
    )(c_rows, w_mod, b_mod.reshape(DEPTH, 1, n_cols))


def _mod_spec(li, k, row_of_tile):
    return pl.BlockSpec((None, None, None, 1, D_MODEL), lambda i, j: (li, row_of_tile(i), k, 0, 0))


def _lat_row(tm):
    return lambda i: i // (SEQ // tm)


def _ctx_row(tm):
    return lambda i: CTX_MOD_ROW


def _ffn_kernel(*refs, final, cast):
    refs = list(refs)
    x_ref, sh_ref, sc_ref, g_ref, ng_ref, wg_ref, wu_ref, wo_ref = refs[:8]
    del refs[:8]
    fg_ref = refs.pop(0) if final else None
    if cast:
        src_in, src_out, o_ref, dst_in, dst_out, h_scr = refs
        dst_in[...] = src_in[...].astype(BF16)
        dst_out[...] = src_out[...].astype(BF16)
    else:
        o_ref, h_scr = refs
    f = pl.program_id(1)

    @pl.when(f == 0)
    def _():
        _adaln_into(h_scr, x_ref, ng_ref[...], sh_ref[...], sc_ref[...])
        o_ref[...] = jnp.zeros_like(o_ref)

    h = h_scr[...]
    gte = _dot(h, wg_ref[...])
    up = _dot(h, wu_ref[...])
    act = (gte * _sigmoid(gte) * up).astype(BF16)
    o_ref[...] += _dot(act, wo_ref[...])

    @pl.when(f == pl.num_programs(1) - 1)
    def _():
        out = x_ref[...] + 0.5 * g_ref[...] * o_ref[...]
        if final:
            out = out * lax.rsqrt(jnp.mean(out * out, axis=-1, keepdims=True) + EPS) * fg_ref[...]
        o_ref[...] = out


def _ffn(x, li, mod, k0, norm_g, w_in, w_out, row_fn, final_gain=None, cast_job=None):
    rows = x.shape[0]
    nf = D_FF // TF
    tm = TM_FFN
    nt = rows // tm
    row_of_tile = row_fn(tm)
    final = final_gain is not None
    cast = cast_job is not None
    in_specs = [
        pl.BlockSpec((tm, D_MODEL), lambda i, f: (i, 0)),
        _mod_spec(li, k0, row_of_tile), _mod_spec(li, k0 + 1, row_of_tile), _mod_spec(li, k0 + 2, row_of_tile),
        pl.BlockSpec((1, D_MODEL), lambda i, f: (0, 0)),
        pl.BlockSpec((D_MODEL, TF), lambda i, f: (0, f)),
        pl.BlockSpec((D_MODEL, TF), lambda i, f: (0, f + nf)),
        pl.BlockSpec((TF, D_MODEL), lambda i, f: (f, 0)),
    ]
    args = [x, mod, mod, mod, norm_g.reshape(1, D_MODEL), w_in, w_in, w_out]
    if final:
        in_specs.append(pl.BlockSpec((1, D_MODEL), lambda i, f: (0, 0)))
        args.append(final_gain.reshape(1, D_MODEL))
    out_specs = [pl.BlockSpec((tm, D_MODEL), lambda i, f: (i, 0))]
    out_shape = [jax.ShapeDtypeStruct((rows, D_MODEL), F32)]
    if cast:
        src_in, src_out, lc = cast_job
        in_blk = (D_MODEL // nt, 2 * D_FF // nf)
        out_blk = (D_FF // nf, D_MODEL // nt)
        in_specs += [pl.BlockSpec((None,) + in_blk, lambda i, f: (lc, i, f)),
                     pl.BlockSpec((None,) + out_blk, lambda i, f: (lc, f, i))]
        args += [src_in, src_out]
        out_specs += [pl.BlockSpec(in_blk, lambda i, f: (i, f)), pl.BlockSpec(out_blk, lambda i, f: (f, i))]
        out_shape += [jax.ShapeDtypeStruct((D_MODEL, 2 * D_FF), BF16), jax.ShapeDtypeStruct((D_FF, D_MODEL), BF16)]
    res = pl.pallas_call(
        functools.partial(_ffn_kernel, final=final, cast=cast),
        grid=(nt, nf),
        in_specs=in_specs,
        out_specs=out_specs,
        out_shape=out_shape,
        scratch_shapes=[pltpu.VMEM((tm, D_MODEL), BF16)],
        compiler_params=_cparams(("arbitrary", "arbitrary")),
        name="ffn_final" if final else "ffn",
    )(*args)
    return res if cast else res[0]


N_HEAD_TILES = COL_NQ // TN_IN
LG_TILE = COL_LG // TN_IN
assert BRANCH_W == TN_IN and COL_GX == COL_LG + TN_IN


def _inproj_kernel(x_ref, sh_ref, sc_ref, ng_ref, wa_ref, wb_ref, wgt_ref, bgt_ref, p_ref, gt_ref, h_scr):
    j = pl.program_id(1)

    @pl.when(j == 0)
    def _():
        h = _adaln(x_ref[...], ng_ref[...], sh_ref[...], sc_ref[...]).astype(BF16)
        h_scr[...] = h
        for s in range(TM_IN // M_CHUNK):
            gt_ref[s] = _dot_nt(wgt_ref[...], h[s * M_CHUNK:(s + 1) * M_CHUNK, :]) + bgt_ref[...]

    @pl.when(j < N_HEAD_TILES)
    def _():
        p_ref[...] = _dot(h_scr[...], wa_ref[...]).astype(p_ref.dtype)

    @pl.when((j >= N_HEAD_TILES) & (j < LG_TILE))
    def _():
        p_ref[...] = _dot(h_scr[...], wb_ref[...]).astype(p_ref.dtype)

    @pl.when(j == LG_TILE)
    def _():
        p_ref[...] = jax.nn.gelu(_dot(h_scr[...], wb_ref[...])).astype(p_ref.dtype)

    @pl.when(j > LG_TILE)
    def _():
        p_ref[...] = _sigmoid(_dot(h_scr[...], wb_ref[...])).astype(p_ref.dtype)


def _inproj(x, li, mod, norm_g, wa, wb, wgt, bgt, row_fn):
    rows = x.shape[0]
    tm = TM_IN
    row_of_tile = row_fn(tm)
    return pl.pallas_call(
        _inproj_kernel,
        grid=(rows // tm, P_COLS // TN_IN),
        in_specs=[
            pl.BlockSpec((tm, D_MODEL), lambda i, j: (i, 0)),
            _mod_spec(li, 3, row_of_tile), _mod_spec(li, 4, row_of_tile),
            pl.BlockSpec((1, D_MODEL), lambda i, j: (0, 0)),
            pl.BlockSpec((None, D_MODEL, TN_IN), lambda i, j: (li, 0, jnp.minimum(j, N_HEAD_TILES - 1))),
            pl.BlockSpec((None, D_MODEL, TN_IN), lambda i, j: (li, 0, jnp.maximum(j - N_HEAD_TILES, 0))),
            pl.BlockSpec((M_HEADS * GATE_R, D_MODEL), lambda i, j: (0, 0)),
            pl.BlockSpec((M_HEADS * GATE_R, 1), lambda i, j: (0, 0)),
        ],
        out_specs=[
            pl.BlockSpec((tm, TN_IN), lambda i, j: (i, j)),
            pl.BlockSpec((tm // M_CHUNK, M_HEADS * GATE_R, M_CHUNK), lambda i, j: (i, 0, 0)),
        ],
        out_shape=[
            jax.ShapeDtypeStruct((rows, P_COLS), BF16),
            jax.ShapeDtypeStruct((rows // M_CHUNK, M_HEADS * GATE_R, M_CHUNK), F32),
        ],
        scratch_shapes=[pltpu.VMEM((tm, D_MODEL), BF16)],
        compiler_params=_cparams(("arbitrary", "arbitrary")),
        name="inproj",
    )(x, mod, mod, norm_g.reshape(1, D_MODEL), wa, wb, wgt, bgt)


def _rope(u, cs, sn):
    halves = []
    for j in range(M_HEAD_DIM // LANES):
        sl = slice(j * LANES, (j + 1) * LANES)
        uj = u[:, sl]
        halves.append(uj * cs[:, sl] + pltpu.roll(uj, LANES // 2, 1) * sn[:, sl])
    return jnp.concatenate(halves, axis=1)


def _mlstm_chunk(q, k, v, i_col, cum_col, i_row, cum_row, mask, last, c_mem, n_mem, m_prev):
    logw = jnp.where(mask, cum_col - cum_row + i_row, -jnp.inf)
    m_row = jnp.maximum(cum_col + m_prev, jnp.max(logw, axis=-1, keepdims=True))
    m_new = m_row[last:last + 1, :]
    cum_last = cum_col[last:last + 1, :]
    w_state = jnp.exp(cum_last - cum_col + i_col - m_new)
    decay = jnp.exp(cum_last + m_prev - m_new)
    inter = jnp.exp(cum_col + m_prev - m_row)
    kf = k.astype(F32)
    sc = _dot_nt(q, k) * jnp.exp(logw - m_row)
    num = inter * _dot_nt(q, c_mem.astype(BF16)) + _dot(sc.astype(BF16), v)
    den = inter * jnp.sum(q.astype(F32) * n_mem, axis=-1, keepdims=True) + jnp.sum(sc, axis=-1, keepdims=True)
    hid = num / jnp.maximum(jnp.abs(den), jnp.exp(-m_row))
    c_new = decay * c_mem + _dot_tn((w_state * v.astype(F32)).astype(BF16), k)
    n_new = decay * n_mem + jnp.sum(w_state * kf, axis=0, keepdims=True)
    return hid, c_new, n_new, m_new


def _mlstm_kernel(ql, kl, vl, ol, qc, kc, vc, oc, gtl, gtc, cos, sin, gn, yl, yc,
                  q_scr, k_scr, hf_scr, hb_scr, grow_scr, gcol_scr):
    ch = M_CHUNK
    kscale = M_HEAD_DIM ** -0.5

    def rows(c):
        return slice(c * ch, (c + 1) * ch)

    t_idx = lax.broadcasted_iota(jnp.int32, (ch, ch), 0)
    s_idx = lax.broadcasted_iota(jnp.int32, (ch, ch), 1)
    lower = s_idx <= t_idx
    upper = s_idx >= t_idx
    lower_b = lower.astype(BF16)
    upper_b = upper.astype(BF16)
    gate_row = lax.broadcasted_iota(jnp.int32, (GATE_R, ch), 0)
    lane_pad = jnp.zeros((LANES - GATE_R, ch), F32)

    for c in range(N_CHUNKS):
        if c == 0:
            q_scr[rows(0), :] = qc[...].astype(BF16)
            k_scr[rows(0), :] = (kc[...].astype(F32) * kscale).astype(BF16)
        else:
            lat = rows(c - 1)
            q_scr[rows(c), :] = _rope(ql[lat, :].astype(F32), cos[lat, :], sin[lat, :]).astype(BF16)
            k_scr[rows(c), :] = (_rope(kl[lat, :].astype(F32), cos[lat, :], sin[lat, :]) * kscale).astype(BF16)
        gt = gtc[0] if c == 0 else gtl[c - 1]
        lf = _log_sigmoid(gt)
        terms = jnp.where(gate_row == 1, _dot_split(lf, upper_b), jnp.where(gate_row == 3, _dot_split(lf, lower_b), gt))
        grow_scr[c] = terms
        gcol_scr[rows(c), :] = jnp.concatenate([terms, lane_pad], axis=0).T

    def values(c):
        return vc[...].astype(BF16) if c == 0 else vl[rows(c - 1), :].astype(BF16)

    def init():
        return (jnp.zeros((M_HEAD_DIM, M_HEAD_DIM), F32), jnp.zeros((1, M_HEAD_DIM), F32),
                jnp.full((1, 1), -jnp.inf, F32))

    def run(c, d, mask, last, state):
        g_rows = grow_scr[c]
        g_cols = gcol_scr[rows(c), :]
        i0, f0 = 2 * d, 2 * d + 1
        return _mlstm_chunk(q_scr[rows(c), :], k_scr[rows(c), :], values(c), g_cols[:, i0:i0 + 1],
                            g_cols[:, f0:f0 + 1], g_rows[i0:i0 + 1, :], g_rows[f0:f0 + 1, :], mask, last, *state)

    st_f, st_b = init(), init()
    for step in range(N_CHUNKS):
        cf = step
        cb = 0 if step == 0 else N_CHUNKS - step
        hid, *st_f = run(cf, 0, lower, ch - 1, st_f)
        hf_scr[rows(cf), :] = hid
        hid, *st_b = run(cb, 1, upper, 0, st_b)
        hb_scr[rows(cb), :] = hid

    gain = gn[...]
    for c in range(N_CHUNKS):
        hsum = hf_scr[rows(c), :] + hb_scr[rows(c), :]
        hn = hsum * lax.rsqrt(jnp.mean(hsum * hsum, axis=-1, keepdims=True) + EPS) * gain
        if c == 0:
            yc[...] = (_sigmoid(oc[...].astype(F32)) * hn).astype(yc.dtype)
        else:
            yl[rows(c - 1), :] = (_sigmoid(ol[rows(c - 1), :].astype(F32)) * hn).astype(yl.dtype)


def _mlstm(p_lat, p_ctx, gt_lat, gt_ctx, cos, sin, gn):
    hd = M_HEAD_DIM
    cb = lambda col: col // hd

    def lat(col):
        return pl.BlockSpec((SEQ, hd), lambda b, h: (b, cb(col) + h))

    def ctx(col):
        return pl.BlockSpec((CTX_LEN, hd), lambda b, h: (b, cb(col) + h))

    seq_chunks = SEQ // M_CHUNK
    ctx_chunks = CTX_LEN // M_CHUNK
    return pl.pallas_call(
        _mlstm_kernel,
        grid=(BATCH, M_HEADS),
        in_specs=[
            lat(COL_MQ), lat(COL_MK), lat(COL_MV), lat(COL_MO),
            ctx(COL_MQ), ctx(COL_MK), ctx(COL_MV), ctx(COL_MO),
            pl.BlockSpec((seq_chunks, GATE_R, M_CHUNK), lambda b, h: (b, h, 0)),
            pl.BlockSpec((ctx_chunks, GATE_R, M_CHUNK), lambda b, h: (b, h, 0)),
            pl.BlockSpec((SEQ, hd), lambda b, h: (0, 0)),
            pl.BlockSpec((SEQ, hd), lambda b, h: (0, 0)),
            pl.BlockSpec((1, hd), lambda b, h: (0, h)),
        ],
        out_specs=[
            pl.BlockSpec((SEQ, hd), lambda b, h: (b, h)),
            pl.BlockSpec((CTX_LEN, hd), lambda b, h: (b, h)),
        ],
        out_shape=[
            jax.ShapeDtypeStruct((BATCH * SEQ, BRANCH_W), BF16),
            jax.ShapeDtypeStruct((BATCH * CTX_LEN, BRANCH_W), BF16),
        ],
        scratch_shapes=[
            pltpu.VMEM((SEQ + CTX_LEN, hd), BF16), pltpu.VMEM((SEQ + CTX_LEN, hd), BF16),
            pltpu.VMEM((SEQ + CTX_LEN, hd), F32), pltpu.VMEM((SEQ + CTX_LEN, hd), F32),
            pltpu.VMEM((N_CHUNKS, GATE_R, M_CHUNK), F32), pltpu.VMEM((SEQ + CTX_LEN, LANES), F32),
        ],
        compiler_params=_cparams(("arbitrary", "arbitrary")),
        name="mlstm",
    )(p_lat, p_lat, p_lat, p_lat, p_ctx, p_ctx, p_ctx, p_ctx, gt_lat, gt_ctx, cos, sin, gn.reshape(1, BRANCH_W))


def _rope_tables():
    pos = jnp.arange(SEQ)
    nf = M_HEAD_DIM // 4
    inv = ROPE_BASE ** (-jnp.arange(nf, dtype=F32) / nf)
    ang_r = (pos // GRID_W).astype(F32)[:, None] * inv[None, :]
    ang_c = (pos % GRID_W).astype(F32)[:, None] * inv[None, :]
    cr, sr, cc, sc = jnp.cos(ang_r), jnp.sin(ang_r), jnp.cos(ang_c), jnp.sin(ang_c)
    return (jnp.concatenate([cr, cr, cc, cc], axis=-1), jnp.concatenate([-sr, sr, -sc, sc], axis=-1))


NA_GROUP = 2
NA_ROWS = SEQ // GRID_W
NA_QROWS = 4
NA_UNION = NA_KH + NA_QROWS
NA_QT = NA_QROWS * GRID_W
NA_KT = NA_UNION * GRID_W
NA_QGROUPS = NA_ROWS // NA_QROWS
NA_TYPES = 3


def _na_union_start(g):
    return max(0, min(NA_QROWS * g - NA_KH // 2, NA_ROWS - NA_UNION))


def _na_kernel(ql, kl, vl, qc, kc, vc, bias, yl, yc):
    scale = NA_HEAD_DIM ** -0.5
    kcs = [kc[:, hh * NA_HEAD_DIM:(hh + 1) * NA_HEAD_DIM].astype(BF16) for hh in range(NA_GROUP)]
    vcs = [vc[:, hh * NA_HEAD_DIM:(hh + 1) * NA_HEAD_DIM].astype(BF16) for hh in range(NA_GROUP)]

    def group_body(g, carry):
        u = jnp.clip(NA_QROWS * g - NA_KH // 2, 0, NA_ROWS - NA_UNION)
        kind = jnp.where(g == 0, 0, jnp.where(g == NA_QGROUPS - 1, NA_TYPES - 1, 1))
        q_rows = pl.ds(pl.multiple_of(g * NA_QT, NA_QT), NA_QT)
        k_rows = pl.ds(pl.multiple_of(u * GRID_W, GRID_W), NA_KT)
        outs = []
        for hh in range(NA_GROUP):
            sl = slice(hh * NA_HEAD_DIM, (hh + 1) * NA_HEAD_DIM)
            q = ql[q_rows, sl].astype(BF16)
            kw = kl[k_rows, sl].astype(BF16)
            vw = vl[k_rows, sl].astype(BF16)
            s_win = _dot_nt(q, kw) * scale + bias[hh, kind]
            s_ctx = _dot_nt(q, kcs[hh]) * scale
            m = jnp.maximum(jnp.max(s_win, axis=-1, keepdims=True), jnp.max(s_ctx, axis=-1, keepdims=True))
            p_win = jnp.exp(s_win - m)
            p_ctx = jnp.exp(s_ctx - m)
            denom = jnp.sum(p_win, axis=-1, keepdims=True) + jnp.sum(p_ctx, axis=-1, keepdims=True)
            o = _dot(p_win.astype(BF16), vw) + _dot(p_ctx.astype(BF16), vcs[hh])
            outs.append(o / denom)
        yl[q_rows, :] = jnp.concatenate(outs, axis=1).astype(yl.dtype)
        return carry

    lax.fori_loop(0, NA_QGROUPS, group_body, 0)

    outs = []
    for hh in range(NA_GROUP):
        sl = slice(hh * NA_HEAD_DIM, (hh + 1) * NA_HEAD_DIM)
        s = _dot_nt(qc[:, sl].astype(BF16), kcs[hh]) * scale
        p = jnp.exp(s - jnp.max(s, axis=-1, keepdims=True))
        outs.append(_dot(p.astype(BF16), vcs[hh]) / jnp.sum(p, axis=-1, keepdims=True))
    yc[...] = jnp.concatenate(outs, axis=1).astype(yc.dtype)


def _na_bias_kernel(rpb_ref, o_ref):
    shape = (LANES, GRID_W * GRID_W)
    n = lax.broadcasted_iota(jnp.int32, shape, 1)
    c = lax.broadcasted_iota(jnp.int32, shape, 0)
    q = jnp.right_shift(n, GRID_W.bit_length() - 1)
    k = jnp.bitwise_and(n, GRID_W - 1)
    dc = jnp.clip(k - q, -(NA_KW - 1), NA_KW - 1) + (NA_KW - 1)
    picked = _dot_exact(rpb_ref[...], (c == dc).astype(F32))
    col_start = jnp.clip(q - NA_KW // 2, 0, GRID_W - NA_KW)
    col_ok = (k >= col_start) & (k < col_start + NA_KW)
    o_ref[...] = jnp.where(col_ok, picked, -jnp.inf)


def _na_bias(rpb):
    assert _na_union_start(1) == 0 and _na_union_start(2) == NA_QROWS
    assert GRID_W & (GRID_W - 1) == 0 and NA_HEADS * (2 * NA_KH - 1) <= LANES and 2 * NA_KW - 1 <= LANES
    n_dr = 2 * NA_KH - 1
    flat = rpb.astype(F32).reshape(NA_HEADS * n_dr, 2 * NA_KW - 1)
    flat = jnp.pad(flat, ((0, LANES - NA_HEADS * n_dr), (0, LANES - (2 * NA_KW - 1))))
    t = pl.pallas_call(
        _na_bias_kernel,
        out_shape=jax.ShapeDtypeStruct((LANES, GRID_W * GRID_W), F32),
        name="natten_bias",
    )(flat)
    t = t[:NA_HEADS * n_dr].reshape(NA_HEADS, n_dr, GRID_W, GRID_W)
    tp = jnp.pad(t, ((0, 0), (NA_UNION, NA_UNION), (0, 0), (0, 0)), constant_values=-jnp.inf)
    runs = []
    for g in (0, 1, NA_QGROUPS - 1):
        u = _na_union_start(g)
        for i in range(NA_QROWS):
            r = NA_QROWS * g + i
            start = max(0, min(r - NA_KH // 2, NA_ROWS - NA_KH))
            dr0 = u - r + NA_KH - 1 + NA_UNION
            in_window = jnp.array([start <= u + j < start + NA_KH for j in range(NA_UNION)])
            runs.append(jnp.where(in_window[None, :, None, None], tp[:, dr0:dr0 + NA_UNION], -jnp.inf))
    table = jnp.stack(runs, axis=1).reshape(NA_HEADS, NA_TYPES, NA_QROWS, NA_UNION, GRID_W, GRID_W)
    return jnp.transpose(table, (0, 1, 2, 4, 3, 5)).reshape(NA_HEADS, NA_TYPES, NA_QT, NA_KT)


def _na(p_lat, p_ctx, bias):
    gw = NA_GROUP * NA_HEAD_DIM
    cb = lambda col: col // gw

    def lat(col):
        return pl.BlockSpec((SEQ, gw), lambda b, g: (b, cb(col) + g))

    def ctx(col):
        return pl.BlockSpec((CTX_LEN, gw), lambda b, g: (b, cb(col) + g))

    return pl.pallas_call(
        _na_kernel,
        grid=(BATCH, NA_HEADS // NA_GROUP),
        in_specs=[
            lat(COL_NQ), lat(COL_NK), lat(COL_NV), ctx(COL_NQ), ctx(COL_NK), ctx(COL_NV),
            pl.BlockSpec((NA_GROUP, NA_TYPES, NA_QT, NA_KT), lambda b, g: (g, 0, 0, 0)),
        ],
        out_specs=[
            pl.BlockSpec((SEQ, gw), lambda b, g: (b, g)),
            pl.BlockSpec((CTX_LEN, gw), lambda b, g: (b, g)),
        ],
        out_shape=[
            jax.ShapeDtypeStruct((BATCH * SEQ, BRANCH_W), BF16),
            jax.ShapeDtypeStruct((BATCH * CTX_LEN, BRANCH_W), BF16),
        ],
        compiler_params=_cparams(("arbitrary", "arbitrary")),
        name="natten",
    )(p_lat, p_lat, p_lat, p_ctx, p_ctx, p_ctx, bias)


LRU_HALO = 2 * SUBLANES


def _lru_kernel(xl, xc, cw, cb, wa, ba, wx, bx, lam, hfl, hbl, hfc, hbc,
                ext_f, ext_b, a_f, u_f, a_b, u_b, carry):
    s = pl.program_id(1)
    ch = LRU_CHUNK
    n_lat = SEQ // ch
    halo = LRU_HALO
    zeros_halo = jnp.zeros((halo, BRANCH_W), F32)

    def fill(ext, chunk):
        @pl.when(s == 0)
        def _():
            ext[0:halo, :] = zeros_halo
            ext[halo:halo + ch, :] = xc[...].astype(F32)
            ext[halo + ch:, :] = zeros_halo

        @pl.when(s > 0)
        def _():
            base = pl.multiple_of(chunk * ch, ch)
            ext[halo:halo + ch, :] = xl[pl.ds(base, ch), :].astype(F32)
            prev = xl[pl.ds(pl.multiple_of(jnp.maximum(base - halo, 0), halo), halo), :].astype(F32)
            nxt = xl[pl.ds(pl.multiple_of(jnp.minimum(base + ch, SEQ - halo), halo), halo), :].astype(F32)
            ext[0:halo, :] = jnp.where(chunk > 0, prev, 0.0)
            ext[halo + ch:, :] = jnp.where(chunk < n_lat - 1, nxt, 0.0)

    chunk_f = jnp.maximum(s - 1, 0)
    chunk_b = jnp.maximum(n_lat - s, 0)
    fill(ext_f, chunk_f)
    fill(ext_b, chunk_b)

    def coeffs(ext, d, a_out, u_out):
        xconv = cb[...] + ext[halo - 1:halo - 1 + ch, :] * cw[0:1, :]
        for j in range(1, LRU_CONV):
            xconv = xconv + ext[halo - 1 + j:halo - 1 + j + ch, :] * cw[j:j + 1, :]
        sp = _softplus(-lam[d:d + 1, :])
        for g in range(LRU_BLOCKS):
            sl = slice(g * LRU_BLOCK_DIM, (g + 1) * LRU_BLOCK_DIM)
            xg = xconv[:, sl]
            xb = xg.astype(BF16)
            r = _sigmoid(_dot(xb, wa[d, g]) + ba[d:d + 1, sl])
            i = _sigmoid(_dot(xb, wx[d, g]) + bx[d:d + 1, sl])
            log_a = -LRU_C * r * sp[:, sl]
            a = jnp.exp(log_a)
            a_out[:, sl] = a
            u_out[:, sl] = jnp.sqrt(jnp.tanh(-log_a) * (a * a + 1.0)) * i * xg

    coeffs(ext_f, 0, a_f, u_f)
    coeffs(ext_b, 1, a_b, u_b)

    @pl.when(s == 0)
    def _():
        carry[...] = jnp.zeros_like(carry)

    row = lax.broadcasted_iota(jnp.int32, (SUBLANES, BRANCH_W), 0)
    n_tiles = ch // SUBLANES

    def scan_tiles(hf_out, hb_out):
        def body(t, hs):
            h_f, h_b = hs
            r0 = pl.multiple_of(t * SUBLANES, SUBLANES)
            a = a_f[pl.ds(r0, SUBLANES), :]
            u = u_f[pl.ds(r0, SUBLANES), :]
            for d in (1, 2, 4):
                ok = row >= d
                u = jnp.where(ok, a * pltpu.roll(u, d, 0) + u, u)
                a = jnp.where(ok, a * pltpu.roll(a, d, 0), a)
            h = u + a * h_f
            hf_out[pl.ds(r0, SUBLANES), :] = h
            h_f = h[SUBLANES - 1:SUBLANES, :]
            r1 = pl.multiple_of((n_tiles - 1 - t) * SUBLANES, SUBLANES)
            a = a_b[pl.ds(r1, SUBLANES), :]
            u = u_b[pl.ds(r1, SUBLANES), :]
            for d in (1, 2, 4):
                ok = row < SUBLANES - d
                u = jnp.where(ok, a * pltpu.roll(u, SUBLANES - d, 0) + u, u)
                a = jnp.where(ok, a * pltpu.roll(a, SUBLANES - d, 0), a)
            h = u + a * h_b
            hb_out[pl.ds(r1, SUBLANES), :] = h
            h_b = h[0:1, :]
            return h_f, h_b

        h_f, h_b = lax.fori_loop(0, n_tiles, body, (carry[0:1, :], carry[1:2, :]))
        carry[0:1, :] = h_f
        carry[1:2, :] = h_b

    @pl.when(s == 0)
    def _():
        scan_tiles(hfc, hbc)

    @pl.when(s > 0)
    def _():
        scan_tiles(hfl, hbl)


def _lru(p_lat, p_ctx, conv_w, conv_b, w_a, b_a, w_x, b_x, lam):
    ch = LRU_CHUNK
    n_lat = SEQ // ch
    full = lambda shape: pl.BlockSpec(shape, lambda b, s: (0,) * len(shape))
    wblk = (2, LRU_BLOCKS, LRU_BLOCK_DIM, LRU_BLOCK_DIM)
    return pl.pallas_call(
        _lru_kernel,
        grid=(BATCH, LRU_STEPS),
        in_specs=[
            pl.BlockSpec((SEQ, BRANCH_W), lambda b, s: (b, COL_LX // BRANCH_W)),
            pl.BlockSpec((CTX_LEN, BRANCH_W), lambda b, s: (b, COL_LX // BRANCH_W)),
            full((LRU_CONV, BRANCH_W)), full((1, BRANCH_W)),
            full(wblk), full((2, BRANCH_W)), full(wblk), full((2, BRANCH_W)), full((2, BRANCH_W)),
        ],
        out_specs=[
            pl.BlockSpec((ch, BRANCH_W), lambda b, s: (b * n_lat + jnp.maximum(s - 1, 0), 0)),
            pl.BlockSpec((ch, BRANCH_W), lambda b, s: (b * n_lat + jnp.minimum(n_lat - s, n_lat - 1), 0)),
            pl.BlockSpec((CTX_LEN, BRANCH_W), lambda b, s: (b, 0)),
            pl.BlockSpec((CTX_LEN, BRANCH_W), lambda b, s: (b, 0)),
        ],
        out_shape=[
            jax.ShapeDtypeStruct((BATCH * SEQ, BRANCH_W), F32),
            jax.ShapeDtypeStruct((BATCH * SEQ, BRANCH_W), F32),
            jax.ShapeDtypeStruct((BATCH * CTX_LEN, BRANCH_W), F32),
            jax.ShapeDtypeStruct((BATCH * CTX_LEN, BRANCH_W), F32),
        ],
        scratch_shapes=[
            pltpu.VMEM((ch + 2 * LRU_HALO, BRANCH_W), F32), pltpu.VMEM((ch + 2 * LRU_HALO, BRANCH_W), F32),
            pltpu.VMEM((ch, BRANCH_W), F32), pltpu.VMEM((ch, BRANCH_W), F32),
            pltpu.VMEM((ch, BRANCH_W), F32), pltpu.VMEM((ch, BRANCH_W), F32),
            pltpu.VMEM((SUBLANES, BRANCH_W), F32),
        ],
        compiler_params=_cparams(("arbitrary", "arbitrary")),
        name="rglru",
    )(p_lat, p_ctx, conv_w, conv_b.reshape(1, BRANCH_W), w_a, b_a, w_x, b_x, lam)


def _merge_kernel(ym, yn, hf, hb, lg, g0, g1, g2, wb, o_ref, yl_s):
    j = pl.program_id(1)

    @pl.when(j == 0)
    def _():
        yl_s[...] = ((hf[...] + hb[...]) * lg[...].astype(F32)).astype(BF16)

    acc = g0[...].astype(F32) * _dot(ym[...], wb[0])
    acc = acc + g1[...].astype(F32) * _dot(yn[...], wb[1])
    acc = acc + g2[...].astype(F32) * _dot(yl_s[...], wb[2])
    o_ref[...] = acc.astype(BF16)


def _merge(ym, yn, hf, hb, p, li, w_branch):
    rows = ym.shape[0]
    tn = TN_MERGE
    tm = TM_MERGE
    br = lambda: pl.BlockSpec((tm, BRANCH_W), lambda i, j: (i, 0))

    def gate(n):
        return pl.BlockSpec((tm, tn), lambda i, j: (i, (COL_GX + n * D_MODEL) // tn + j))

    return pl.pallas_call(
        _merge_kernel,
        grid=(rows // tm, D_MODEL // tn),
        in_specs=[
            br(), br(), br(), br(),
            pl.BlockSpec((tm, BRANCH_W), lambda i, j: (i, COL_LG // BRANCH_W)),
            gate(0), gate(1), gate(2),
            pl.BlockSpec((None, N_BRANCH, BRANCH_W, tn), lambda i, j: (li, 0, 0, j)),
        ],
        out_specs=pl.BlockSpec((tm, tn), lambda i, j: (i, j)),
        out_shape=jax.ShapeDtypeStruct((rows, D_MODEL), BF16),
        scratch_shapes=[pltpu.VMEM((tm, BRANCH_W), BF16)],
        compiler_params=_cparams(("arbitrary", "arbitrary")),
        name="merge",
    )(ym, yn, hf, hb, p, p, p, p, w_branch)


def _outproj_kernel(s_ref, w_ref, x_ref, g_ref, o_ref):
    o_ref[...] = x_ref[...] + g_ref[...] * _dot(s_ref[...], w_ref[...])


def _outproj(s, li, w_out, x, mod, row_fn):
    rows = x.shape[0]
    tn = TN_OUT
    tm = TM_OUT
    row_of_tile = row_fn(tm)
    return pl.pallas_call(
        _outproj_kernel,
        grid=(rows // tm, D_MODEL // tn),
        in_specs=[
            pl.BlockSpec((tm, D_MODEL), lambda i, j: (i, 0)),
            pl.BlockSpec((None, D_MODEL, tn), lambda i, j: (li, 0, j)),
            pl.BlockSpec((tm, tn), lambda i, j: (i, j)),
            pl.BlockSpec((None, None, None, 1, tn), lambda i, j: (li, row_of_tile(i), 5, 0, j)),
        ],
        out_specs=pl.BlockSpec((tm, tn), lambda i, j: (i, j)),
        out_shape=jax.ShapeDtypeStruct((rows, D_MODEL), F32),
        compiler_params=_cparams(("arbitrary", "arbitrary")),
        name="outproj",
    )(s, w_out, x, mod)


def _split_w_in(w):
    n_gate = 4 * M_HEADS
    o_mg = COL_NQ
    wbf = w.astype(BF16)
    wa = wbf[:, :, :o_mg]
    wb = wbf[:, :, o_mg + n_gate:]
    mg = w[:, :, o_mg:o_mg + n_gate].reshape(DEPTH, D_MODEL, 2, 2, M_HEADS)
    mg = jnp.transpose(mg, (0, 4, 2, 3, 1)).reshape(DEPTH, M_HEADS, 4, D_MODEL)
    wgt = jnp.pad(mg, ((0, 0), (0, 0), (0, GATE_R - 4), (0, 0)))
    return wa, wb, wgt.reshape(DEPTH, M_HEADS * GATE_R, D_MODEL).astype(BF16)


def _gate_bias(b_i, b_f):
    b = jnp.stack([b_i, b_f], axis=1)
    b = jnp.transpose(b, (2, 0, 1)).reshape(M_HEADS, 4).astype(F32)
    return jnp.pad(b, ((0, 0), (0, GATE_R - 4))).reshape(M_HEADS * GATE_R, 1)


def kernel(x, c, ctx, c_ctx, w_mod, b_mod, norm_ffn1, norm_mix, norm_ffn2, ffn1_w_in, ffn1_w_out, ffn2_w_in,
           ffn2_w_out, w_in, mlstm_b_i, mlstm_b_f, mlstm_gn, na_rpb, lru_conv_w, lru_conv_b, lru_w_a, lru_b_a,
           lru_w_x, lru_b_x, lru_lambda, w_branch, w_out, norm_final):
    assert x.shape == (BATCH, SEQ, D_MODEL) and ctx.shape == (BATCH, CTX_LEN, D_MODEL)
    xl = x.reshape(BATCH * SEQ, D_MODEL)
    xc = ctx.reshape(BATCH * CTX_LEN, D_MODEL)
    c_rows = jnp.concatenate([c, c_ctx[None, :], jnp.zeros((MOD_ROWS - BATCH - 1, D_MODEL), F32)], axis=0)
    mod = _modulation(c_rows, w_mod, b_mod).reshape(DEPTH, MOD_ROWS, N_MOD, 1, D_MODEL)
    cos, sin = _rope_tables()
    w1_in, w1_out = ffn1_w_in[0].astype(BF16), ffn1_w_out[0].astype(BF16)
    w_head, w_tail, w_gate = _split_w_in(w_in)
    wb = w_branch.astype(BF16)
    wo = w_out.astype(BF16)

    for li in range(DEPTH):
        last = li == DEPTH - 1
        wgt = w_gate[li]
        bgt = _gate_bias(mlstm_b_i[li], mlstm_b_f[li])
        bias = _na_bias(na_rpb[li])
        wa, wx = lru_w_a[li].astype(BF16), lru_w_x[li].astype(BF16)

        xl, w2_in, w2_out = _ffn(xl, li, mod, 0, norm_ffn1[li], w1_in, w1_out, _lat_row,
                                 cast_job=(ffn2_w_in, ffn2_w_out, li))
        xc = _ffn(xc, li, mod, 0, norm_ffn1[li], w1_in, w1_out, _ctx_row)

        p_lat, gt_lat = _inproj(xl, li, mod, norm_mix[li], w_head, w_tail, wgt, bgt, _lat_row)
        p_ctx, gt_ctx = _inproj(xc, li, mod, norm_mix[li], w_head, w_tail, wgt, bgt, _ctx_row)

        ym_l, ym_c = _mlstm(p_lat, p_ctx, gt_lat, gt_ctx, cos, sin, mlstm_gn[li])
        yn_l, yn_c = _na(p_lat, p_ctx, bias)
        hf_l, hb_l, hf_c, hb_c = _lru(p_lat, p_ctx, lru_conv_w[li], lru_conv_b[li], wa, lru_b_a[li], wx,
                                      lru_b_x[li], lru_lambda[li])

        s_l = _merge(ym_l, yn_l, hf_l, hb_l, p_lat, li, wb)
        xl = _outproj(s_l, li, wo, xl, mod, _lat_row)
        if last:
            xl = _ffn(xl, li, mod, 6, norm_ffn2[li], w2_in, w2_out, _lat_row, norm_final)
        else:
            xl, w1_in, w1_out = _ffn(xl, li, mod, 6, norm_ffn2[li], w2_in, w2_out, _lat_row,
                                     cast_job=(ffn1_w_in, ffn1_w_out, li + 1))
            s_c = _merge(ym_c, yn_c, hf_c, hb_c, p_ctx, li, wb)
            xc = _outproj(s_c, li, wo, xc, mod, _ctx_row)
            xc = _ffn(xc, li, mod, 6, norm_ffn2[li], w2_in, w2_out, _ctx_row)
    return xl.reshape(BATCH, SEQ, D_MODEL)
```

```python
import functools

import jax
import jax.numpy as jnp
from jax import lax
from jax.experimental import pallas as pl
from jax.experimental.pallas import tpu as pltpu

F32 = jnp.float32
BF16 = jnp.bfloat16

D_MODEL = 2048
BATCH = 4
SEQ = 2048
DEPTH = 2
GRID_W = 64
CTX_LEN = 256
EPS = 1e-6
D_FF = (11 * D_MODEL) // 4
N_MOD = 9
BRANCH_W = D_MODEL // 2
N_BRANCH = 3
M_HEADS = 4
M_HEAD_DIM = BRANCH_W // M_HEADS
ROPE_BASE = 10000.0
NA_HEADS = 8
NA_HEAD_DIM = BRANCH_W // NA_HEADS
NA_KH = 8
NA_KW = 16
LRU_BLOCKS = 8
LRU_BLOCK_DIM = BRANCH_W // LRU_BLOCKS
LRU_CONV = 4
LRU_C = 8.0

SUBLANES = 8
LANES = 128

MOD_ROWS = 8
CTX_MOD_ROW = BATCH
TM_FFN = 1024
TM_IN = 1024
TM_MERGE = 1024
TM_OUT = 1024
TF = 512
TN_IN = 1024
TN_MERGE = 512
TN_OUT = 512
TN_MOD = 1024
M_CHUNK = 256
N_CHUNKS = (SEQ + CTX_LEN) // M_CHUNK
LRU_CHUNK = 256
LRU_STEPS = (SEQ + CTX_LEN) // LRU_CHUNK
GATE_R = SUBLANES

COL_MQ, COL_MK, COL_MV, COL_MO = 0, BRANCH_W, 2 * BRANCH_W, 3 * BRANCH_W
COL_NQ, COL_NK, COL_NV = 4 * BRANCH_W, 5 * BRANCH_W, 6 * BRANCH_W
COL_LX, COL_LG = 7 * BRANCH_W, 8 * BRANCH_W
COL_GX = 9 * BRANCH_W
P_COLS = COL_GX + N_BRANCH * D_MODEL

VMEM_BYTES = 64 * 1024 * 1024
VMEM_LIMIT = VMEM_BYTES - 8 * 1024 * 1024


def _cparams(sem):
    return pltpu.CompilerParams(dimension_semantics=sem, vmem_limit_bytes=VMEM_LIMIT)


def _sigmoid(x):
    return jax.nn.sigmoid(x)


def _log_sigmoid(x):
    return jnp.minimum(x, 0.0) - jnp.log1p(jnp.exp(-jnp.abs(x)))


def _softplus(x):
    return jnp.maximum(x, 0.0) + jnp.log1p(jnp.exp(-jnp.abs(x)))


def _dot(a, b):
    return jnp.dot(a, b, preferred_element_type=F32)


def _dot_nt(a, b):
    return lax.dot_general(a, b, (((1,), (1,)), ((), ())), preferred_element_type=F32)


def _dot_tn(a, b):
    return lax.dot_general(a, b, (((0,), (0,)), ((), ())), preferred_element_type=F32)


def _dot_exact(a, b):
    return jnp.dot(a, b, preferred_element_type=F32, precision=lax.Precision.HIGHEST)


def _dot_split(x, m):
    hi = x.astype(BF16)
    r1 = x - hi.astype(F32)
    mid = r1.astype(BF16)
    lo = (r1 - mid.astype(F32)).astype(BF16)
    return _dot(hi, m) + _dot(mid, m) + _dot(lo, m)


def _adaln(x, gain, shift, scale):
    y = x * lax.rsqrt(jnp.mean(x * x, axis=-1, keepdims=True) + EPS)
    return (y * gain) * (1.0 + scale) + shift


ADALN_ROWS = 16


def _adaln_into(h_ref, x_ref, gain, shift, scale):
    gs = gain * (1.0 + scale)

    def body(i, carry):
        rows = pl.ds(pl.multiple_of(i * ADALN_ROWS, ADALN_ROWS), ADALN_ROWS)
        x = x_ref[rows, :]
        y = x * lax.rsqrt(jnp.mean(x * x, axis=-1, keepdims=True) + EPS)
        h_ref[rows, :] = (y * gs + shift).astype(h_ref.dtype)
        return carry

    lax.fori_loop(0, x_ref.shape[0] // ADALN_ROWS, body, 0, unroll=4)


def _mod_kernel(c_ref, w_ref, b_ref, o_ref):
    c = c_ref[...]
    act = (c * _sigmoid(c)).astype(BF16)
    o_ref[...] = _dot(act, w_ref[...].astype(BF16)) + b_ref[...]


def _modulation(c_rows, w_mod, b_mod):
    n_cols = N_MOD * D_MODEL
    return pl.pallas_call(
        _mod_kernel,
        grid=(DEPTH, n_cols // TN_MOD),
        in_specs=[
            pl.BlockSpec((MOD_ROWS, D_MODEL), lambda l, j: (0, 0)),
            pl.BlockSpec((None, D_MODEL, TN_MOD), lambda l, j: (l, 0, j)),
            pl.BlockSpec((None, 1, TN_MOD), lambda l, j: (l, 0, j)),
        ],
        out_specs=pl.BlockSpec((None, MOD_ROWS, TN_MOD), lambda l, j: (l, 0, j)),
        out_shape=jax.ShapeDtypeStruct((DEPTH, MOD_ROWS, n_cols), F32),
        compiler_params=_cparams(("arbitrary", "arbitrary")),
        name="modulation",
    )(c_rows, w_mod, b_mod.reshape(DEPTH, 1, n_cols))


def _mod_spec(li, k, row_of_tile):
    return pl.BlockSpec((None, None, None, 1, D_MODEL), lambda i, j: (li, row_of_tile(i), k, 0, 0))


def _lat_row(tm):
    return lambda i: i // (SEQ // tm)


def _ctx_row(tm):
    return lambda i: CTX_MOD_ROW


def _ffn_kernel(*refs, final, cast):
    refs = list(refs)
    x_ref, sh_ref, sc_ref, g_ref, ng_ref, wg_ref, wu_ref, wo_ref = refs[:8]
    del refs[:8]
    fg_ref = refs.pop(0) if final else None
    if cast:
        src_in, src_out, o_ref, dst_in, dst_out, h_scr = refs
        dst_in[...] = src_in[...].astype(BF16)
        dst_out[...] = src_out[...].astype(BF16)
    else:
        o_ref, h_scr = refs
    f = pl.program_id(1)

    @pl.when(f == 0)
    def _():
        _adaln_into(h_scr, x_ref, ng_ref[...], sh_ref[...], sc_ref[...])
        o_ref[...] = jnp.zeros_like(o_ref)

    h = h_scr[...]
    gte = _dot(h, wg_ref[...])
    up = _dot(h, wu_ref[...])
    act = (gte * _sigmoid(gte) * up).astype(BF16)
    o_ref[...] += _dot(act, wo_ref[...])

    @pl.when(f == pl.num_programs(1) - 1)
    def _():
        out = x_ref[...] + 0.5 * g_ref[...] * o_ref[...]
        if final:
            out = out * lax.rsqrt(jnp.mean(out * out, axis=-1, keepdims=True) + EPS) * fg_ref[...]
        o_ref[...] = out


def _ffn(x, li, mod, k0, norm_g, w_in, w_out, row_fn, final_gain=None, cast_job=None):
    rows = x.shape[0]
    nf = D_FF // TF
    tm = TM_FFN
    nt = rows // tm
    row_of_tile = row_fn(tm)
    final = final_gain is not None
    cast = cast_job is not None
    in_specs = [
        pl.BlockSpec((tm, D_MODEL), lambda i, f: (i, 0), pipeline_mode=pl.Buffered(1)),
        _mod_spec(li, k0, row_of_tile), _mod_spec(li, k0 + 1, row_of_tile), _mod_spec(li, k0 + 2, row_of_tile),
        pl.BlockSpec((1, D_MODEL), lambda i, f: (0, 0)),
        pl.BlockSpec((D_MODEL, TF), lambda i, f: (0, f)),
        pl.BlockSpec((D_MODEL, TF), lambda i, f: (0, f + nf)),
        pl.BlockSpec((TF, D_MODEL), lambda i, f: (f, 0)),
    ]
    args = [x, mod, mod, mod, norm_g.reshape(1, D_MODEL), w_in, w_in, w_out]
    if final:
        in_specs.append(pl.BlockSpec((1, D_MODEL), lambda i, f: (0, 0)))
        args.append(final_gain.reshape(1, D_MODEL))
    out_specs = [pl.BlockSpec((tm, D_MODEL), lambda i, f: (i, 0))]
    out_shape = [jax.ShapeDtypeStruct((rows, D_MODEL), F32)]
    if cast:
        src_in, src_out, lc = cast_job
        in_blk = (D_MODEL // nt, 2 * D_FF // nf)
        out_blk = (D_FF // nf, D_MODEL // nt)
        in_specs += [pl.BlockSpec((None,) + in_blk, lambda i, f: (lc, i, f)),
                     pl.BlockSpec((None,) + out_blk, lambda i, f: (lc, f, i))]
        args += [src_in, src_out]
        out_specs += [pl.BlockSpec(in_blk, lambda i, f: (i, f)), pl.BlockSpec(out_blk, lambda i, f: (f, i))]
        out_shape += [jax.ShapeDtypeStruct((D_MODEL, 2 * D_FF), BF16), jax.ShapeDtypeStruct((D_FF, D_MODEL), BF16)]
    res = pl.pallas_call(
        functools.partial(_ffn_kernel, final=final, cast=cast),
        grid=(nt, nf),
        in_specs=in_specs,
        out_specs=out_specs,
        out_shape=out_shape,
        scratch_shapes=[pltpu.VMEM((tm, D_MODEL), BF16)],
        compiler_params=_cparams(("arbitrary", "arbitrary")),
        name="ffn_final" if final else "ffn",
    )(*args)
    return res if cast else res[0]


N_HEAD_TILES = COL_NQ // TN_IN


def _inproj_kernel(x_ref, sh_ref, sc_ref, ng_ref, wa_ref, wb_ref, wgt_ref, bgt_ref, p_ref, gt_ref, h_scr):
    j = pl.program_id(1)

    @pl.when(j == 0)
    def _():
        h = _adaln(x_ref[...], ng_ref[...], sh_ref[...], sc_ref[...]).astype(BF16)
        h_scr[...] = h
        for s in range(TM_IN // M_CHUNK):
            gt_ref[s] = _dot_nt(wgt_ref[...], h[s * M_CHUNK:(s + 1) * M_CHUNK, :]) + bgt_ref[...]

    @pl.when(j < N_HEAD_TILES)
    def _():
        p_ref[...] = _dot(h_scr[...], wa_ref[...]).astype(p_ref.dtype)

    @pl.when(j >= N_HEAD_TILES)
    def _():
        p_ref[...] = _dot(h_scr[...], wb_ref[...]).astype(p_ref.dtype)


def _inproj(x, li, mod, norm_g, wa, wb, wgt, bgt, row_fn):
    rows = x.shape[0]
    tm = TM_IN
    row_of_tile = row_fn(tm)
    return pl.pallas_call(
        _inproj_kernel,
        grid=(rows // tm, P_COLS // TN_IN),
        in_specs=[
            pl.BlockSpec((tm, D_MODEL), lambda i, j: (i, 0)),
            _mod_spec(li, 3, row_of_tile), _mod_spec(li, 4, row_of_tile),
            pl.BlockSpec((1, D_MODEL), lambda i, j: (0, 0)),
            pl.BlockSpec((None, D_MODEL, TN_IN), lambda i, j: (li, 0, jnp.minimum(j, N_HEAD_TILES - 1))),
            pl.BlockSpec((None, D_MODEL, TN_IN), lambda i, j: (li, 0, jnp.maximum(j - N_HEAD_TILES, 0))),
            pl.BlockSpec((M_HEADS * GATE_R, D_MODEL), lambda i, j: (0, 0)),
            pl.BlockSpec((M_HEADS * GATE_R, 1), lambda i, j: (0, 0)),
        ],
        out_specs=[
            pl.BlockSpec((tm, TN_IN), lambda i, j: (i, j)),
            pl.BlockSpec((tm // M_CHUNK, M_HEADS * GATE_R, M_CHUNK), lambda i, j: (i, 0, 0)),
        ],
        out_shape=[
            jax.ShapeDtypeStruct((rows, P_COLS), BF16),
            jax.ShapeDtypeStruct((rows // M_CHUNK, M_HEADS * GATE_R, M_CHUNK), F32),
        ],
        scratch_shapes=[pltpu.VMEM((tm, D_MODEL), BF16)],
        compiler_params=_cparams(("arbitrary", "arbitrary")),
        name="inproj",
    )(x, mod, mod, norm_g.reshape(1, D_MODEL), wa, wb, wgt, bgt)


def _rope(u, cs, sn):
    halves = []
    for j in range(M_HEAD_DIM // LANES):
        sl = slice(j * LANES, (j + 1) * LANES)
        uj = u[:, sl]
        halves.append(uj * cs[:, sl] + pltpu.roll(uj, LANES // 2, 1) * sn[:, sl])
    return jnp.concatenate(halves, axis=1)


def _mlstm_chunk(q, k, v, i_col, cum_col, i_row, cum_row, mask, last, c_mem, n_mem, m_prev):
    logw = jnp.where(mask, cum_col - cum_row + i_row, -jnp.inf)
    m_row = jnp.maximum(cum_col + m_prev, jnp.max(logw, axis=-1, keepdims=True))
    m_new = m_row[last:last + 1, :]
    cum_last = cum_col[last:last + 1, :]
    w_state = jnp.exp(cum_last - cum_col + i_col - m_new)
    decay = jnp.exp(cum_last + m_prev - m_new)
    inter = jnp.exp(cum_col + m_prev - m_row)
    kf = k.astype(F32)
    sc = _dot_nt(q, k) * jnp.exp(logw - m_row)
    num = inter * _dot_nt(q, c_mem.astype(BF16)) + _dot(sc.astype(BF16), v)
    den = inter * jnp.sum(q.astype(F32) * n_mem, axis=-1, keepdims=True) + jnp.sum(sc, axis=-1, keepdims=True)
    hid = num / jnp.maximum(jnp.abs(den), jnp.exp(-m_row))
    c_new = decay * c_mem + _dot_tn((w_state * v.astype(F32)).astype(BF16), k)
    n_new = decay * n_mem + jnp.sum(w_state * kf, axis=0, keepdims=True)
    return hid, c_new, n_new, m_new


def _mlstm_kernel(ql, kl, vl, ol, qc, kc, vc, oc, gtl, gtc, cos, sin, gn, yl, yc,
                  q_scr, k_scr, hf_scr, hb_scr, grow_scr, gcol_scr):
    ch = M_CHUNK
    kscale = M_HEAD_DIM ** -0.5

    def rows(c):
        return slice(c * ch, (c + 1) * ch)

    t_idx = lax.broadcasted_iota(jnp.int32, (ch, ch), 0)
    s_idx = lax.broadcasted_iota(jnp.int32, (ch, ch), 1)
    lower = s_idx <= t_idx
    upper = s_idx >= t_idx
    lower_b = lower.astype(BF16)
    upper_b = upper.astype(BF16)
    gate_row = lax.broadcasted_iota(jnp.int32, (GATE_R, ch), 0)
    lane_pad = jnp.zeros((LANES - GATE_R, ch), F32)

    for c in range(N_CHUNKS):
        if c == 0:
            q_scr[rows(0), :] = qc[...].astype(BF16)
            k_scr[rows(0), :] = (kc[...].astype(F32) * kscale).astype(BF16)
        else:
            lat = rows(c - 1)
            q_scr[rows(c), :] = _rope(ql[lat, :].astype(F32), cos[lat, :], sin[lat, :]).astype(BF16)
            k_scr[rows(c), :] = (_rope(kl[lat, :].astype(F32), cos[lat, :], sin[lat, :]) * kscale).astype(BF16)
        gt = gtc[0] if c == 0 else gtl[c - 1]
        lf = _log_sigmoid(gt)
        terms = jnp.where(gate_row == 1, _dot_split(lf, upper_b), jnp.where(gate_row == 3, _dot_split(lf, lower_b), gt))
        grow_scr[c] = terms
        gcol_scr[rows(c), :] = jnp.concatenate([terms, lane_pad], axis=0).T

    def values(c):
        return vc[...].astype(BF16) if c == 0 else vl[rows(c - 1), :].astype(BF16)

    def init():
        return (jnp.zeros((M_HEAD_DIM, M_HEAD_DIM), F32), jnp.zeros((1, M_HEAD_DIM), F32),
                jnp.full((1, 1), -jnp.inf, F32))

    def run(c, d, mask, last, state):
        g_rows = grow_scr[c]
        g_cols = gcol_scr[rows(c), :]
        i0, f0 = 2 * d, 2 * d + 1
        return _mlstm_chunk(q_scr[rows(c), :], k_scr[rows(c), :], values(c), g_cols[:, i0:i0 + 1],
                            g_cols[:, f0:f0 + 1], g_rows[i0:i0 + 1, :], g_rows[f0:f0 + 1, :], mask, last, *state)

    st_f, st_b = init(), init()
    for step in range(N_CHUNKS):
        cf = step
        cb = 0 if step == 0 else N_CHUNKS - step
        hid, *st_f = run(cf, 0, lower, ch - 1, st_f)
        hf_scr[rows(cf), :] = hid
        hid, *st_b = run(cb, 1, upper, 0, st_b)
        hb_scr[rows(cb), :] = hid

    gain = gn[...]
    for c in range(N_CHUNKS):
        hsum = hf_scr[rows(c), :] + hb_scr[rows(c), :]
        hn = hsum * lax.rsqrt(jnp.mean(hsum * hsum, axis=-1, keepdims=True) + EPS) * gain
        if c == 0:
            yc[...] = (_sigmoid(oc[...].astype(F32)) * hn).astype(yc.dtype)
        else:
            yl[rows(c - 1), :] = (_sigmoid(ol[rows(c - 1), :].astype(F32)) * hn).astype(yl.dtype)


def _mlstm(p_lat, p_ctx, gt_lat, gt_ctx, cos, sin, gn):
    hd = M_HEAD_DIM
    cb = lambda col: col // hd

    def lat(col):
        return pl.BlockSpec((SEQ, hd), lambda b, h: (b, cb(col) + h))

    def ctx(col):
        return pl.BlockSpec((CTX_LEN, hd), lambda b, h: (b, cb(col) + h))

    seq_chunks = SEQ // M_CHUNK
    ctx_chunks = CTX_LEN // M_CHUNK
    return pl.pallas_call(
        _mlstm_kernel,
        grid=(BATCH, M_HEADS),
        in_specs=[
            lat(COL_MQ), lat(COL_MK), lat(COL_MV), lat(COL_MO),
            ctx(COL_MQ), ctx(COL_MK), ctx(COL_MV), ctx(COL_MO),
            pl.BlockSpec((seq_chunks, GATE_R, M_CHUNK), lambda b, h: (b, h, 0)),
            pl.BlockSpec((ctx_chunks, GATE_R, M_CHUNK), lambda b, h: (b, h, 0)),
            pl.BlockSpec((SEQ, hd), lambda b, h: (0, 0)),
            pl.BlockSpec((SEQ, hd), lambda b, h: (0, 0)),
            pl.BlockSpec((1, hd), lambda b, h: (0, h)),
        ],
        out_specs=[
            pl.BlockSpec((SEQ, hd), lambda b, h: (b, h)),
            pl.BlockSpec((CTX_LEN, hd), lambda b, h: (b, h)),
        ],
        out_shape=[
            jax.ShapeDtypeStruct((BATCH * SEQ, BRANCH_W), BF16),
            jax.ShapeDtypeStruct((BATCH * CTX_LEN, BRANCH_W), BF16),
        ],
        scratch_shapes=[
            pltpu.VMEM((SEQ + CTX_LEN, hd), BF16), pltpu.VMEM((SEQ + CTX_LEN, hd), BF16),
            pltpu.VMEM((SEQ + CTX_LEN, hd), F32), pltpu.VMEM((SEQ + CTX_LEN, hd), F32),
            pltpu.VMEM((N_CHUNKS, GATE_R, M_CHUNK), F32), pltpu.VMEM((SEQ + CTX_LEN, LANES), F32),
        ],
        compiler_params=_cparams(("arbitrary", "arbitrary")),
        name="mlstm",
    )(p_lat, p_lat, p_lat, p_lat, p_ctx, p_ctx, p_ctx, p_ctx, gt_lat, gt_ctx, cos, sin, gn.reshape(1, BRANCH_W))


def _rope_tables():
    pos = jnp.arange(SEQ)
    nf = M_HEAD_DIM // 4
    inv = ROPE_BASE ** (-jnp.arange(nf, dtype=F32) / nf)
    ang_r = (pos // GRID_W).astype(F32)[:, None] * inv[None, :]
    ang_c = (pos % GRID_W).astype(F32)[:, None] * inv[None, :]
    cr, sr, cc, sc = jnp.cos(ang_r), jnp.sin(ang_r), jnp.cos(ang_c), jnp.sin(ang_c)
    return (jnp.concatenate([cr, cr, cc, cc], axis=-1), jnp.concatenate([-sr, sr, -sc, sc], axis=-1))


NA_GROUP = 2
NA_ROWS = SEQ // GRID_W
NA_QROWS = 4
NA_UNION = NA_KH + NA_QROWS
NA_QT = NA_QROWS * GRID_W
NA_KT = NA_UNION * GRID_W
NA_QGROUPS = NA_ROWS // NA_QROWS
NA_TYPES = 3


def _na_union_start(g):
    return max(0, min(NA_QROWS * g - NA_KH // 2, NA_ROWS - NA_UNION))


def _na_kernel(ql, kl, vl, qc, kc, vc, bias, yl, yc):
    scale = NA_HEAD_DIM ** -0.5
    kcs = [kc[:, hh * NA_HEAD_DIM:(hh + 1) * NA_HEAD_DIM].astype(BF16) for hh in range(NA_GROUP)]
    vcs = [vc[:, hh * NA_HEAD_DIM:(hh + 1) * NA_HEAD_DIM].astype(BF16) for hh in range(NA_GROUP)]

    def group_body(g, carry):
        u = jnp.clip(NA_QROWS * g - NA_KH // 2, 0, NA_ROWS - NA_UNION)
        kind = jnp.where(g == 0, 0, jnp.where(g == NA_QGROUPS - 1, NA_TYPES - 1, 1))
        q_rows = pl.ds(pl.multiple_of(g * NA_QT, NA_QT), NA_QT)
        k_rows = pl.ds(pl.multiple_of(u * GRID_W, GRID_W), NA_KT)
        outs = []
        for hh in range(NA_GROUP):
            sl = slice(hh * NA_HEAD_DIM, (hh + 1) * NA_HEAD_DIM)
            q = ql[q_rows, sl].astype(BF16)
            kw = kl[k_rows, sl].astype(BF16)
            vw = vl[k_rows, sl].astype(BF16)
            s_win = _dot_nt(q, kw) * scale + bias[hh, kind]
            s_ctx = _dot_nt(q, kcs[hh]) * scale
            m = jnp.maximum(jnp.max(s_win, axis=-1, keepdims=True), jnp.max(s_ctx, axis=-1, keepdims=True))
            p_win = jnp.exp(s_win - m)
            p_ctx = jnp.exp(s_ctx - m)
            denom = jnp.sum(p_win, axis=-1, keepdims=True) + jnp.sum(p_ctx, axis=-1, keepdims=True)
            o = _dot(p_win.astype(BF16), vw) + _dot(p_ctx.astype(BF16), vcs[hh])
            outs.append(o / denom)
        yl[q_rows, :] = jnp.concatenate(outs, axis=1).astype(yl.dtype)
        return carry

    lax.fori_loop(0, NA_QGROUPS, group_body, 0)

    outs = []
    for hh in range(NA_GROUP):
        sl = slice(hh * NA_HEAD_DIM, (hh + 1) * NA_HEAD_DIM)
        s = _dot_nt(qc[:, sl].astype(BF16), kcs[hh]) * scale
        p = jnp.exp(s - jnp.max(s, axis=-1, keepdims=True))
        outs.append(_dot(p.astype(BF16), vcs[hh]) / jnp.sum(p, axis=-1, keepdims=True))
    yc[...] = jnp.concatenate(outs, axis=1).astype(yc.dtype)


def _na_bias_kernel(rpb_ref, o_ref):
    shape = (LANES, GRID_W * GRID_W)
    n = lax.broadcasted_iota(jnp.int32, shape, 1)
    c = lax.broadcasted_iota(jnp.int32, shape, 0)
    q = jnp.right_shift(n, GRID_W.bit_length() - 1)
    k = jnp.bitwise_and(n, GRID_W - 1)
    dc = jnp.clip(k - q, -(NA_KW - 1), NA_KW - 1) + (NA_KW - 1)
    picked = _dot_exact(rpb_ref[...], (c == dc).astype(F32))
    col_start = jnp.clip(q - NA_KW // 2, 0, GRID_W - NA_KW)
    col_ok = (k >= col_start) & (k < col_start + NA_KW)
    o_ref[...] = jnp.where(col_ok, picked, -jnp.inf)


def _na_bias(rpb):
    assert _na_union_start(1) == 0 and _na_union_start(2) == NA_QROWS
    assert GRID_W & (GRID_W - 1) == 0 and NA_HEADS * (2 * NA_KH - 1) <= LANES and 2 * NA_KW - 1 <= LANES
    n_dr = 2 * NA_KH - 1
    flat = rpb.astype(F32).reshape(NA_HEADS * n_dr, 2 * NA_KW - 1)
    flat = jnp.pad(flat, ((0, LANES - NA_HEADS * n_dr), (0, LANES - (2 * NA_KW - 1))))
    t = pl.pallas_call(
        _na_bias_kernel,
        out_shape=jax.ShapeDtypeStruct((LANES, GRID_W * GRID_W), F32),
        name="natten_bias",
    )(flat)
    t = t[:NA_HEADS * n_dr].reshape(NA_HEADS, n_dr, GRID_W, GRID_W)
    tp = jnp.pad(t, ((0, 0), (NA_UNION, NA_UNION), (0, 0), (0, 0)), constant_values=-jnp.inf)
    runs = []
    for g in (0, 1, NA_QGROUPS - 1):
        u = _na_union_start(g)
        for i in range(NA_QROWS):
            r = NA_QROWS * g + i
            start = max(0, min(r - NA_KH // 2, NA_ROWS - NA_KH))
            dr0 = u - r + NA_KH - 1 + NA_UNION
            in_window = jnp.array([start <= u + j < start + NA_KH for j in range(NA_UNION)])
            runs.append(jnp.where(in_window[None, :, None, None], tp[:, dr0:dr0 + NA_UNION], -jnp.inf))
    table = jnp.stack(runs, axis=1).reshape(NA_HEADS, NA_TYPES, NA_QROWS, NA_UNION, GRID_W, GRID_W)
    return jnp.transpose(table, (0, 1, 2, 4, 3, 5)).reshape(NA_HEADS, NA_TYPES, NA_QT, NA_KT)


def _na(p_lat, p_ctx, bias):
    gw = NA_GROUP * NA_HEAD_DIM
    cb = lambda col: col // gw

    def lat(col):
        return pl.BlockSpec((SEQ, gw), lambda b, g: (b, cb(col) + g))

    def ctx(col):
        return pl.BlockSpec((CTX_LEN, gw), lambda b, g: (b, cb(col) + g))

    return pl.pallas_call(
        _na_kernel,
        grid=(BATCH, NA_HEADS // NA_GROUP),
        in_specs=[
            lat(COL_NQ), lat(COL_NK), lat(COL_NV), ctx(COL_NQ), ctx(COL_NK), ctx(COL_NV),
            pl.BlockSpec((NA_GROUP, NA_TYPES, NA_QT, NA_KT), lambda b, g: (g, 0, 0, 0)),
        ],
        out_specs=[
            pl.BlockSpec((SEQ, gw), lambda b, g: (b, g)),
            pl.BlockSpec((CTX_LEN, gw), lambda b, g: (b, g)),
        ],
        out_shape=[
            jax.ShapeDtypeStruct((BATCH * SEQ, BRANCH_W), BF16),
            jax.ShapeDtypeStruct((BATCH * CTX_LEN, BRANCH_W), BF16),
        ],
        compiler_params=_cparams(("arbitrary", "arbitrary")),
        name="natten",
    )(p_lat, p_lat, p_lat, p_ctx, p_ctx, p_ctx, bias)


LRU_HALO = 2 * SUBLANES


def _lru_kernel(xl, xc, cw, cb, wa, ba, wx, bx, lam, hfl, hbl, hfc, hbc,
                ext_f, ext_b, a_f, u_f, a_b, u_b, carry):
    s = pl.program_id(1)
    ch = LRU_CHUNK
    n_lat = SEQ // ch
    halo = LRU_HALO
    zeros_halo = jnp.zeros((halo, BRANCH_W), F32)

    def fill(ext, chunk):
        @pl.when(s == 0)
        def _():
            ext[0:halo, :] = zeros_halo
            ext[halo:halo + ch, :] = xc[...].astype(F32)
            ext[halo + ch:, :] = zeros_halo

        @pl.when(s > 0)
        def _():
            base = pl.multiple_of(chunk * ch, ch)
            ext[halo:halo + ch, :] = xl[pl.ds(base, ch), :].astype(F32)
            prev = xl[pl.ds(pl.multiple_of(jnp.maximum(base - halo, 0), halo), halo), :].astype(F32)
            nxt = xl[pl.ds(pl.multiple_of(jnp.minimum(base + ch, SEQ - halo), halo), halo), :].astype(F32)
            ext[0:halo, :] = jnp.where(chunk > 0, prev, 0.0)
            ext[halo + ch:, :] = jnp.where(chunk < n_lat - 1, nxt, 0.0)

    chunk_f = jnp.maximum(s - 1, 0)
    chunk_b = jnp.maximum(n_lat - s, 0)
    fill(ext_f, chunk_f)
    fill(ext_b, chunk_b)

    def coeffs(ext, d, a_out, u_out):
        xconv = cb[...] + ext[halo - 1:halo - 1 + ch, :] * cw[0:1, :]
        for j in range(1, LRU_CONV):
            xconv = xconv + ext[halo - 1 + j:halo - 1 + j + ch, :] * cw[j:j + 1, :]
        sp = _softplus(-lam[d:d + 1, :])
        for g in range(LRU_BLOCKS):
            sl = slice(g * LRU_BLOCK_DIM, (g + 1) * LRU_BLOCK_DIM)
            xg = xconv[:, sl]
            xb = xg.astype(BF16)
            r = _sigmoid(_dot(xb, wa[d, g]) + ba[d:d + 1, sl])
            i = _sigmoid(_dot(xb, wx[d, g]) + bx[d:d + 1, sl])
            log_a = -LRU_C * r * sp[:, sl]
            a = jnp.exp(log_a)
            a_out[:, sl] = a
            u_out[:, sl] = jnp.sqrt(jnp.tanh(-log_a) * (a * a + 1.0)) * i * xg

    coeffs(ext_f, 0, a_f, u_f)
    coeffs(ext_b, 1, a_b, u_b)

    @pl.when(s == 0)
    def _():
        carry[...] = jnp.zeros_like(carry)

    row = lax.broadcasted_iota(jnp.int32, (SUBLANES, BRANCH_W), 0)
    n_tiles = ch // SUBLANES

    def scan_tiles(hf_out, hb_out):
        def body(t, hs):
            h_f, h_b = hs
            r0 = pl.multiple_of(t * SUBLANES, SUBLANES)
            a = a_f[pl.ds(r0, SUBLANES), :]
            u = u_f[pl.ds(r0, SUBLANES), :]
            for d in (1, 2, 4):
                ok = row >= d
                u = jnp.where(ok, a * pltpu.roll(u, d, 0) + u, u)
                a = jnp.where(ok, a * pltpu.roll(a, d, 0), a)
            h = u + a * h_f
            hf_out[pl.ds(r0, SUBLANES), :] = h
            h_f = h[SUBLANES - 1:SUBLANES, :]
            r1 = pl.multiple_of((n_tiles - 1 - t) * SUBLANES, SUBLANES)
            a = a_b[pl.ds(r1, SUBLANES), :]
            u = u_b[pl.ds(r1, SUBLANES), :]
            for d in (1, 2, 4):
                ok = row < SUBLANES - d
                u = jnp.where(ok, a * pltpu.roll(u, SUBLANES - d, 0) + u, u)
                a = jnp.where(ok, a * pltpu.roll(a, SUBLANES - d, 0), a)
            h = u + a * h_b
            hb_out[pl.ds(r1, SUBLANES), :] = h
            h_b = h[0:1, :]
            return h_f, h_b

        h_f, h_b = lax.fori_loop(0, n_tiles, body, (carry[0:1, :], carry[1:2, :]))
        carry[0:1, :] = h_f
        carry[1:2, :] = h_b

    @pl.when(s == 0)
    def _():
        scan_tiles(hfc, hbc)

    @pl.when(s > 0)
    def _():
        scan_tiles(hfl, hbl)


def _lru(p_lat, p_ctx, conv_w, conv_b, w_a, b_a, w_x, b_x, lam):
    ch = LRU_CHUNK
    n_lat = SEQ // ch
    full = lambda shape: pl.BlockSpec(shape, lambda b, s: (0,) * len(shape))
    wblk = (2, LRU_BLOCKS, LRU_BLOCK_DIM, LRU_BLOCK_DIM)
    return pl.pallas_call(
        _lru_kernel,
        grid=(BATCH, LRU_STEPS),
        in_specs=[
            pl.BlockSpec((SEQ, BRANCH_W), lambda b, s: (b, COL_LX // BRANCH_W)),
            pl.BlockSpec((CTX_LEN, BRANCH_W), lambda b, s: (b, COL_LX // BRANCH_W)),
            full((LRU_CONV, BRANCH_W)), full((1, BRANCH_W)),
            full(wblk), full((2, BRANCH_W)), full(wblk), full((2, BRANCH_W)), full((2, BRANCH_W)),
        ],
        out_specs=[
            pl.BlockSpec((ch, BRANCH_W), lambda b, s: (b * n_lat + jnp.maximum(s - 1, 0), 0)),
            pl.BlockSpec((ch, BRANCH_W), lambda b, s: (b * n_lat + jnp.minimum(n_lat - s, n_lat - 1), 0)),
            pl.BlockSpec((CTX_LEN, BRANCH_W), lambda b, s: (b, 0)),
            pl.BlockSpec((CTX_LEN, BRANCH_W), lambda b, s: (b, 0)),
        ],
        out_shape=[
            jax.ShapeDtypeStruct((BATCH * SEQ, BRANCH_W), F32),
            jax.ShapeDtypeStruct((BATCH * SEQ, BRANCH_W), F32),
            jax.ShapeDtypeStruct((BATCH * CTX_LEN, BRANCH_W), F32),
            jax.ShapeDtypeStruct((BATCH * CTX_LEN, BRANCH_W), F32),
        ],
        scratch_shapes=[
            pltpu.VMEM((ch + 2 * LRU_HALO, BRANCH_W), F32), pltpu.VMEM((ch + 2 * LRU_HALO, BRANCH_W), F32),
            pltpu.VMEM((ch, BRANCH_W), F32), pltpu.VMEM((ch, BRANCH_W), F32),
            pltpu.VMEM((ch, BRANCH_W), F32), pltpu.VMEM((ch, BRANCH_W), F32),
            pltpu.VMEM((SUBLANES, BRANCH_W), F32),
        ],
        compiler_params=_cparams(("arbitrary", "arbitrary")),
        name="rglru",
    )(p_lat, p_ctx, conv_w, conv_b.reshape(1, BRANCH_W), w_a, b_a, w_x, b_x, lam)


def _merge_kernel(ym, yn, hf, hb, lg, g0, g1, g2, wb, o_ref, yl_s):
    j = pl.program_id(1)

    @pl.when(j == 0)
    def _():
        yl_s[...] = ((hf[...] + hb[...]) * jax.nn.gelu(lg[...].astype(F32))).astype(BF16)

    acc = _sigmoid(g0[...].astype(F32)) * _dot(ym[...], wb[0])
    acc = acc + _sigmoid(g1[...].astype(F32)) * _dot(yn[...], wb[1])
    acc = acc + _sigmoid(g2[...].astype(F32)) * _dot(yl_s[...], wb[2])
    o_ref[...] = acc.astype(BF16)


def _merge(ym, yn, hf, hb, p, li, w_branch):
    rows = ym.shape[0]
    tn = TN_MERGE
    tm = TM_MERGE
    br = lambda: pl.BlockSpec((tm, BRANCH_W), lambda i, j: (i, 0))

    def gate(n):
        return pl.BlockSpec((tm, tn), lambda i, j: (i, (COL_GX + n * D_MODEL) // tn + j))

    return pl.pallas_call(
        _merge_kernel,
        grid=(rows // tm, D_MODEL // tn),
        in_specs=[
            br(), br(), br(), br(),
            pl.BlockSpec((tm, BRANCH_W), lambda i, j: (i, COL_LG // BRANCH_W)),
            gate(0), gate(1), gate(2),
            pl.BlockSpec((None, N_BRANCH, BRANCH_W, tn), lambda i, j: (li, 0, 0, j)),
        ],
        out_specs=pl.BlockSpec((tm, tn), lambda i, j: (i, j)),
        out_shape=jax.ShapeDtypeStruct((rows, D_MODEL), BF16),
        scratch_shapes=[pltpu.VMEM((tm, BRANCH_W), BF16)],
        compiler_params=_cparams(("arbitrary", "arbitrary")),
        name="merge",
    )(ym, yn, hf, hb, p, p, p, p, w_branch)


def _outproj_kernel(s_ref, w_ref, x_ref, g_ref, o_ref):
    o_ref[...] = x_ref[...] + g_ref[...] * _dot(s_ref[...], w_ref[...])


def _outproj(s, li, w_out, x, mod, row_fn):
    rows = x.shape[0]
    tn = TN_OUT
    tm = TM_OUT
    row_of_tile = row_fn(tm)
    return pl.pallas_call(
        _outproj_kernel,
        grid=(rows // tm, D_MODEL // tn),
        in_specs=[
            pl.BlockSpec((tm, D_MODEL), lambda i, j: (i, 0)),
            pl.BlockSpec((None, D_MODEL, tn), lambda i, j: (li, 0, j)),
            pl.BlockSpec((tm, tn), lambda i, j: (i, j)),
            pl.BlockSpec((None, None, None, 1, tn), lambda i, j: (li, row_of_tile(i), 5, 0, j)),
        ],
        out_specs=pl.BlockSpec((tm, tn), lambda i, j: (i, j)),
        out_shape=jax.ShapeDtypeStruct((rows, D_MODEL), F32),
        compiler_params=_cparams(("arbitrary", "arbitrary")),
        name="outproj",
    )(s, w_out, x, mod)


def _split_w_in(w):
    n_gate = 4 * M_HEADS
    o_mg = COL_NQ
    wbf = w.astype(BF16)
    wa = wbf[:, :, :o_mg]
    wb = wbf[:, :, o_mg + n_gate:]
    mg = w[:, :, o_mg:o_mg + n_gate].reshape(DEPTH, D_MODEL, 2, 2, M_HEADS)
    mg = jnp.transpose(mg, (0, 4, 2, 3, 1)).reshape(DEPTH, M_HEADS, 4, D_MODEL)
    wgt = jnp.pad(mg, ((0, 0), (0, 0), (0, GATE_R - 4), (0, 0)))
    return wa, wb, wgt.reshape(DEPTH, M_HEADS * GATE_R, D_MODEL).astype(BF16)


def _gate_bias(b_i, b_f):
    b = jnp.stack([b_i, b_f], axis=1)
    b = jnp.transpose(b, (2, 0, 1)).reshape(M_HEADS, 4).astype(F32)
    return jnp.pad(b, ((0, 0), (0, GATE_R - 4))).reshape(M_HEADS * GATE_R, 1)


def kernel(x, c, ctx, c_ctx, w_mod, b_mod, norm_ffn1, norm_mix, norm_ffn2, ffn1_w_in, ffn1_w_out, ffn2_w_in,
           ffn2_w_out, w_in, mlstm_b_i, mlstm_b_f, mlstm_gn, na_rpb, lru_conv_w, lru_conv_b, lru_w_a, lru_b_a,
           lru_w_x, lru_b_x, lru_lambda, w_branch, w_out, norm_final):
    assert x.shape == (BATCH, SEQ, D_MODEL) and ctx.shape == (BATCH, CTX_LEN, D_MODEL)
    xl = x.reshape(BATCH * SEQ, D_MODEL)
    xc = ctx.reshape(BATCH * CTX_LEN, D_MODEL)
    c_rows = jnp.concatenate([c, c_ctx[None, :], jnp.zeros((MOD_ROWS - BATCH - 1, D_MODEL), F32)], axis=0)
    mod = _modulation(c_rows, w_mod, b_mod).reshape(DEPTH, MOD_ROWS, N_MOD, 1, D_MODEL)
    cos, sin = _rope_tables()
    w1_in, w1_out = ffn1_w_in[0].astype(BF16), ffn1_w_out[0].astype(BF16)
    w_head, w_tail, w_gate = _split_w_in(w_in)
    wb = w_branch.astype(BF16)
    wo = w_out.astype(BF16)

    for li in range(DEPTH):
        last = li == DEPTH - 1
        wgt = w_gate[li]
        bgt = _gate_bias(mlstm_b_i[li], mlstm_b_f[li])
        bias = _na_bias(na_rpb[li])
        wa, wx = lru_w_a[li].astype(BF16), lru_w_x[li].astype(BF16)

        xl, w2_in, w2_out = _ffn(xl, li, mod, 0, norm_ffn1[li], w1_in, w1_out, _lat_row,
                                 cast_job=(ffn2_w_in, ffn2_w_out, li))
        xc = _ffn(xc, li, mod, 0, norm_ffn1[li], w1_in, w1_out, _ctx_row)

        p_lat, gt_lat = _inproj(xl, li, mod, norm_mix[li], w_head, w_tail, wgt, bgt, _lat_row)
        p_ctx, gt_ctx = _inproj(xc, li, mod, norm_mix[li], w_head, w_tail, wgt, bgt, _ctx_row)

        ym_l, ym_c = _mlstm(p_lat, p_ctx, gt_lat, gt_ctx, cos, sin, mlstm_gn[li])
        yn_l, yn_c = _na(p_lat, p_ctx, bias)
        hf_l, hb_l, hf_c, hb_c = _lru(p_lat, p_ctx, lru_conv_w[li], lru_conv_b[li], wa, lru_b_a[li], wx,
                                      lru_b_x[li], lru_lambda[li])

        s_l = _merge(ym_l, yn_l, hf_l, hb_l, p_lat, li, wb)
        xl = _outproj(s_l, li, wo, xl, mod, _lat_row)
        if last:
            xl = _ffn(xl, li, mod, 6, norm_ffn2[li], w2_in, w2_out, _lat_row, norm_final)
        else:
            xl, w1_in, w1_out = _ffn(xl, li, mod, 6, norm_ffn2[li], w2_in, w2_out, _lat_row,
                                     cast_job=(ffn1_w_in, ffn1_w_out, li + 1))
            s_c = _merge(ym_c, yn_c, hf_c, hb_c, p_ctx, li, wb)
            xc = _outproj(s_c, li, wo, xc, mod, _ctx_row)
            xc = _ffn(xc, li, mod, 6, norm_ffn2[li], w2_in, w2_out, _ctx_row)
    return xl.reshape(BATCH, SEQ, D_MODEL)
```

```python
import functools

import jax
import jax.numpy as jnp
from jax import lax
from jax.experimental import pallas as pl
from jax.experimental.pallas import tpu as pltpu

F32 = jnp.float32
BF16 = jnp.bfloat16

D_MODEL = 2048
BATCH = 4
SEQ = 2048
DEPTH = 2
GRID_W = 64
CTX_LEN = 256
EPS = 1e-6
D_FF = (11 * D_MODEL) // 4
N_MOD = 9
BRANCH_W = D_MODEL // 2
N_BRANCH = 3
M_HEADS = 4
M_HEAD_DIM = BRANCH_W // M_HEADS
ROPE_BASE = 10000.0
NA_HEADS = 8
NA_HEAD_DIM = BRANCH_W // NA_HEADS
NA_KH = 8
NA_KW = 16
LRU_BLOCKS = 8
LRU_BLOCK_DIM = BRANCH_W // LRU_BLOCKS
LRU_CONV = 4
LRU_C = 8.0

SUBLANES = 8
LANES = 128

MIX_GATE = 5
MOD_ROWS = 8
CTX_MOD_ROW = BATCH
TM_FFN = 512
TM_IN = 1024
TM_MERGE = 1024
TF = 512
TN_IN = 1024
TN_MERGE = 512
TN_OUT = 512
TN_MOD = 1024
M_CHUNK = 256
N_CHUNKS = (SEQ + CTX_LEN) // M_CHUNK
LRU_CHUNK = 256
LRU_STEPS = (SEQ + CTX_LEN) // LRU_CHUNK
GATE_R = SUBLANES

COL_MQ, COL_MK, COL_MV, COL_MO = 0, BRANCH_W, 2 * BRANCH_W, 3 * BRANCH_W
COL_NQ, COL_NK, COL_NV = 4 * BRANCH_W, 5 * BRANCH_W, 6 * BRANCH_W
COL_LX, COL_LG = 7 * BRANCH_W, 8 * BRANCH_W
COL_GX = 9 * BRANCH_W
P_COLS = COL_GX + N_BRANCH * D_MODEL

VMEM_BYTES = 64 * 1024 * 1024
VMEM_LIMIT = VMEM_BYTES - 8 * 1024 * 1024


def _cparams(sem):
    return pltpu.CompilerParams(dimension_semantics=sem, vmem_limit_bytes=VMEM_LIMIT)


def _sigmoid(x):
    return jax.nn.sigmoid(x)


def _log_sigmoid(x):
    return jnp.minimum(x, 0.0) - jnp.log1p(jnp.exp(-jnp.abs(x)))


def _softplus(x):
    return jnp.maximum(x, 0.0) + jnp.log1p(jnp.exp(-jnp.abs(x)))


def _dot(a, b):
    return jnp.dot(a, b, preferred_element_type=F32)


def _dot_nt(a, b):
    return lax.dot_general(a, b, (((1,), (1,)), ((), ())), preferred_element_type=F32)


def _dot_tn(a, b):
    return lax.dot_general(a, b, (((0,), (0,)), ((), ())), preferred_element_type=F32)


def _dot_exact(a, b):
    return jnp.dot(a, b, preferred_element_type=F32, precision=lax.Precision.HIGHEST)


def _dot_split(x, m):
    hi = x.astype(BF16)
    r1 = x - hi.astype(F32)
    mid = r1.astype(BF16)
    lo = (r1 - mid.astype(F32)).astype(BF16)
    return _dot(hi, m) + _dot(mid, m) + _dot(lo, m)


def _adaln(x, gain, shift, scale):
    y = x * lax.rsqrt(jnp.mean(x * x, axis=-1, keepdims=True) + EPS)
    return (y * gain) * (1.0 + scale) + shift


ADALN_ROWS = 16


def _adaln_into(h_ref, x_ref, gain, shift, scale):
    gs = gain * (1.0 + scale)

    def body(i, carry):
        rows = pl.ds(pl.multiple_of(i * ADALN_ROWS, ADALN_ROWS), ADALN_ROWS)
        x = x_ref[rows, :]
        y = x * lax.rsqrt(jnp.mean(x * x, axis=-1, keepdims=True) + EPS)
        h_ref[rows, :] = (y * gs + shift).astype(h_ref.dtype)
        return carry

    lax.fori_loop(0, x_ref.shape[0] // ADALN_ROWS, body, 0, unroll=4)


def _mod_kernel(c_ref, w_ref, b_ref, o_ref):
    c = c_ref[...]
    act = (c * _sigmoid(c)).astype(BF16)
    o_ref[...] = _dot(act, w_ref[...].astype(BF16)) + b_ref[...]


def _modulation(c_rows, w_mod, b_mod):
    n_cols = N_MOD * D_MODEL
    return pl.pallas_call(
        _mod_kernel,
        grid=(DEPTH, n_cols // TN_MOD),
        in_specs=[
            pl.BlockSpec((MOD_ROWS, D_MODEL), lambda l, j: (0, 0)),
            pl.BlockSpec((None, D_MODEL, TN_MOD), lambda l, j: (l, 0, j)),
            pl.BlockSpec((None, 1, TN_MOD), lambda l, j: (l, 0, j)),
        ],
        out_specs=pl.BlockSpec((None, MOD_ROWS, TN_MOD), lambda l, j: (l, 0, j)),
        out_shape=jax.ShapeDtypeStruct((DEPTH, MOD_ROWS, n_cols), F32),
        compiler_params=_cparams(("arbitrary", "arbitrary")),
        name="modulation",
    )(c_rows, w_mod, b_mod.reshape(DEPTH, 1, n_cols))


def _mod_spec(li, k, row_of_tile):
    return pl.BlockSpec((None, None, None, 1, D_MODEL), lambda i, j: (li, row_of_tile(i), k, 0, 0))


def _lat_row(tm):
    return lambda i: i // (SEQ // tm)


def _ctx_row(tm):
    return lambda i: CTX_MOD_ROW


def _ffn_kernel(*refs, final, cast, pre):
    refs = list(refs)
    x_ref, sh_ref, sc_ref, g_ref, ng_ref, wg_ref, wu_ref, wo_ref = refs[:8]
    del refs[:8]
    fg_ref = refs.pop(0) if final else None
    if pre:
        s_ref, wp_ref, gp_ref = refs[:3]
        del refs[:3]
    if cast:
        src_in, src_out = refs[:2]
        del refs[:2]
    o_ref = refs.pop(0)
    if cast:
        dst_in, dst_out = refs[:2]
        del refs[:2]
        dst_in[...] = src_in[...].astype(BF16)
        dst_out[...] = src_out[...].astype(BF16)
    h_scr = refs.pop(0)
    res_ref = refs.pop(0) if pre else x_ref
    f = pl.program_id(1)

    @pl.when(f == 0)
    def _():
        if pre:
            for n in range(D_MODEL // TN_OUT):
                cols = slice(n * TN_OUT, (n + 1) * TN_OUT)
                res_ref[:, cols] = x_ref[:, cols] + gp_ref[:, cols] * _dot(s_ref[...], wp_ref[:, cols])
        _adaln_into(h_scr, res_ref, ng_ref[...], sh_ref[...], sc_ref[...])
        o_ref[...] = jnp.zeros_like(o_ref)

    h = h_scr[...]
    gte = _dot(h, wg_ref[...])
    up = _dot(h, wu_ref[...])
    act = (gte * _sigmoid(gte) * up).astype(BF16)
    o_ref[...] += _dot(act, wo_ref[...])

    @pl.when(f == pl.num_programs(1) - 1)
    def _():
        out = res_ref[...] + 0.5 * g_ref[...] * o_ref[...]
        if final:
            out = out * lax.rsqrt(jnp.mean(out * out, axis=-1, keepdims=True) + EPS) * fg_ref[...]
        o_ref[...] = out


def _ffn(x, li, mod, k0, norm_g, w_in, w_out, row_fn, final_gain=None, cast_job=None, pre=None):
    rows = x.shape[0]
    nf = D_FF // TF
    tm = TM_FFN
    nt = rows // tm
    row_of_tile = row_fn(tm)
    final = final_gain is not None
    cast = cast_job is not None
    fused = pre is not None
    in_specs = [
        pl.BlockSpec((tm, D_MODEL), lambda i, f: (i, 0)),
        _mod_spec(li, k0, row_of_tile), _mod_spec(li, k0 + 1, row_of_tile), _mod_spec(li, k0 + 2, row_of_tile),
        pl.BlockSpec((1, D_MODEL), lambda i, f: (0, 0)),
        pl.BlockSpec((D_MODEL, TF), lambda i, f: (0, f)),
        pl.BlockSpec((D_MODEL, TF), lambda i, f: (0, f + nf)),
        pl.BlockSpec((TF, D_MODEL), lambda i, f: (f, 0)),
    ]
    args = [x, mod, mod, mod, norm_g.reshape(1, D_MODEL), w_in, w_in, w_out]
    if final:
        in_specs.append(pl.BlockSpec((1, D_MODEL), lambda i, f: (0, 0)))
        args.append(final_gain.reshape(1, D_MODEL))
    scratch = [pltpu.VMEM((tm, D_MODEL), BF16)]
    if fused:
        s, w_proj = pre
        in_specs += [pl.BlockSpec((tm, D_MODEL), lambda i, f: (i, 0)),
                     pl.BlockSpec((None, D_MODEL, D_MODEL), lambda i, f: (li, 0, 0)),
                     _mod_spec(li, MIX_GATE, row_of_tile)]
        args += [s, w_proj, mod]
        scratch.append(pltpu.VMEM((tm, D_MODEL), F32))
    out_specs = [pl.BlockSpec((tm, D_MODEL), lambda i, f: (i, 0))]
    out_shape = [jax.ShapeDtypeStruct((rows, D_MODEL), F32)]
    if cast:
        src_in, src_out, lc = cast_job
        in_blk = (D_MODEL // nt, 2 * D_FF // nf)
        out_blk = (D_FF // nf, D_MODEL // nt)
        in_specs += [pl.BlockSpec((None,) + in_blk, lambda i, f: (lc, i, f)),
                     pl.BlockSpec((None,) + out_blk, lambda i, f: (lc, f, i))]
        args += [src_in, src_out]
        out_specs += [pl.BlockSpec(in_blk, lambda i, f: (i, f)), pl.BlockSpec(out_blk, lambda i, f: (f, i))]
        out_shape += [jax.ShapeDtypeStruct((D_MODEL, 2 * D_FF), BF16), jax.ShapeDtypeStruct((D_FF, D_MODEL), BF16)]
    res = pl.pallas_call(
        functools.partial(_ffn_kernel, final=final, cast=cast, pre=fused),
        grid=(nt, nf),
        in_specs=in_specs,
        out_specs=out_specs,
        out_shape=out_shape,
        scratch_shapes=scratch,
        compiler_params=_cparams(("arbitrary", "arbitrary")),
        name="ffn_final" if final else "ffn",
    )(*args)
    return res if cast else res[0]


N_HEAD_TILES = COL_NQ // TN_IN


def _inproj_kernel(x_ref, sh_ref, sc_ref, ng_ref, wa_ref, wb_ref, wgt_ref, bgt_ref, p_ref, gt_ref, h_scr):
    j = pl.program_id(1)

    @pl.when(j == 0)
    def _():
        h = _adaln(x_ref[...], ng_ref[...], sh_ref[...], sc_ref[...]).astype(BF16)
        h_scr[...] = h
        for s in range(TM_IN // M_CHUNK):
            gt_ref[s] = _dot_nt(wgt_ref[...], h[s * M_CHUNK:(s + 1) * M_CHUNK, :]) + bgt_ref[...]

    @pl.when(j < N_HEAD_TILES)
    def _():
        p_ref[...] = _dot(h_scr[...], wa_ref[...]).astype(p_ref.dtype)

    @pl.when(j >= N_HEAD_TILES)
    def _():
        p_ref[...] = _dot(h_scr[...], wb_ref[...]).astype(p_ref.dtype)


def _inproj(x, li, mod, norm_g, wa, wb, wgt, bgt, row_fn):
    rows = x.shape[0]
    tm = TM_IN
    row_of_tile = row_fn(tm)
    return pl.pallas_call(
        _inproj_kernel,
        grid=(rows // tm, P_COLS // TN_IN),
        in_specs=[
            pl.BlockSpec((tm, D_MODEL), lambda i, j: (i, 0)),
            _mod_spec(li, 3, row_of_tile), _mod_spec(li, 4, row_of_tile),
            pl.BlockSpec((1, D_MODEL), lambda i, j: (0, 0)),
            pl.BlockSpec((None, D_MODEL, TN_IN), lambda i, j: (li, 0, jnp.minimum(j, N_HEAD_TILES - 1))),
            pl.BlockSpec((None, D_MODEL, TN_IN), lambda i, j: (li, 0, jnp.maximum(j - N_HEAD_TILES, 0))),
            pl.BlockSpec((M_HEADS * GATE_R, D_MODEL), lambda i, j: (0, 0)),
            pl.BlockSpec((M_HEADS * GATE_R, 1), lambda i, j: (0, 0)),
        ],
        out_specs=[
            pl.BlockSpec((tm, TN_IN), lambda i, j: (i, j)),
            pl.BlockSpec((tm // M_CHUNK, M_HEADS * GATE_R, M_CHUNK), lambda i, j: (i, 0, 0)),
        ],
        out_shape=[
            jax.ShapeDtypeStruct((rows, P_COLS), BF16),
            jax.ShapeDtypeStruct((rows // M_CHUNK, M_HEADS * GATE_R, M_CHUNK), F32),
        ],
        scratch_shapes=[pltpu.VMEM((tm, D_MODEL), BF16)],
        compiler_params=_cparams(("arbitrary", "arbitrary")),
        name="inproj",
    )(x, mod, mod, norm_g.reshape(1, D_MODEL), wa, wb, wgt, bgt)


def _rope(u, cs, sn):
    halves = []
    for j in range(M_HEAD_DIM // LANES):
        sl = slice(j * LANES, (j + 1) * LANES)
        uj = u[:, sl]
        halves.append(uj * cs[:, sl] + pltpu.roll(uj, LANES // 2, 1) * sn[:, sl])
    return jnp.concatenate(halves, axis=1)


def _mlstm_chunk(q, k, v, i_col, cum_col, i_row, cum_row, mask, last, c_mem, n_mem, m_prev):
    logw = jnp.where(mask, cum_col - cum_row + i_row, -jnp.inf)
    m_row = jnp.maximum(cum_col + m_prev, jnp.max(logw, axis=-1, keepdims=True))
    m_new = m_row[last:last + 1, :]
    cum_last = cum_col[last:last + 1, :]
    w_state = jnp.exp(cum_last - cum_col + i_col - m_new)
    decay = jnp.exp(cum_last + m_prev - m_new)
    inter = jnp.exp(cum_col + m_prev - m_row)
    kf = k.astype(F32)
    sc = _dot_nt(q, k) * jnp.exp(logw - m_row)
    num = inter * _dot_nt(q, c_mem.astype(BF16)) + _dot(sc.astype(BF16), v)
    den = inter * jnp.sum(q.astype(F32) * n_mem, axis=-1, keepdims=True) + jnp.sum(sc, axis=-1, keepdims=True)
    hid = num / jnp.maximum(jnp.abs(den), jnp.exp(-m_row))
    c_new = decay * c_mem + _dot_tn((w_state * v.astype(F32)).astype(BF16), k)
    n_new = decay * n_mem + jnp.sum(w_state * kf, axis=0, keepdims=True)
    return hid, c_new, n_new, m_new


def _mlstm_kernel(ql, kl, vl, ol, qc, kc, vc, oc, gtl, gtc, cos, sin, gn, yl, yc,
                  q_scr, k_scr, hf_scr, hb_scr, grow_scr, gcol_scr):
    ch = M_CHUNK
    kscale = M_HEAD_DIM ** -0.5

    def rows(c):
        return slice(c * ch, (c + 1) * ch)

    t_idx = lax.broadcasted_iota(jnp.int32, (ch, ch), 0)
    s_idx = lax.broadcasted_iota(jnp.int32, (ch, ch), 1)
    lower = s_idx <= t_idx
    upper = s_idx >= t_idx
    lower_b = lower.astype(BF16)
    upper_b = upper.astype(BF16)
    gate_row = lax.broadcasted_iota(jnp.int32, (GATE_R, ch), 0)
    lane_pad = jnp.zeros((LANES - GATE_R, ch), F32)

    for c in range(N_CHUNKS):
        if c == 0:
            q_scr[rows(0), :] = qc[...].astype(BF16)
            k_scr[rows(0), :] = (kc[...].astype(F32) * kscale).astype(BF16)
        else:
            lat = rows(c - 1)
            q_scr[rows(c), :] = _rope(ql[lat, :].astype(F32), cos[lat, :], sin[lat, :]).astype(BF16)
            k_scr[rows(c), :] = (_rope(kl[lat, :].astype(F32), cos[lat, :], sin[lat, :]) * kscale).astype(BF16)
        gt = gtc[0] if c == 0 else gtl[c - 1]
        lf = _log_sigmoid(gt)
        terms = jnp.where(gate_row == 1, _dot_split(lf, upper_b), jnp.where(gate_row == 3, _dot_split(lf, lower_b), gt))
        grow_scr[c] = terms
        gcol_scr[rows(c), :] = jnp.concatenate([terms, lane_pad], axis=0).T

    def values(c):
        return vc[...].astype(BF16) if c == 0 else vl[rows(c - 1), :].astype(BF16)

    def init():
        return (jnp.zeros((M_HEAD_DIM, M_HEAD_DIM), F32), jnp.zeros((1, M_HEAD_DIM), F32),
                jnp.full((1, 1), -jnp.inf, F32))

    def run(c, d, mask, last, state):
        g_rows = grow_scr[c]
        g_cols = gcol_scr[rows(c), :]
        i0, f0 = 2 * d, 2 * d + 1
        return _mlstm_chunk(q_scr[rows(c), :], k_scr[rows(c), :], values(c), g_cols[:, i0:i0 + 1],
                            g_cols[:, f0:f0 + 1], g_rows[i0:i0 + 1, :], g_rows[f0:f0 + 1, :], mask, last, *state)

    st_f, st_b = init(), init()
    for step in range(N_CHUNKS):
        cf = step
        cb = 0 if step == 0 else N_CHUNKS - step
        hid, *st_f = run(cf, 0, lower, ch - 1, st_f)
        hf_scr[rows(cf), :] = hid
        hid, *st_b = run(cb, 1, upper, 0, st_b)
        hb_scr[rows(cb), :] = hid

    gain = gn[...]
    for c in range(N_CHUNKS):
        hsum = hf_scr[rows(c), :] + hb_scr[rows(c), :]
        hn = hsum * lax.rsqrt(jnp.mean(hsum * hsum, axis=-1, keepdims=True) + EPS) * gain
        if c == 0:
            yc[...] = (_sigmoid(oc[...].astype(F32)) * hn).astype(yc.dtype)
        else:
            yl[rows(c - 1), :] = (_sigmoid(ol[rows(c - 1), :].astype(F32)) * hn).astype(yl.dtype)


def _mlstm(p_lat, p_ctx, gt_lat, gt_ctx, cos, sin, gn):
    hd = M_HEAD_DIM
    cb = lambda col: col // hd

    def lat(col):
        return pl.BlockSpec((SEQ, hd), lambda b, h: (b, cb(col) + h))

    def ctx(col):
        return pl.BlockSpec((CTX_LEN, hd), lambda b, h: (b, cb(col) + h))

    seq_chunks = SEQ // M_CHUNK
    ctx_chunks = CTX_LEN // M_CHUNK
    return pl.pallas_call(
        _mlstm_kernel,
        grid=(BATCH, M_HEADS),
        in_specs=[
            lat(COL_MQ), lat(COL_MK), lat(COL_MV), lat(COL_MO),
            ctx(COL_MQ), ctx(COL_MK), ctx(COL_MV), ctx(COL_MO),
            pl.BlockSpec((seq_chunks, GATE_R, M_CHUNK), lambda b, h: (b, h, 0)),
            pl.BlockSpec((ctx_chunks, GATE_R, M_CHUNK), lambda b, h: (b, h, 0)),
            pl.BlockSpec((SEQ, hd), lambda b, h: (0, 0)),
            pl.BlockSpec((SEQ, hd), lambda b, h: (0, 0)),
            pl.BlockSpec((1, hd), lambda b, h: (0, h)),
        ],
        out_specs=[
            pl.BlockSpec((SEQ, hd), lambda b, h: (b, h)),
            pl.BlockSpec((CTX_LEN, hd), lambda b, h: (b, h)),
        ],
        out_shape=[
            jax.ShapeDtypeStruct((BATCH * SEQ, BRANCH_W), BF16),
            jax.ShapeDtypeStruct((BATCH * CTX_LEN, BRANCH_W), BF16),
        ],
        scratch_shapes=[
            pltpu.VMEM((SEQ + CTX_LEN, hd), BF16), pltpu.VMEM((SEQ + CTX_LEN, hd), BF16),
            pltpu.VMEM((SEQ + CTX_LEN, hd), F32), pltpu.VMEM((SEQ + CTX_LEN, hd), F32),
            pltpu.VMEM((N_CHUNKS, GATE_R, M_CHUNK), F32), pltpu.VMEM((SEQ + CTX_LEN, LANES), F32),
        ],
        compiler_params=_cparams(("arbitrary", "arbitrary")),
        name="mlstm",
    )(p_lat, p_lat, p_lat, p_lat, p_ctx, p_ctx, p_ctx, p_ctx, gt_lat, gt_ctx, cos, sin, gn.reshape(1, BRANCH_W))


def _rope_tables():
    pos = jnp.arange(SEQ)
    nf = M_HEAD_DIM // 4
    inv = ROPE_BASE ** (-jnp.arange(nf, dtype=F32) / nf)
    ang_r = (pos // GRID_W).astype(F32)[:, None] * inv[None, :]
    ang_c = (pos % GRID_W).astype(F32)[:, None] * inv[None, :]
    cr, sr, cc, sc = jnp.cos(ang_r), jnp.sin(ang_r), jnp.cos(ang_c), jnp.sin(ang_c)
    return (jnp.concatenate([cr, cr, cc, cc], axis=-1), jnp.concatenate([-sr, sr, -sc, sc], axis=-1))


NA_GROUP = 2
NA_ROWS = SEQ // GRID_W
NA_QROWS = 4
NA_UNION = NA_KH + NA_QROWS
NA_QT = NA_QROWS * GRID_W
NA_KT = NA_UNION * GRID_W
NA_QGROUPS = NA_ROWS // NA_QROWS
NA_TYPES = 3


def _na_union_start(g):
    return max(0, min(NA_QROWS * g - NA_KH // 2, NA_ROWS - NA_UNION))


def _na_kernel(ql, kl, vl, qc, kc, vc, bias, yl, yc):
    scale = NA_HEAD_DIM ** -0.5
    kcs = [kc[:, hh * NA_HEAD_DIM:(hh + 1) * NA_HEAD_DIM].astype(BF16) for hh in range(NA_GROUP)]
    vcs = [vc[:, hh * NA_HEAD_DIM:(hh + 1) * NA_HEAD_DIM].astype(BF16) for hh in range(NA_GROUP)]

    def group_body(g, carry):
        u = jnp.clip(NA_QROWS * g - NA_KH // 2, 0, NA_ROWS - NA_UNION)
        kind = jnp.where(g == 0, 0, jnp.where(g == NA_QGROUPS - 1, NA_TYPES - 1, 1))
        q_rows = pl.ds(pl.multiple_of(g * NA_QT, NA_QT), NA_QT)
        k_rows = pl.ds(pl.multiple_of(u * GRID_W, GRID_W), NA_KT)
        outs = []
        for hh in range(NA_GROUP):
            sl = slice(hh * NA_HEAD_DIM, (hh + 1) * NA_HEAD_DIM)
            q = ql[q_rows, sl].astype(BF16)
            kw = kl[k_rows, sl].astype(BF16)
            vw = vl[k_rows, sl].astype(BF16)
            s_win = _dot_nt(q, kw) * scale + bias[hh, kind]
            s_ctx = _dot_nt(q, kcs[hh]) * scale
            m = jnp.maximum(jnp.max(s_win, axis=-1, keepdims=True), jnp.max(s_ctx, axis=-1, keepdims=True))
            p_win = jnp.exp(s_win - m)
            p_ctx = jnp.exp(s_ctx - m)
            denom = jnp.sum(p_win, axis=-1, keepdims=True) + jnp.sum(p_ctx, axis=-1, keepdims=True)
            o = _dot(p_win.astype(BF16), vw) + _dot(p_ctx.astype(BF16), vcs[hh])
            outs.append(o / denom)
        yl[q_rows, :] = jnp.concatenate(outs, axis=1).astype(yl.dtype)
        return carry

    lax.fori_loop(0, NA_QGROUPS, group_body, 0)

    outs = []
    for hh in range(NA_GROUP):
        sl = slice(hh * NA_HEAD_DIM, (hh + 1) * NA_HEAD_DIM)
        s = _dot_nt(qc[:, sl].astype(BF16), kcs[hh]) * scale
        p = jnp.exp(s - jnp.max(s, axis=-1, keepdims=True))
        outs.append(_dot(p.astype(BF16), vcs[hh]) / jnp.sum(p, axis=-1, keepdims=True))
    yc[...] = jnp.concatenate(outs, axis=1).astype(yc.dtype)


def _na_bias_kernel(rpb_ref, o_ref):
    shape = (LANES, GRID_W * GRID_W)
    n = lax.broadcasted_iota(jnp.int32, shape, 1)
    c = lax.broadcasted_iota(jnp.int32, shape, 0)
    q = jnp.right_shift(n, GRID_W.bit_length() - 1)
    k = jnp.bitwise_and(n, GRID_W - 1)
    dc = jnp.clip(k - q, -(NA_KW - 1), NA_KW - 1) + (NA_KW - 1)
    picked = _dot_exact(rpb_ref[...], (c == dc).astype(F32))
    col_start = jnp.clip(q - NA_KW // 2, 0, GRID_W - NA_KW)
    col_ok = (k >= col_start) & (k < col_start + NA_KW)
    o_ref[...] = jnp.where(col_ok, picked, -jnp.inf)


def _na_bias(rpb):
    assert _na_union_start(1) == 0 and _na_union_start(2) == NA_QROWS
    assert GRID_W & (GRID_W - 1) == 0 and NA_HEADS * (2 * NA_KH - 1) <= LANES and 2 * NA_KW - 1 <= LANES
    n_dr = 2 * NA_KH - 1
    flat = rpb.astype(F32).reshape(NA_HEADS * n_dr, 2 * NA_KW - 1)
    flat = jnp.pad(flat, ((0, LANES - NA_HEADS * n_dr), (0, LANES - (2 * NA_KW - 1))))
    t = pl.pallas_call(
        _na_bias_kernel,
        out_shape=jax.ShapeDtypeStruct((LANES, GRID_W * GRID_W), F32),
        name="natten_bias",
    )(flat)
    t = t[:NA_HEADS * n_dr].reshape(NA_HEADS, n_dr, GRID_W, GRID_W)
    tp = jnp.pad(t, ((0, 0), (NA_UNION, NA_UNION), (0, 0), (0, 0)), constant_values=-jnp.inf)
    runs = []
    for g in (0, 1, NA_QGROUPS - 1):
        u = _na_union_start(g)
        for i in range(NA_QROWS):
            r = NA_QROWS * g + i
            start = max(0, min(r - NA_KH // 2, NA_ROWS - NA_KH))
            dr0 = u - r + NA_KH - 1 + NA_UNION
            in_window = jnp.array([start <= u + j < start + NA_KH for j in range(NA_UNION)])
            runs.append(jnp.where(in_window[None, :, None, None], tp[:, dr0:dr0 + NA_UNION], -jnp.inf))
    table = jnp.stack(runs, axis=1).reshape(NA_HEADS, NA_TYPES, NA_QROWS, NA_UNION, GRID_W, GRID_W)
    return jnp.transpose(table, (0, 1, 2, 4, 3, 5)).reshape(NA_HEADS, NA_TYPES, NA_QT, NA_KT)


def _na(p_lat, p_ctx, bias):
    gw = NA_GROUP * NA_HEAD_DIM
    cb = lambda col: col // gw

    def lat(col):
        return pl.BlockSpec((SEQ, gw), lambda b, g: (b, cb(col) + g))

    def ctx(col):
        return pl.BlockSpec((CTX_LEN, gw), lambda b, g: (b, cb(col) + g))

    return pl.pallas_call(
        _na_kernel,
        grid=(BATCH, NA_HEADS // NA_GROUP),
        in_specs=[
            lat(COL_NQ), lat(COL_NK), lat(COL_NV), ctx(COL_NQ), ctx(COL_NK), ctx(COL_NV),
            pl.BlockSpec((NA_GROUP, NA_TYPES, NA_QT, NA_KT), lambda b, g: (g, 0, 0, 0)),
        ],
        out_specs=[
            pl.BlockSpec((SEQ, gw), lambda b, g: (b, g)),
            pl.BlockSpec((CTX_LEN, gw), lambda b, g: (b, g)),
        ],
        out_shape=[
            jax.ShapeDtypeStruct((BATCH * SEQ, BRANCH_W), BF16),
            jax.ShapeDtypeStruct((BATCH * CTX_LEN, BRANCH_W), BF16),
        ],
        compiler_params=_cparams(("arbitrary", "arbitrary")),
        name="natten",
    )(p_lat, p_lat, p_lat, p_ctx, p_ctx, p_ctx, bias)


LRU_HALO = 2 * SUBLANES


def _lru_kernel(xl, xc, cw, cb, wa, ba, wx, bx, lam, hfl, hbl, hfc, hbc,
                ext_f, ext_b, a_f, u_f, a_b, u_b, carry):
    s = pl.program_id(1)
    ch = LRU_CHUNK
    n_lat = SEQ // ch
    halo = LRU_HALO
    zeros_halo = jnp.zeros((halo, BRANCH_W), F32)

    def fill(ext, chunk):
        @pl.when(s == 0)
        def _():
            ext[0:halo, :] = zeros_halo
            ext[halo:halo + ch, :] = xc[...].astype(F32)
            ext[halo + ch:, :] = zeros_halo

        @pl.when(s > 0)
        def _():
            base = pl.multiple_of(chunk * ch, ch)
            ext[halo:halo + ch, :] = xl[pl.ds(base, ch), :].astype(F32)
            prev = xl[pl.ds(pl.multiple_of(jnp.maximum(base - halo, 0), halo), halo), :].astype(F32)
            nxt = xl[pl.ds(pl.multiple_of(jnp.minimum(base + ch, SEQ - halo), halo), halo), :].astype(F32)
            ext[0:halo, :] = jnp.where(chunk > 0, prev, 0.0)
            ext[halo + ch:, :] = jnp.where(chunk < n_lat - 1, nxt, 0.0)

    chunk_f = jnp.maximum(s - 1, 0)
    chunk_b = jnp.maximum(n_lat - s, 0)
    fill(ext_f, chunk_f)
    fill(ext_b, chunk_b)

    def coeffs(ext, d, a_out, u_out):
        xconv = cb[...] + ext[halo - 1:halo - 1 + ch, :] * cw[0:1, :]
        for j in range(1, LRU_CONV):
            xconv = xconv + ext[halo - 1 + j:halo - 1 + j + ch, :] * cw[j:j + 1, :]
        sp = _softplus(-lam[d:d + 1, :])
        for g in range(LRU_BLOCKS):
            sl = slice(g * LRU_BLOCK_DIM, (g + 1) * LRU_BLOCK_DIM)
            xg = xconv[:, sl]
            xb = xg.astype(BF16)
            r = _sigmoid(_dot(xb, wa[d, g]) + ba[d:d + 1, sl])
            i = _sigmoid(_dot(xb, wx[d, g]) + bx[d:d + 1, sl])
            log_a = -LRU_C * r * sp[:, sl]
            a = jnp.exp(log_a)
            a_out[:, sl] = a
            u_out[:, sl] = jnp.sqrt(jnp.tanh(-log_a) * (a * a + 1.0)) * i * xg

    coeffs(ext_f, 0, a_f, u_f)
    coeffs(ext_b, 1, a_b, u_b)

    @pl.when(s == 0)
    def _():
        carry[...] = jnp.zeros_like(carry)

    row = lax.broadcasted_iota(jnp.int32, (SUBLANES, BRANCH_W), 0)
    n_tiles = ch // SUBLANES

    def scan_tiles(hf_out, hb_out):
        def body(t, hs):
            h_f, h_b = hs
            r0 = pl.multiple_of(t * SUBLANES, SUBLANES)
            a = a_f[pl.ds(r0, SUBLANES), :]
            u = u_f[pl.ds(r0, SUBLANES), :]
            for d in (1, 2, 4):
                ok = row >= d
                u = jnp.where(ok, a * pltpu.roll(u, d, 0) + u, u)
                a = jnp.where(ok, a * pltpu.roll(a, d, 0), a)
            h = u + a * h_f
            hf_out[pl.ds(r0, SUBLANES), :] = h
            h_f = h[SUBLANES - 1:SUBLANES, :]
            r1 = pl.multiple_of((n_tiles - 1 - t) * SUBLANES, SUBLANES)
            a = a_b[pl.ds(r1, SUBLANES), :]
            u = u_b[pl.ds(r1, SUBLANES), :]
            for d in (1, 2, 4):
                ok = row < SUBLANES - d
                u = jnp.where(ok, a * pltpu.roll(u, SUBLANES - d, 0) + u, u)
                a = jnp.where(ok, a * pltpu.roll(a, SUBLANES - d, 0), a)
            h = u + a * h_b
            hb_out[pl.ds(r1, SUBLANES), :] = h
            h_b = h[0:1, :]
            return h_f, h_b

        h_f, h_b = lax.fori_loop(0, n_tiles, body, (carry[0:1, :], carry[1:2, :]))
        carry[0:1, :] = h_f
        carry[1:2, :] = h_b

    @pl.when(s == 0)
    def _():
        scan_tiles(hfc, hbc)

    @pl.when(s > 0)
    def _():
        scan_tiles(hfl, hbl)


def _lru(p_lat, p_ctx, conv_w, conv_b, w_a, b_a, w_x, b_x, lam):
    ch = LRU_CHUNK
    n_lat = SEQ // ch
    full = lambda shape: pl.BlockSpec(shape, lambda b, s: (0,) * len(shape))
    wblk = (2, LRU_BLOCKS, LRU_BLOCK_DIM, LRU_BLOCK_DIM)
    return pl.pallas_call(
        _lru_kernel,
        grid=(BATCH, LRU_STEPS),
        in_specs=[
            pl.BlockSpec((SEQ, BRANCH_W), lambda b, s: (b, COL_LX // BRANCH_W)),
            pl.BlockSpec((CTX_LEN, BRANCH_W), lambda b, s: (b, COL_LX // BRANCH_W)),
            full((LRU_CONV, BRANCH_W)), full((1, BRANCH_W)),
            full(wblk), full((2, BRANCH_W)), full(wblk), full((2, BRANCH_W)), full((2, BRANCH_W)),
        ],
        out_specs=[
            pl.BlockSpec((ch, BRANCH_W), lambda b, s: (b * n_lat + jnp.maximum(s - 1, 0), 0)),
            pl.BlockSpec((ch, BRANCH_W), lambda b, s: (b * n_lat + jnp.minimum(n_lat - s, n_lat - 1), 0)),
            pl.BlockSpec((CTX_LEN, BRANCH_W), lambda b, s: (b, 0)),
            pl.BlockSpec((CTX_LEN, BRANCH_W), lambda b, s: (b, 0)),
        ],
        out_shape=[
            jax.ShapeDtypeStruct((BATCH * SEQ, BRANCH_W), F32),
            jax.ShapeDtypeStruct((BATCH * SEQ, BRANCH_W), F32),
            jax.ShapeDtypeStruct((BATCH * CTX_LEN, BRANCH_W), F32),
            jax.ShapeDtypeStruct((BATCH * CTX_LEN, BRANCH_W), F32),
        ],
        scratch_shapes=[
            pltpu.VMEM((ch + 2 * LRU_HALO, BRANCH_W), F32), pltpu.VMEM((ch + 2 * LRU_HALO, BRANCH_W), F32),
            pltpu.VMEM((ch, BRANCH_W), F32), pltpu.VMEM((ch, BRANCH_W), F32),
            pltpu.VMEM((ch, BRANCH_W), F32), pltpu.VMEM((ch, BRANCH_W), F32),
            pltpu.VMEM((SUBLANES, BRANCH_W), F32),
        ],
        compiler_params=_cparams(("arbitrary", "arbitrary")),
        name="rglru",
    )(p_lat, p_ctx, conv_w, conv_b.reshape(1, BRANCH_W), w_a, b_a, w_x, b_x, lam)


def _merge_kernel(ym, yn, hf, hb, lg, g0, g1, g2, wb, o_ref, yl_s):
    j = pl.program_id(1)

    @pl.when(j == 0)
    def _():
        yl_s[...] = ((hf[...] + hb[...]) * jax.nn.gelu(lg[...].astype(F32))).astype(BF16)

    acc = _sigmoid(g0[...].astype(F32)) * _dot(ym[...], wb[0])
    acc = acc + _sigmoid(g1[...].astype(F32)) * _dot(yn[...], wb[1])
    acc = acc + _sigmoid(g2[...].astype(F32)) * _dot(yl_s[...], wb[2])
    o_ref[...] = acc.astype(BF16)


def _merge(ym, yn, hf, hb, p, li, w_branch):
    rows = ym.shape[0]
    tn = TN_MERGE
    tm = TM_MERGE
    br = lambda: pl.BlockSpec((tm, BRANCH_W), lambda i, j: (i, 0))

    def gate(n):
        return pl.BlockSpec((tm, tn), lambda i, j: (i, (COL_GX + n * D_MODEL) // tn + j))

    return pl.pallas_call(
        _merge_kernel,
        grid=(rows // tm, D_MODEL // tn),
        in_specs=[
            br(), br(), br(), br(),
            pl.BlockSpec((tm, BRANCH_W), lambda i, j: (i, COL_LG // BRANCH_W)),
            gate(0), gate(1), gate(2),
            pl.BlockSpec((None, N_BRANCH, BRANCH_W, tn), lambda i, j: (li, 0, 0, j)),
        ],
        out_specs=pl.BlockSpec((tm, tn), lambda i, j: (i, j)),
        out_shape=jax.ShapeDtypeStruct((rows, D_MODEL), BF16),
        scratch_shapes=[pltpu.VMEM((tm, BRANCH_W), BF16)],
        compiler_params=_cparams(("arbitrary", "arbitrary")),
        name="merge",
    )(ym, yn, hf, hb, p, p, p, p, w_branch)


def _split_w_in(w):
    n_gate = 4 * M_HEADS
    o_mg = COL_NQ
    wbf = w.astype(BF16)
    wa = wbf[:, :, :o_mg]
    wb = wbf[:, :, o_mg + n_gate:]
    mg = w[:, :, o_mg:o_mg + n_gate].reshape(DEPTH, D_MODEL, 2, 2, M_HEADS)
    mg = jnp.transpose(mg, (0, 4, 2, 3, 1)).reshape(DEPTH, M_HEADS, 4, D_MODEL)
    wgt = jnp.pad(mg, ((0, 0), (0, 0), (0, GATE_R - 4), (0, 0)))
    return wa, wb, wgt.reshape(DEPTH, M_HEADS * GATE_R, D_MODEL).astype(BF16)


def _gate_bias(b_i, b_f):
    b = jnp.stack([b_i, b_f], axis=1)
    b = jnp.transpose(b, (2, 0, 1)).reshape(M_HEADS, 4).astype(F32)
    return jnp.pad(b, ((0, 0), (0, GATE_R - 4))).reshape(M_HEADS * GATE_R, 1)


def kernel(x, c, ctx, c_ctx, w_mod, b_mod, norm_ffn1, norm_mix, norm_ffn2, ffn1_w_in, ffn1_w_out, ffn2_w_in,
           ffn2_w_out, w_in, mlstm_b_i, mlstm_b_f, mlstm_gn, na_rpb, lru_conv_w, lru_conv_b, lru_w_a, lru_b_a,
           lru_w_x, lru_b_x, lru_lambda, w_branch, w_out, norm_final):
    assert x.shape == (BATCH, SEQ, D_MODEL) and ctx.shape == (BATCH, CTX_LEN, D_MODEL)
    xl = x.reshape(BATCH * SEQ, D_MODEL)
    xc = ctx.reshape(BATCH * CTX_LEN, D_MODEL)
    c_rows = jnp.concatenate([c, c_ctx[None, :], jnp.zeros((MOD_ROWS - BATCH - 1, D_MODEL), F32)], axis=0)
    mod = _modulation(c_rows, w_mod, b_mod).reshape(DEPTH, MOD_ROWS, N_MOD, 1, D_MODEL)
    cos, sin = _rope_tables()
    w1_in, w1_out = ffn1_w_in[0].astype(BF16), ffn1_w_out[0].astype(BF16)
    w_head, w_tail, w_gate = _split_w_in(w_in)
    wb = w_branch.astype(BF16)
    wo = w_out.astype(BF16)

    for li in range(DEPTH):
        last = li == DEPTH - 1
        wgt = w_gate[li]
        bgt = _gate_bias(mlstm_b_i[li], mlstm_b_f[li])
        bias = _na_bias(na_rpb[li])
        wa, wx = lru_w_a[li].astype(BF16), lru_w_x[li].astype(BF16)

        xl, w2_in, w2_out = _ffn(xl, li, mod, 0, norm_ffn1[li], w1_in, w1_out, _lat_row,
                                 cast_job=(ffn2_w_in, ffn2_w_out, li))
        xc = _ffn(xc, li, mod, 0, norm_ffn1[li], w1_in, w1_out, _ctx_row)

        p_lat, gt_lat = _inproj(xl, li, mod, norm_mix[li], w_head, w_tail, wgt, bgt, _lat_row)
        p_ctx, gt_ctx = _inproj(xc, li, mod, norm_mix[li], w_head, w_tail, wgt, bgt, _ctx_row)

        ym_l, ym_c = _mlstm(p_lat, p_ctx, gt_lat, gt_ctx, cos, sin, mlstm_gn[li])
        yn_l, yn_c = _na(p_lat, p_ctx, bias)
        hf_l, hb_l, hf_c, hb_c = _lru(p_lat, p_ctx, lru_conv_w[li], lru_conv_b[li], wa, lru_b_a[li], wx,
                                      lru_b_x[li], lru_lambda[li])

        s_l = _merge(ym_l, yn_l, hf_l, hb_l, p_lat, li, wb)
        if last:
            xl = _ffn(xl, li, mod, 6, norm_ffn2[li], w2_in, w2_out, _lat_row, norm_final, pre=(s_l, wo))
        else:
            xl, w1_in, w1_out = _ffn(xl, li, mod, 6, norm_ffn2[li], w2_in, w2_out, _lat_row,
                                     cast_job=(ffn1_w_in, ffn1_w_out, li + 1), pre=(s_l, wo))
            s_c = _merge(ym_c, yn_c, hf_c, hb_c, p_ctx, li, wb)
            xc = _ffn(xc, li, mod, 6, norm_ffn2[li], w2_in, w2_out, _ctx_row, pre=(s_c, wo))
    return xl.reshape(BATCH, SEQ, D_MODEL)
```

```python
import functools

import jax
import jax.numpy as jnp
from jax import lax
from jax.experimental import pallas as pl
from jax.experimental.pallas import tpu as pltpu

F32 = jnp.float32
BF16 = jnp.bfloat16

D_MODEL = 2048
BATCH = 4
SEQ = 2048
DEPTH = 2
GRID_W = 64
CTX_LEN = 256
EPS = 1e-6
D_FF = (11 * D_MODEL) // 4
N_MOD = 9
BRANCH_W = D_MODEL // 2
N_BRANCH = 3
M_HEADS = 4
M_HEAD_DIM = BRANCH_W // M_HEADS
ROPE_BASE = 10000.0
NA_HEADS = 8
NA_HEAD_DIM = BRANCH_W // NA_HEADS
NA_KH = 8
NA_KW = 16
LRU_BLOCKS = 8
LRU_BLOCK_DIM = BRANCH_W // LRU_BLOCKS
LRU_CONV = 4
LRU_C = 8.0

SUBLANES = 8
LANES = 128

MIX_GATE = 5
MOD_ROWS = 8
CTX_MOD_ROW = BATCH
TM_FFN = 512
TM_IN = 1024
TM_MERGE = 1024
TF = 512
TN_IN = 1024
TN_MERGE = 512
TN_OUT = 512
TN_MOD = 1024
M_CHUNK = 256
N_CHUNKS = (SEQ + CTX_LEN) // M_CHUNK
LRU_CHUNK = 256
LRU_STEPS = (SEQ + CTX_LEN) // LRU_CHUNK
GATE_R = SUBLANES

COL_MQ, COL_MK, COL_MV, COL_MO = 0, BRANCH_W, 2 * BRANCH_W, 3 * BRANCH_W
COL_NQ, COL_NK, COL_NV = 4 * BRANCH_W, 5 * BRANCH_W, 6 * BRANCH_W
COL_LX, COL_LG = 7 * BRANCH_W, 8 * BRANCH_W
COL_GX = 9 * BRANCH_W
P_COLS = COL_GX + N_BRANCH * D_MODEL

VMEM_BYTES = 64 * 1024 * 1024
VMEM_LIMIT = VMEM_BYTES - 8 * 1024 * 1024


def _cparams(sem):
    return pltpu.CompilerParams(dimension_semantics=sem, vmem_limit_bytes=VMEM_LIMIT)


def _sigmoid(x):
    return jax.nn.sigmoid(x)


def _log_sigmoid(x):
    return jnp.minimum(x, 0.0) - jnp.log1p(jnp.exp(-jnp.abs(x)))


def _softplus(x):
    return jnp.maximum(x, 0.0) + jnp.log1p(jnp.exp(-jnp.abs(x)))


def _dot(a, b):
    return jnp.dot(a, b, preferred_element_type=F32)


def _dot_nt(a, b):
    return lax.dot_general(a, b, (((1,), (1,)), ((), ())), preferred_element_type=F32)


def _dot_tn(a, b):
    return lax.dot_general(a, b, (((0,), (0,)), ((), ())), preferred_element_type=F32)


def _dot_exact(a, b):
    return jnp.dot(a, b, preferred_element_type=F32, precision=lax.Precision.HIGHEST)


def _dot_split(x, m):
    hi = x.astype(BF16)
    r1 = x - hi.astype(F32)
    mid = r1.astype(BF16)
    lo = (r1 - mid.astype(F32)).astype(BF16)
    return _dot(hi, m) + _dot(mid, m) + _dot(lo, m)


def _adaln(x, gain, shift, scale):
    y = x * lax.rsqrt(jnp.mean(x * x, axis=-1, keepdims=True) + EPS)
    return (y * gain) * (1.0 + scale) + shift


ADALN_ROWS = 16


def _adaln_into(h_ref, x_ref, gain, shift, scale):
    gs = gain * (1.0 + scale)

    def body(i, carry):
        rows = pl.ds(pl.multiple_of(i * ADALN_ROWS, ADALN_ROWS), ADALN_ROWS)
        x = x_ref[rows, :]
        y = x * lax.rsqrt(jnp.mean(x * x, axis=-1, keepdims=True) + EPS)
        h_ref[rows, :] = (y * gs + shift).astype(h_ref.dtype)
        return carry

    lax.fori_loop(0, x_ref.shape[0] // ADALN_ROWS, body, 0, unroll=4)


def _mod_kernel(c_ref, w_ref, b_ref, o_ref):
    c = c_ref[...]
    act = (c * _sigmoid(c)).astype(BF16)
    o_ref[...] = _dot(act, w_ref[...].astype(BF16)) + b_ref[...]


def _modulation(c_rows, w_mod, b_mod):
    n_cols = N_MOD * D_MODEL
    return pl.pallas_call(
        _mod_kernel,
        grid=(DEPTH, n_cols // TN_MOD),
        in_specs=[
            pl.BlockSpec((MOD_ROWS, D_MODEL), lambda l, j: (0, 0)),
            pl.BlockSpec((None, D_MODEL, TN_MOD), lambda l, j: (l, 0, j)),
            pl.BlockSpec((None, 1, TN_MOD), lambda l, j: (l, 0, j)),
        ],
        out_specs=pl.BlockSpec((None, MOD_ROWS, TN_MOD), lambda l, j: (l, 0, j)),
        out_shape=jax.ShapeDtypeStruct((DEPTH, MOD_ROWS, n_cols), F32),
        compiler_params=_cparams(("arbitrary", "arbitrary")),
        name="modulation",
    )(c_rows, w_mod, b_mod.reshape(DEPTH, 1, n_cols))


def _mod_spec(li, k, row_of_tile):
    return pl.BlockSpec((None, None, None, 1, D_MODEL), lambda i, j: (li, row_of_tile(i), k, 0, 0))


def _lat_row(tm):
    return lambda i: i // (SEQ // tm)


def _ctx_row(tm):
    return lambda i: CTX_MOD_ROW


def _ffn_kernel(*refs, final, cast, pre):
    refs = list(refs)
    x_ref, sh_ref, sc_ref, g_ref, ng_ref, wg_ref, wu_ref, wo_ref = refs[:8]
    del refs[:8]
    fg_ref = refs.pop(0) if final else None
    if pre:
        s_ref, wp_ref, gp_ref = refs[:3]
        del refs[:3]
    if cast:
        src_in, src_out = refs[:2]
        del refs[:2]
    o_ref = refs.pop(0)
    if cast:
        dst_in, dst_out = refs[:2]
        del refs[:2]
        dst_in[...] = src_in[...].astype(BF16)
        dst_out[...] = src_out[...].astype(BF16)
    h_scr = refs.pop(0)
    res_ref = refs.pop(0) if pre else x_ref
    f = pl.program_id(1)

    @pl.when(f == 0)
    def _():
        if pre:
            for n in range(D_MODEL // TN_OUT):
                cols = slice(n * TN_OUT, (n + 1) * TN_OUT)
                res_ref[:, cols] = x_ref[:, cols] + gp_ref[:, cols] * _dot(s_ref[...], wp_ref[:, cols])
        _adaln_into(h_scr, res_ref, ng_ref[...], sh_ref[...], sc_ref[...])
        o_ref[...] = jnp.zeros_like(o_ref)

    h = h_scr[...]
    gte = _dot(h, wg_ref[...])
    up = _dot(h, wu_ref[...])
    act = (gte * _sigmoid(gte) * up).astype(BF16)
    o_ref[...] += _dot(act, wo_ref[...])

    @pl.when(f == pl.num_programs(1) - 1)
    def _():
        out = res_ref[...] + 0.5 * g_ref[...] * o_ref[...]
        if final:
            out = out * lax.rsqrt(jnp.mean(out * out, axis=-1, keepdims=True) + EPS) * fg_ref[...]
        o_ref[...] = out


def _ffn(x, li, mod, k0, norm_g, w_in, w_out, row_fn, final_gain=None, cast_job=None, pre=None):
    rows = x.shape[0]
    nf = D_FF // TF
    tm = TM_FFN
    nt = rows // tm
    row_of_tile = row_fn(tm)
    final = final_gain is not None
    cast = cast_job is not None
    fused = pre is not None
    in_specs = [
        pl.BlockSpec((tm, D_MODEL), lambda i, f: (i, 0)),
        _mod_spec(li, k0, row_of_tile), _mod_spec(li, k0 + 1, row_of_tile), _mod_spec(li, k0 + 2, row_of_tile),
        pl.BlockSpec((1, D_MODEL), lambda i, f: (0, 0)),
        pl.BlockSpec((D_MODEL, TF), lambda i, f: (0, f)),
        pl.BlockSpec((D_MODEL, TF), lambda i, f: (0, f + nf)),
        pl.BlockSpec((TF, D_MODEL), lambda i, f: (f, 0)),
    ]
    args = [x, mod, mod, mod, norm_g.reshape(1, D_MODEL), w_in, w_in, w_out]
    if final:
        in_specs.append(pl.BlockSpec((1, D_MODEL), lambda i, f: (0, 0)))
        args.append(final_gain.reshape(1, D_MODEL))
    scratch = [pltpu.VMEM((tm, D_MODEL), BF16)]
    if fused:
        s, w_proj = pre
        in_specs += [pl.BlockSpec((tm, D_MODEL), lambda i, f: (i, 0)),
                     pl.BlockSpec((None, D_MODEL, D_MODEL), lambda i, f: (li, 0, 0)),
                     _mod_spec(li, MIX_GATE, row_of_tile)]
        args += [s, w_proj, mod]
        scratch.append(pltpu.VMEM((tm, D_MODEL), F32))
    out_specs = [pl.BlockSpec((tm, D_MODEL), lambda i, f: (i, 0))]
    out_shape = [jax.ShapeDtypeStruct((rows, D_MODEL), F32)]
    if cast:
        src_in, src_out, lc = cast_job
        in_blk = (D_MODEL // nt, 2 * D_FF // nf)
        out_blk = (D_FF // nf, D_MODEL // nt)
        in_specs += [pl.BlockSpec((None,) + in_blk, lambda i, f: (lc, i, f)),
                     pl.BlockSpec((None,) + out_blk, lambda i, f: (lc, f, i))]
        args += [src_in, src_out]
        out_specs += [pl.BlockSpec(in_blk, lambda i, f: (i, f)), pl.BlockSpec(out_blk, lambda i, f: (f, i))]
        out_shape += [jax.ShapeDtypeStruct((D_MODEL, 2 * D_FF), BF16), jax.ShapeDtypeStruct((D_FF, D_MODEL), BF16)]
    res = pl.pallas_call(
        functools.partial(_ffn_kernel, final=final, cast=cast, pre=fused),
        grid=(nt, nf),
        in_specs=in_specs,
        out_specs=out_specs,
        out_shape=out_shape,
        scratch_shapes=scratch,
        compiler_params=_cparams(("arbitrary", "arbitrary")),
        name="ffn_final" if final else "ffn",
    )(*args)
    return res if cast else res[0]


N_HEAD_TILES = COL_NQ // TN_IN


def _inproj_kernel(x_ref, sh_ref, sc_ref, ng_ref, wa_ref, wb_ref, wgt_ref, bgt_ref, p_ref, gt_ref, h_scr):
    j = pl.program_id(1)

    @pl.when(j == 0)
    def _():
        h = _adaln(x_ref[...], ng_ref[...], sh_ref[...], sc_ref[...]).astype(BF16)
        h_scr[...] = h
        for s in range(TM_IN // M_CHUNK):
            gt_ref[s] = _dot_nt(wgt_ref[...], h[s * M_CHUNK:(s + 1) * M_CHUNK, :]) + bgt_ref[...]

    @pl.when(j < N_HEAD_TILES)
    def _():
        p_ref[...] = _dot(h_scr[...], wa_ref[...]).astype(p_ref.dtype)

    @pl.when(j >= N_HEAD_TILES)
    def _():
        p_ref[...] = _dot(h_scr[...], wb_ref[...]).astype(p_ref.dtype)


def _inproj(x, li, mod, norm_g, wa, wb, wgt, bgt, row_fn):
    rows = x.shape[0]
    tm = TM_IN
    row_of_tile = row_fn(tm)
    return pl.pallas_call(
        _inproj_kernel,
        grid=(rows // tm, P_COLS // TN_IN),
        in_specs=[
            pl.BlockSpec((tm, D_MODEL), lambda i, j: (i, 0)),
            _mod_spec(li, 3, row_of_tile), _mod_spec(li, 4, row_of_tile),
            pl.BlockSpec((1, D_MODEL), lambda i, j: (0, 0)),
            pl.BlockSpec((None, D_MODEL, TN_IN), lambda i, j: (li, 0, jnp.minimum(j, N_HEAD_TILES - 1))),
            pl.BlockSpec((None, D_MODEL, TN_IN), lambda i, j: (li, 0, jnp.maximum(j - N_HEAD_TILES, 0))),
            pl.BlockSpec((M_HEADS * GATE_R, D_MODEL), lambda i, j: (0, 0)),
            pl.BlockSpec((M_HEADS * GATE_R, 1), lambda i, j: (0, 0)),
        ],
        out_specs=[
            pl.BlockSpec((tm, TN_IN), lambda i, j: (i, j)),
            pl.BlockSpec((tm // M_CHUNK, M_HEADS * GATE_R, M_CHUNK), lambda i, j: (i, 0, 0)),
        ],
        out_shape=[
            jax.ShapeDtypeStruct((rows, P_COLS), BF16),
            jax.ShapeDtypeStruct((rows // M_CHUNK, M_HEADS * GATE_R, M_CHUNK), F32),
        ],
        scratch_shapes=[pltpu.VMEM((tm, D_MODEL), BF16)],
        compiler_params=_cparams(("arbitrary", "arbitrary")),
        name="inproj",
    )(x, mod, mod, norm_g.reshape(1, D_MODEL), wa, wb, wgt, bgt)


def _rope(u, cs, sn):
    halves = []
    for j in range(M_HEAD_DIM // LANES):
        sl = slice(j * LANES, (j + 1) * LANES)
        uj = u[:, sl]
        halves.append(uj * cs[:, sl] + pltpu.roll(uj, LANES // 2, 1) * sn[:, sl])
    return jnp.concatenate(halves, axis=1)


def _mlstm_chunk(q, k, v, i_col, cum_col, i_row, cum_row, mask, last, c_mem, n_mem, m_prev):
    logw = jnp.where(mask, cum_col - cum_row + i_row, -jnp.inf)
    m_row = jnp.maximum(cum_col + m_prev, jnp.max(logw, axis=-1, keepdims=True))
    m_new = m_row[last:last + 1, :]
    cum_last = cum_col[last:last + 1, :]
    w_state = jnp.exp(cum_last - cum_col + i_col - m_new)
    decay = jnp.exp(cum_last + m_prev - m_new)
    inter = jnp.exp(cum_col + m_prev - m_row)
    kf = k.astype(F32)
    sc = _dot_nt(q, k) * jnp.exp(logw - m_row)
    num = inter * _dot_nt(q, c_mem.astype(BF16)) + _dot(sc.astype(BF16), v)
    den = inter * jnp.sum(q.astype(F32) * n_mem, axis=-1, keepdims=True) + jnp.sum(sc, axis=-1, keepdims=True)
    hid = num / jnp.maximum(jnp.abs(den), jnp.exp(-m_row))
    c_new = decay * c_mem + _dot_tn((w_state * v.astype(F32)).astype(BF16), k)
    n_new = decay * n_mem + jnp.sum(w_state * kf, axis=0, keepdims=True)
    return hid, c_new, n_new, m_new


def _mlstm_kernel(ql, kl, vl, ol, qc, kc, vc, oc, gtl, gtc, cos, sin, gn, yl, yc,
                  q_scr, k_scr, hf_scr, hb_scr, grow_scr, gcol_scr):
    ch = M_CHUNK
    kscale = M_HEAD_DIM ** -0.5

    def rows(c):
        return slice(c * ch, (c + 1) * ch)

    t_idx = lax.broadcasted_iota(jnp.int32, (ch, ch), 0)
    s_idx = lax.broadcasted_iota(jnp.int32, (ch, ch), 1)
    lower = s_idx <= t_idx
    upper = s_idx >= t_idx
    lower_b = lower.astype(BF16)
    upper_b = upper.astype(BF16)
    gate_row = lax.broadcasted_iota(jnp.int32, (GATE_R, ch), 0)
    lane_pad = jnp.zeros((LANES - GATE_R, ch), F32)

    for c in range(N_CHUNKS):
        if c == 0:
            q_scr[rows(0), :] = qc[...].astype(BF16)
            k_scr[rows(0), :] = (kc[...].astype(F32) * kscale).astype(BF16)
        else:
            lat = rows(c - 1)
            q_scr[rows(c), :] = _rope(ql[lat, :].astype(F32), cos[lat, :], sin[lat, :]).astype(BF16)
            k_scr[rows(c), :] = (_rope(kl[lat, :].astype(F32), cos[lat, :], sin[lat, :]) * kscale).astype(BF16)
        gt = gtc[0] if c == 0 else gtl[c - 1]
        lf = _log_sigmoid(gt)
        terms = jnp.where(gate_row == 1, _dot_split(lf, upper_b), jnp.where(gate_row == 3, _dot_split(lf, lower_b), gt))
        grow_scr[c] = terms
        gcol_scr[rows(c), :] = jnp.concatenate([terms, lane_pad], axis=0).T

    def values(c):
        return vc[...].astype(BF16) if c == 0 else vl[rows(c - 1), :].astype(BF16)

    def init():
        return (jnp.zeros((M_HEAD_DIM, M_HEAD_DIM), F32), jnp.zeros((1, M_HEAD_DIM), F32),
                jnp.full((1, 1), -jnp.inf, F32))

    def run(c, d, mask, last, state):
        g_rows = grow_scr[c]
        g_cols = gcol_scr[rows(c), :]
        i0, f0 = 2 * d, 2 * d + 1
        return _mlstm_chunk(q_scr[rows(c), :], k_scr[rows(c), :], values(c), g_cols[:, i0:i0 + 1],
                            g_cols[:, f0:f0 + 1], g_rows[i0:i0 + 1, :], g_rows[f0:f0 + 1, :], mask, last, *state)

    st_f, st_b = init(), init()
    for step in range(N_CHUNKS):
        cf = step
        cb = 0 if step == 0 else N_CHUNKS - step
        hid, *st_f = run(cf, 0, lower, ch - 1, st_f)
        hf_scr[rows(cf), :] = hid
        hid, *st_b = run(cb, 1, upper, 0, st_b)
        hb_scr[rows(cb), :] = hid

    gain = gn[...]
    for c in range(N_CHUNKS):
        hsum = hf_scr[rows(c), :] + hb_scr[rows(c), :]
        hn = hsum * lax.rsqrt(jnp.mean(hsum * hsum, axis=-1, keepdims=True) + EPS) * gain
        if c == 0:
            yc[...] = (_sigmoid(oc[...].astype(F32)) * hn).astype(yc.dtype)
        else:
            yl[rows(c - 1), :] = (_sigmoid(ol[rows(c - 1), :].astype(F32)) * hn).astype(yl.dtype)


def _mlstm(p_lat, p_ctx, gt_lat, gt_ctx, cos, sin, gn):
    hd = M_HEAD_DIM
    cb = lambda col: col // hd

    def lat(col):
        return pl.BlockSpec((SEQ, hd), lambda b, h: (b, cb(col) + h))

    def ctx(col):
        return pl.BlockSpec((CTX_LEN, hd), lambda b, h: (b, cb(col) + h))

    seq_chunks = SEQ // M_CHUNK
    ctx_chunks = CTX_LEN // M_CHUNK
    return pl.pallas_call(
        _mlstm_kernel,
        grid=(BATCH, M_HEADS),
        in_specs=[
            lat(COL_MQ), lat(COL_MK), lat(COL_MV), lat(COL_MO),
            ctx(COL_MQ), ctx(COL_MK), ctx(COL_MV), ctx(COL_MO),
            pl.BlockSpec((seq_chunks, GATE_R, M_CHUNK), lambda b, h: (b, h, 0)),
            pl.BlockSpec((ctx_chunks, GATE_R, M_CHUNK), lambda b, h: (b, h, 0)),
            pl.BlockSpec((SEQ, hd), lambda b, h: (0, 0)),
            pl.BlockSpec((SEQ, hd), lambda b, h: (0, 0)),
            pl.BlockSpec((1, hd), lambda b, h: (0, h)),
        ],
        out_specs=[
            pl.BlockSpec((SEQ, hd), lambda b, h: (b, h)),
            pl.BlockSpec((CTX_LEN, hd), lambda b, h: (b, h)),
        ],
        out_shape=[
            jax.ShapeDtypeStruct((BATCH * SEQ, BRANCH_W), BF16),
            jax.ShapeDtypeStruct((BATCH * CTX_LEN, BRANCH_W), BF16),
        ],
        scratch_shapes=[
            pltpu.VMEM((SEQ + CTX_LEN, hd), BF16), pltpu.VMEM((SEQ + CTX_LEN, hd), BF16),
            pltpu.VMEM((SEQ + CTX_LEN, hd), F32), pltpu.VMEM((SEQ + CTX_LEN, hd), F32),
            pltpu.VMEM((N_CHUNKS, GATE_R, M_CHUNK), F32), pltpu.VMEM((SEQ + CTX_LEN, LANES), F32),
        ],
        compiler_params=_cparams(("arbitrary", "arbitrary")),
        name="mlstm",
    )(p_lat, p_lat, p_lat, p_lat, p_ctx, p_ctx, p_ctx, p_ctx, gt_lat, gt_ctx, cos, sin, gn.reshape(1, BRANCH_W))


def _rope_tables():
    pos = jnp.arange(SEQ)
    nf = M_HEAD_DIM // 4
    inv = ROPE_BASE ** (-jnp.arange(nf, dtype=F32) / nf)
    ang_r = (pos // GRID_W).astype(F32)[:, None] * inv[None, :]
    ang_c = (pos % GRID_W).astype(F32)[:, None] * inv[None, :]
    cr, sr, cc, sc = jnp.cos(ang_r), jnp.sin(ang_r), jnp.cos(ang_c), jnp.sin(ang_c)
    return (jnp.concatenate([cr, cr, cc, cc], axis=-1), jnp.concatenate([-sr, sr, -sc, sc], axis=-1))


NA_GROUP = 2
NA_ROWS = SEQ // GRID_W
NA_QROWS = 4
NA_UNION = NA_KH + NA_QROWS
NA_QT = NA_QROWS * GRID_W
NA_KT = NA_UNION * GRID_W
NA_QGROUPS = NA_ROWS // NA_QROWS
NA_TYPES = 3


def _na_union_start(g):
    return max(0, min(NA_QROWS * g - NA_KH // 2, NA_ROWS - NA_UNION))


def _na_kernel(ql, kl, vl, qc, kc, vc, bias, yl, yc):
    scale = NA_HEAD_DIM ** -0.5
    kcs = [kc[:, hh * NA_HEAD_DIM:(hh + 1) * NA_HEAD_DIM].astype(BF16) for hh in range(NA_GROUP)]
    vcs = [vc[:, hh * NA_HEAD_DIM:(hh + 1) * NA_HEAD_DIM].astype(BF16) for hh in range(NA_GROUP)]

    def group_body(g, carry):
        u = jnp.clip(NA_QROWS * g - NA_KH // 2, 0, NA_ROWS - NA_UNION)
        kind = jnp.where(g == 0, 0, jnp.where(g == NA_QGROUPS - 1, NA_TYPES - 1, 1))
        q_rows = pl.ds(pl.multiple_of(g * NA_QT, NA_QT), NA_QT)
        k_rows = pl.ds(pl.multiple_of(u * GRID_W, GRID_W), NA_KT)
        outs = []
        for hh in range(NA_GROUP):
            sl = slice(hh * NA_HEAD_DIM, (hh + 1) * NA_HEAD_DIM)
            q = ql[q_rows, sl].astype(BF16)
            kw = kl[k_rows, sl].astype(BF16)
            vw = vl[k_rows, sl].astype(BF16)
            s_win = _dot_nt(q, kw) * scale + bias[hh, kind]
            s_ctx = _dot_nt(q, kcs[hh]) * scale
            m = jnp.maximum(jnp.max(s_win, axis=-1, keepdims=True), jnp.max(s_ctx, axis=-1, keepdims=True))
            p_win = jnp.exp(s_win - m)
            p_ctx = jnp.exp(s_ctx - m)
            denom = jnp.sum(p_win, axis=-1, keepdims=True) + jnp.sum(p_ctx, axis=-1, keepdims=True)
            o = _dot(p_win.astype(BF16), vw) + _dot(p_ctx.astype(BF16), vcs[hh])
            outs.append(o / denom)
        yl[q_rows, :] = jnp.concatenate(outs, axis=1).astype(yl.dtype)
        return carry

    lax.fori_loop(0, NA_QGROUPS, group_body, 0)

    outs = []
    for hh in range(NA_GROUP):
        sl = slice(hh * NA_HEAD_DIM, (hh + 1) * NA_HEAD_DIM)
        s = _dot_nt(qc[:, sl].astype(BF16), kcs[hh]) * scale
        p = jnp.exp(s - jnp.max(s, axis=-1, keepdims=True))
        outs.append(_dot(p.astype(BF16), vcs[hh]) / jnp.sum(p, axis=-1, keepdims=True))
    yc[...] = jnp.concatenate(outs, axis=1).astype(yc.dtype)


def _na_bias_kernel(rpb_ref, o_ref):
    shape = (LANES, GRID_W * GRID_W)
    n = lax.broadcasted_iota(jnp.int32, shape, 1)
    c = lax.broadcasted_iota(jnp.int32, shape, 0)
    q = jnp.right_shift(n, GRID_W.bit_length() - 1)
    k = jnp.bitwise_and(n, GRID_W - 1)
    dc = jnp.clip(k - q, -(NA_KW - 1), NA_KW - 1) + (NA_KW - 1)
    picked = _dot_exact(rpb_ref[...], (c == dc).astype(F32))
    col_start = jnp.clip(q - NA_KW // 2, 0, GRID_W - NA_KW)
    col_ok = (k >= col_start) & (k < col_start + NA_KW)
    o_ref[...] = jnp.where(col_ok, picked, -jnp.inf)


def _na_bias(rpb):
    assert _na_union_start(1) == 0 and _na_union_start(2) == NA_QROWS
    assert GRID_W & (GRID_W - 1) == 0 and NA_HEADS * (2 * NA_KH - 1) <= LANES and 2 * NA_KW - 1 <= LANES
    n_dr = 2 * NA_KH - 1
    flat = rpb.astype(F32).reshape(NA_HEADS * n_dr, 2 * NA_KW - 1)
    flat = jnp.pad(flat, ((0, LANES - NA_HEADS * n_dr), (0, LANES - (2 * NA_KW - 1))))
    t = pl.pallas_call(
        _na_bias_kernel,
        out_shape=jax.ShapeDtypeStruct((LANES, GRID_W * GRID_W), F32),
        name="natten_bias",
    )(flat)
    t = t[:NA_HEADS * n_dr].reshape(NA_HEADS, n_dr, GRID_W, GRID_W)
    tp = jnp.pad(t, ((0, 0), (NA_UNION, NA_UNION), (0, 0), (0, 0)), constant_values=-jnp.inf)
    runs = []
    for g in (0, 1, NA_QGROUPS - 1):
        u = _na_union_start(g)
        for i in range(NA_QROWS):
            r = NA_QROWS * g + i
            start = max(0, min(r - NA_KH // 2, NA_ROWS - NA_KH))
            dr0 = u - r + NA_KH - 1 + NA_UNION
            in_window = jnp.array([start <= u + j < start + NA_KH for j in range(NA_UNION)])
            runs.append(jnp.where(in_window[None, :, None, None], tp[:, dr0:dr0 + NA_UNION], -jnp.inf))
    table = jnp.stack(runs, axis=1).reshape(NA_HEADS, NA_TYPES, NA_QROWS, NA_UNION, GRID_W, GRID_W)
    return jnp.transpose(table, (0, 1, 2, 4, 3, 5)).reshape(NA_HEADS, NA_TYPES, NA_QT, NA_KT)


def _na(p_lat, p_ctx, bias):
    gw = NA_GROUP * NA_HEAD_DIM
    cb = lambda col: col // gw

    def lat(col):
        return pl.BlockSpec((SEQ, gw), lambda b, g: (b, cb(col) + g))

    def ctx(col):
        return pl.BlockSpec((CTX_LEN, gw), lambda b, g: (b, cb(col) + g))

    return pl.pallas_call(
        _na_kernel,
        grid=(BATCH, NA_HEADS // NA_GROUP),
        in_specs=[
            lat(COL_NQ), lat(COL_NK), lat(COL_NV), ctx(COL_NQ), ctx(COL_NK), ctx(COL_NV),
            pl.BlockSpec((NA_GROUP, NA_TYPES, NA_QT, NA_KT), lambda b, g: (g, 0, 0, 0)),
        ],
        out_specs=[
            pl.BlockSpec((SEQ, gw), lambda b, g: (b, g)),
            pl.BlockSpec((CTX_LEN, gw), lambda b, g: (b, g)),
        ],
        out_shape=[
            jax.ShapeDtypeStruct((BATCH * SEQ, BRANCH_W), BF16),
            jax.ShapeDtypeStruct((BATCH * CTX_LEN, BRANCH_W), BF16),
        ],
        compiler_params=_cparams(("arbitrary", "arbitrary")),
        name="natten",
    )(p_lat, p_lat, p_lat, p_ctx, p_ctx, p_ctx, bias)


LRU_HALO = 2 * SUBLANES


def _lru_kernel(xl, xc, cw, cb, wa, ba, wx, bx, lam, hfl, hbl, hfc, hbc,
                ext_f, ext_b, a_f, u_f, a_b, u_b, carry):
    s = pl.program_id(1)
    ch = LRU_CHUNK
    n_lat = SEQ // ch
    halo = LRU_HALO
    zeros_halo = jnp.zeros((halo, BRANCH_W), F32)

    def fill(ext, chunk):
        @pl.when(s == 0)
        def _():
            ext[0:halo, :] = zeros_halo
            ext[halo:halo + ch, :] = xc[...].astype(F32)
            ext[halo + ch:, :] = zeros_halo

        @pl.when(s > 0)
        def _():
            base = pl.multiple_of(chunk * ch, ch)
            ext[halo:halo + ch, :] = xl[pl.ds(base, ch), :].astype(F32)
            prev = xl[pl.ds(pl.multiple_of(jnp.maximum(base - halo, 0), halo), halo), :].astype(F32)
            nxt = xl[pl.ds(pl.multiple_of(jnp.minimum(base + ch, SEQ - halo), halo), halo), :].astype(F32)
            ext[0:halo, :] = jnp.where(chunk > 0, prev, 0.0)
            ext[halo + ch:, :] = jnp.where(chunk < n_lat - 1, nxt, 0.0)

    chunk_f = jnp.maximum(s - 1, 0)
    chunk_b = jnp.maximum(n_lat - s, 0)
    fill(ext_f, chunk_f)
    fill(ext_b, chunk_b)

    def coeffs(ext, d, a_out, u_out):
        xconv = cb[...] + ext[halo - 1:halo - 1 + ch, :] * cw[0:1, :]
        for j in range(1, LRU_CONV):
            xconv = xconv + ext[halo - 1 + j:halo - 1 + j + ch, :] * cw[j:j + 1, :]
        sp = _softplus(-lam[d:d + 1, :])
        for g in range(LRU_BLOCKS):
            sl = slice(g * LRU_BLOCK_DIM, (g + 1) * LRU_BLOCK_DIM)
            xg = xconv[:, sl]
            xb = xg.astype(BF16)
            r = _sigmoid(_dot(xb, wa[d, g]) + ba[d:d + 1, sl])
            i = _sigmoid(_dot(xb, wx[d, g]) + bx[d:d + 1, sl])
            log_a = -LRU_C * r * sp[:, sl]
            a = jnp.exp(log_a)
            a_out[:, sl] = a
            u_out[:, sl] = jnp.sqrt(jnp.tanh(-log_a) * (a * a + 1.0)) * i * xg

    coeffs(ext_f, 0, a_f, u_f)
    coeffs(ext_b, 1, a_b, u_b)

    @pl.when(s == 0)
    def _():
        carry[...] = jnp.zeros_like(carry)

    row = lax.broadcasted_iota(jnp.int32, (SUBLANES, BRANCH_W), 0)
    n_tiles = ch // SUBLANES

    def scan_tiles(hf_out, hb_out):
        def body(t, hs):
            h_f, h_b = hs
            r0 = pl.multiple_of(t * SUBLANES, SUBLANES)
            a = a_f[pl.ds(r0, SUBLANES), :]
            u = u_f[pl.ds(r0, SUBLANES), :]
            u = jnp.where(row == 0, u + a * h_f, u)
            for d in (1, 2, 4):
                u = jnp.where(row >= d, a * pltpu.roll(u, d, 0) + u, u)
                if d < 4:
                    a = a * pltpu.roll(a, d, 0)
            h = u
            hf_out[pl.ds(r0, SUBLANES), :] = h
            h_f = h[SUBLANES - 1:SUBLANES, :]
            r1 = pl.multiple_of((n_tiles - 1 - t) * SUBLANES, SUBLANES)
            a = a_b[pl.ds(r1, SUBLANES), :]
            u = u_b[pl.ds(r1, SUBLANES), :]
            u = jnp.where(row == SUBLANES - 1, u + a * h_b, u)
            for d in (1, 2, 4):
                u = jnp.where(row < SUBLANES - d, a * pltpu.roll(u, SUBLANES - d, 0) + u, u)
                if d < 4:
                    a = a * pltpu.roll(a, SUBLANES - d, 0)
            h = u
            hb_out[pl.ds(r1, SUBLANES), :] = h
            h_b = h[0:1, :]
            return h_f, h_b

        h_f, h_b = lax.fori_loop(0, n_tiles, body, (carry[0:1, :], carry[1:2, :]))
        carry[0:1, :] = h_f
        carry[1:2, :] = h_b

    @pl.when(s == 0)
    def _():
        scan_tiles(hfc, hbc)

    @pl.when(s > 0)
    def _():
        scan_tiles(hfl, hbl)


def _lru(p_lat, p_ctx, conv_w, conv_b, w_a, b_a, w_x, b_x, lam):
    ch = LRU_CHUNK
    n_lat = SEQ // ch
    full = lambda shape: pl.BlockSpec(shape, lambda b, s: (0,) * len(shape))
    wblk = (2, LRU_BLOCKS, LRU_BLOCK_DIM, LRU_BLOCK_DIM)
    return pl.pallas_call(
        _lru_kernel,
        grid=(BATCH, LRU_STEPS),
        in_specs=[
            pl.BlockSpec((SEQ, BRANCH_W), lambda b, s: (b, COL_LX // BRANCH_W)),
            pl.BlockSpec((CTX_LEN, BRANCH_W), lambda b, s: (b, COL_LX // BRANCH_W)),
            full((LRU_CONV, BRANCH_W)), full((1, BRANCH_W)),
            full(wblk), full((2, BRANCH_W)), full(wblk), full((2, BRANCH_W)), full((2, BRANCH_W)),
        ],
        out_specs=[
            pl.BlockSpec((ch, BRANCH_W), lambda b, s: (b * n_lat + jnp.maximum(s - 1, 0), 0)),
            pl.BlockSpec((ch, BRANCH_W), lambda b, s: (b * n_lat + jnp.minimum(n_lat - s, n_lat - 1), 0)),
            pl.BlockSpec((CTX_LEN, BRANCH_W), lambda b, s: (b, 0)),
            pl.BlockSpec((CTX_LEN, BRANCH_W), lambda b, s: (b, 0)),
        ],
        out_shape=[
            jax.ShapeDtypeStruct((BATCH * SEQ, BRANCH_W), F32),
            jax.ShapeDtypeStruct((BATCH * SEQ, BRANCH_W), F32),
            jax.ShapeDtypeStruct((BATCH * CTX_LEN, BRANCH_W), F32),
            jax.ShapeDtypeStruct((BATCH * CTX_LEN, BRANCH_W), F32),
        ],
        scratch_shapes=[
            pltpu.VMEM((ch + 2 * LRU_HALO, BRANCH_W), F32), pltpu.VMEM((ch + 2 * LRU_HALO, BRANCH_W), F32),
            pltpu.VMEM((ch, BRANCH_W), F32), pltpu.VMEM((ch, BRANCH_W), F32),
            pltpu.VMEM((ch, BRANCH_W), F32), pltpu.VMEM((ch, BRANCH_W), F32),
            pltpu.VMEM((SUBLANES, BRANCH_W), F32),
        ],
        compiler_params=_cparams(("arbitrary", "arbitrary")),
        name="rglru",
    )(p_lat, p_ctx, conv_w, conv_b.reshape(1, BRANCH_W), w_a, b_a, w_x, b_x, lam)


def _merge_kernel(ym, yn, hf, hb, lg, g0, g1, g2, wb, o_ref, yl_s):
    j = pl.program_id(1)

    @pl.when(j == 0)
    def _():
        yl_s[...] = ((hf[...] + hb[...]) * jax.nn.gelu(lg[...].astype(F32))).astype(BF16)

    acc = _sigmoid(g0[...].astype(F32)) * _dot(ym[...], wb[0])
    acc = acc + _sigmoid(g1[...].astype(F32)) * _dot(yn[...], wb[1])
    acc = acc + _sigmoid(g2[...].astype(F32)) * _dot(yl_s[...], wb[2])
    o_ref[...] = acc.astype(BF16)


def _merge(ym, yn, hf, hb, p, li, w_branch):
    rows = ym.shape[0]
    tn = TN_MERGE
    tm = TM_MERGE
    br = lambda: pl.BlockSpec((tm, BRANCH_W), lambda i, j: (i, 0))

    def gate(n):
        return pl.BlockSpec((tm, tn), lambda i, j: (i, (COL_GX + n * D_MODEL) // tn + j))

    return pl.pallas_call(
        _merge_kernel,
        grid=(rows // tm, D_MODEL // tn),
        in_specs=[
            br(), br(), br(), br(),
            pl.BlockSpec((tm, BRANCH_W), lambda i, j: (i, COL_LG // BRANCH_W)),
            gate(0), gate(1), gate(2),
            pl.BlockSpec((None, N_BRANCH, BRANCH_W, tn), lambda i, j: (li, 0, 0, j)),
        ],
        out_specs=pl.BlockSpec((tm, tn), lambda i, j: (i, j)),
        out_shape=jax.ShapeDtypeStruct((rows, D_MODEL), BF16),
        scratch_shapes=[pltpu.VMEM((tm, BRANCH_W), BF16)],
        compiler_params=_cparams(("arbitrary", "arbitrary")),
        name="merge",
    )(ym, yn, hf, hb, p, p, p, p, w_branch)


def _split_w_in(w):
    n_gate = 4 * M_HEADS
    o_mg = COL_NQ
    wbf = w.astype(BF16)
    wa = wbf[:, :, :o_mg]
    wb = wbf[:, :, o_mg + n_gate:]
    mg = w[:, :, o_mg:o_mg + n_gate].reshape(DEPTH, D_MODEL, 2, 2, M_HEADS)
    mg = jnp.transpose(mg, (0, 4, 2, 3, 1)).reshape(DEPTH, M_HEADS, 4, D_MODEL)
    wgt = jnp.pad(mg, ((0, 0), (0, 0), (0, GATE_R - 4), (0, 0)))
    return wa, wb, wgt.reshape(DEPTH, M_HEADS * GATE_R, D_MODEL).astype(BF16)


def _gate_bias(b_i, b_f):
    b = jnp.stack([b_i, b_f], axis=1)
    b = jnp.transpose(b, (2, 0, 1)).reshape(M_HEADS, 4).astype(F32)
    return jnp.pad(b, ((0, 0), (0, GATE_R - 4))).reshape(M_HEADS * GATE_R, 1)


def kernel(x, c, ctx, c_ctx, w_mod, b_mod, norm_ffn1, norm_mix, norm_ffn2, ffn1_w_in, ffn1_w_out, ffn2_w_in,
           ffn2_w_out, w_in, mlstm_b_i, mlstm_b_f, mlstm_gn, na_rpb, lru_conv_w, lru_conv_b, lru_w_a, lru_b_a,
           lru_w_x, lru_b_x, lru_lambda, w_branch, w_out, norm_final):
    assert x.shape == (BATCH, SEQ, D_MODEL) and ctx.shape == (BATCH, CTX_LEN, D_MODEL)
    xl = x.reshape(BATCH * SEQ, D_MODEL)
    xc = ctx.reshape(BATCH * CTX_LEN, D_MODEL)
    c_rows = jnp.concatenate([c, c_ctx[None, :], jnp.zeros((MOD_ROWS - BATCH - 1, D_MODEL), F32)], axis=0)
    mod = _modulation(c_rows, w_mod, b_mod).reshape(DEPTH, MOD_ROWS, N_MOD, 1, D_MODEL)
    cos, sin = _rope_tables()
    w1_in, w1_out = ffn1_w_in[0].astype(BF16), ffn1_w_out[0].astype(BF16)
    w_head, w_tail, w_gate = _split_w_in(w_in)
    wb = w_branch.astype(BF16)
    wo = w_out.astype(BF16)

    for li in range(DEPTH):
        last = li == DEPTH - 1
        wgt = w_gate[li]
        bgt = _gate_bias(mlstm_b_i[li], mlstm_b_f[li])
        bias = _na_bias(na_rpb[li])
        wa, wx = lru_w_a[li].astype(BF16), lru_w_x[li].astype(BF16)

        xl, w2_in, w2_out = _ffn(xl, li, mod, 0, norm_ffn1[li], w1_in, w1_out, _lat_row,
                                 cast_job=(ffn2_w_in, ffn2_w_out, li))
        xc = _ffn(xc, li, mod, 0, norm_ffn1[li], w1_in, w1_out, _ctx_row)

        p_lat, gt_lat = _inproj(xl, li, mod, norm_mix[li], w_head, w_tail, wgt, bgt, _lat_row)
        p_ctx, gt_ctx = _inproj(xc, li, mod, norm_mix[li], w_head, w_tail, wgt, bgt, _ctx_row)

        ym_l, ym_c = _mlstm(p_lat, p_ctx, gt_lat, gt_ctx, cos, sin, mlstm_gn[li])
        yn_l, yn_c = _na(p_lat, p_ctx, bias)
        hf_l, hb_l, hf_c, hb_c = _lru(p_lat, p_ctx, lru_conv_w[li], lru_conv_b[li], wa, lru_b_a[li], wx,
                                      lru_b_x[li], lru_lambda[li])

        s_l = _merge(ym_l, yn_l, hf_l, hb_l, p_lat, li, wb)
        if last:
            xl = _ffn(xl, li, mod, 6, norm_ffn2[li], w2_in, w2_out, _lat_row, norm_final, pre=(s_l, wo))
        else:
            xl, w1_in, w1_out = _ffn(xl, li, mod, 6, norm_ffn2[li], w2_in, w2_out, _lat_row,
                                     cast_job=(ffn1_w_in, ffn1_w_out, li + 1), pre=(s_l, wo))
            s_c = _merge(ym_c, yn_c, hf_c, hb_c, p_ctx, li, wb)
            xc = _ffn(xc, li, mod, 6, norm_ffn2[li], w2_in, w2_out, _ctx_row, pre=(s_c, wo))
    return xl.reshape(BATCH, SEQ, D_MODEL)
```
